```python
import jax, jax.numpy as jnp
from jax import lax
import numpy as np

D_MODEL = 2048
BATCH = 8
SEQ = 4096
DEPTH = 2

N_A_LAYERS = DEPTH // 2
N_B_LAYERS = DEPTH - N_A_LAYERS
A_HEADS = 16
A_KEY_DIM = D_MODEL // A_HEADS
A_VAL_DIM = D_MODEL // A_HEADS
A_CHUNK = 16
B_HEADS = 16
B_KV_HEADS = 4
B_HEAD_DIM = D_MODEL // B_HEADS
B_GROUP = B_HEADS // B_KV_HEADS
Q_BLOCK = 128
D_FF = 4 * D_MODEL
NORM_EPS = 1e-6

kernel_name = 'yoco_hgrn2_stickbreaking_adaln_block'


def rms_norm(x, gain):
    xf = x.astype(jnp.float32)
    inv = lax.rsqrt(jnp.mean(xf * xf, axis=-1, keepdims=True) + NORM_EPS)
    return (xf * inv).astype(x.dtype) * gain


def modulate(h, shift, scale):
    return h * (1 + scale[:, None, :]) + shift[:, None, :]


def squared_relu_mlp(h, w1, w2):
    a = jax.nn.relu(h @ w1)
    return (a * a) @ w2


def hgrn2_mixer(h, w_in, lb, out_gain, w_out):
    b, s, _ = h.shape
    proj = h @ w_in
    q, f_logit, i_in, g = jnp.split(proj, 4, axis=-1)
    q = jax.nn.silu(q.astype(jnp.float32))
    f_logit = f_logit.astype(jnp.float32)
    f = lb + (1 - lb) * jax.nn.sigmoid(f_logit)
    log_f = jnp.log(f)
    k = (1 - lb) * jax.nn.sigmoid(-f_logit)
    v = i_in.astype(jnp.float32)
    nc = s // A_CHUNK
    shp = (b, nc, A_CHUNK, A_HEADS, A_KEY_DIM)
    q = q.reshape(shp)
    k = k.reshape(shp)
    log_f = log_f.reshape(shp)
    v = v.reshape(b, nc, A_CHUNK, A_HEADS, A_VAL_DIM)
    cum = jnp.cumsum(log_f, axis=2)
    q_dec = q * jnp.exp(cum)
    k_intra = k * jnp.exp(-cum)
    k_state = k * jnp.exp(cum[:, :, -1:] - cum)
    chunk_decay = jnp.exp(cum[:, :, -1])
    causal = jnp.tril(jnp.ones((A_CHUNK, A_CHUNK), dtype=bool))
    scores = jnp.einsum('bnthk,bnshk->bnhts', q_dec, k_intra)
    scores = jnp.where(causal, scores, 0.0)
    o_intra = jnp.einsum('bnhts,bnshv->bnthv', scores, v)

    def step(state, inp):
        q_c, k_c, v_c, dec_c = inp
        o_c = jnp.einsum('bthk,bhkv->bthv', q_c, state)
        state = dec_c[..., None] * state + jnp.einsum('bthk,bthv->bhkv', k_c, v_c)
        return state, o_c

    xs = (jnp.moveaxis(q_dec, 1, 0), jnp.moveaxis(k_state, 1, 0),
          jnp.moveaxis(v, 1, 0), jnp.moveaxis(chunk_decay, 1, 0))
    s0 = jnp.zeros((b, A_HEADS, A_KEY_DIM, A_VAL_DIM), jnp.float32)
    _, o_inter = lax.scan(step, s0, xs)
    o = (o_intra + jnp.moveaxis(o_inter, 0, 1)).reshape(b, s, A_HEADS, A_VAL_DIM)
    o = o * lax.rsqrt(jnp.mean(o * o, axis=-1, keepdims=True) + NORM_EPS)
    o = o.reshape(b, s, A_HEADS * A_VAL_DIM) * out_gain
    o = o * jax.nn.silu(g.astype(jnp.float32))
    return o.astype(h.dtype) @ w_out


def stick_breaking_mixer(h, w_q, k, v, w_out):
    b, s, _ = h.shape
    q = (h @ w_q).reshape(b, s, B_KV_HEADS, B_GROUP, B_HEAD_DIM)
    scale = B_HEAD_DIM ** -0.5
    outs = []
    for blk in range(s // Q_BLOCK):
        t0 = blk * Q_BLOCK
        t1 = t0 + Q_BLOCK
        q_blk = q[:, t0:t1]
        k_pre = k[:, :t1]
        v_pre = v[:, :t1]
        z = jnp.einsum('btkgd,bskd->bkgts', q_blk, k_pre).astype(jnp.float32) * scale
        t_pos = t0 + jnp.arange(Q_BLOCK)[:, None]
        s_pos = jnp.arange(t1)[None, :]
        mask = s_pos < t_pos
        log_beta = jax.nn.log_sigmoid(z)
        log_rest = jnp.where(mask, log_beta - z, 0.0)
        between = lax.cumsum(log_rest, axis=4, reverse=True) - log_rest
        weights = jnp.where(mask, jnp.exp(log_beta + between), 0.0)
        outs.append(jnp.einsum('bkgts,bskd->btkgd', weights.astype(v.dtype), v_pre))
    o = jnp.concatenate(outs, axis=1).reshape(b, s, B_HEADS * B_HEAD_DIM)
    return o @ w_out


def _fwd_setup_inputs(seed: int = 0) -> dict:
    key = jax.random.key(seed)
    ks = jax.random.split(key, 20)
    d = D_MODEL
    f32 = jnp.float32

    def nrm(k, shape, fan_in, gain=1.0):
        return jax.random.normal(k, shape, f32) * (gain * fan_in ** -0.5)

    def small(k, shape):
        return 0.02 * jax.random.normal(k, shape, f32)

    return {
        'x': jax.random.normal(ks[0], (BATCH, SEQ, d), f32),
        'c': jax.random.normal(ks[1], (BATCH, d), f32),
        'ada_w': nrm(ks[2], (DEPTH, d, 6 * d), d, 0.5),
        'ada_b': small(ks[3], (DEPTH, 6 * d)),
        'norm_mix': 1.0 + small(ks[4], (DEPTH, d)),
        'norm_mlp': 1.0 + small(ks[5], (DEPTH, d)),
        'a_w_in': nrm(ks[6], (N_A_LAYERS, d, 4 * d), d),
        'a_lb_logits': 0.1 * jax.random.normal(ks[7], (N_A_LAYERS + 1, A_HEADS * A_KEY_DIM), f32),
        'a_out_gain': 1.0 + small(ks[8], (N_A_LAYERS, A_HEADS * A_VAL_DIM)),
        'a_w_out': nrm(ks[9], (N_A_LAYERS, A_HEADS * A_VAL_DIM, d), A_HEADS * A_VAL_DIM),
        'kv_ada_w': nrm(ks[10], (d, 2 * d), d, 0.5),
        'kv_ada_b': small(ks[11], (2 * d,)),
        'kv_norm': 1.0 + small(ks[12], (d,)),
        'w_kv': nrm(ks[13], (d, 2 * B_KV_HEADS * B_HEAD_DIM), d),
        'b_w_q': nrm(ks[14], (N_B_LAYERS, d, B_HEADS * B_HEAD_DIM), d),
        'b_w_out': nrm(ks[15], (N_B_LAYERS, B_HEADS * B_HEAD_DIM, d), B_HEADS * B_HEAD_DIM),
        'mlp_w1': nrm(ks[16], (DEPTH, d, D_FF), d),
        'mlp_w2': nrm(ks[17], (DEPTH, D_FF, d), D_FF),
        'final_norm': 1.0 + small(ks[18], (d,)),
    }


def _fwd_reference(x, c, ada_w, ada_b, norm_mix, norm_mlp, a_w_in, a_lb_logits, a_out_gain, a_w_out,
              kv_ada_w, kv_ada_b, kv_norm, w_kv, b_w_q, b_w_out, mlp_w1, mlp_w2, final_norm):
    b, s, _ = x.shape
    c_act = jax.nn.silu(c)
    lb_all = jnp.cumsum(jax.nn.softmax(a_lb_logits.astype(jnp.float32), axis=0), axis=0)
    k_shared = None
    v_shared = None
    for layer in range(DEPTH):
        mod = c_act @ ada_w[layer] + ada_b[layer]
        sh1, sc1, g1, sh2, sc2, g2 = jnp.split(mod, 6, axis=-1)
        h = modulate(rms_norm(x, norm_mix[layer]), sh1, sc1)
        if layer < N_A_LAYERS:
            y = hgrn2_mixer(h, a_w_in[layer], lb_all[layer], a_out_gain[layer], a_w_out[layer])
        else:
            j = layer - N_A_LAYERS
            y = stick_breaking_mixer(h, b_w_q[j], k_shared, v_shared, b_w_out[j])
        x = x + g1[:, None, :] * y
        h = modulate(rms_norm(x, norm_mlp[layer]), sh2, sc2)
        x = x + g2[:, None, :] * squared_relu_mlp(h, mlp_w1[layer], mlp_w2[layer])
        if layer == N_A_LAYERS - 1:
            kv_sh, kv_sc = jnp.split(c_act @ kv_ada_w + kv_ada_b, 2, axis=-1)
            hk = modulate(rms_norm(x, kv_norm), kv_sh, kv_sc)
            kv = (hk @ w_kv).reshape(b, s, 2, B_KV_HEADS, B_HEAD_DIM)
            k_shared = kv[:, :, 0]
            v_shared = kv[:, :, 1]
    return rms_norm(x, final_norm)


import jax as _jax
import jax.numpy as _jnp

TWIN_FORMAT = 'train_step'
FWD_PARAMS = ['x', 'c', 'ada_w', 'ada_b', 'norm_mix', 'norm_mlp', 'a_w_in', 'a_lb_logits', 'a_out_gain', 'a_w_out', 'kv_ada_w', 'kv_ada_b', 'kv_norm', 'w_kv', 'b_w_q', 'b_w_out', 'mlp_w1', 'mlp_w2', 'final_norm']
TWIN_WEIGHTS = ['ada_w', 'ada_b', 'norm_mix', 'norm_mlp', 'a_w_in', 'a_lb_logits', 'a_out_gain', 'a_w_out', 'kv_ada_w', 'kv_ada_b', 'kv_norm', 'w_kv', 'b_w_q', 'b_w_out', 'mlp_w1', 'mlp_w2', 'final_norm']
TWIN_DIFF_INPUT = 'x'
TWIN_INPUTS = ['x', 'c', 'ada_w', 'ada_b', 'norm_mix', 'norm_mlp', 'a_w_in', 'a_lb_logits', 'a_out_gain', 'a_w_out', 'kv_ada_w', 'kv_ada_b', 'kv_norm', 'w_kv', 'b_w_q', 'b_w_out', 'mlp_w1', 'mlp_w2', 'final_norm', 'loss_target', 'm_ada_w', 'm_ada_b', 'm_norm_mix', 'm_norm_mlp', 'm_a_w_in', 'm_a_lb_logits', 'm_a_out_gain', 'm_a_w_out', 'm_kv_ada_w', 'm_kv_ada_b', 'm_kv_norm', 'm_w_kv', 'm_b_w_q', 'm_b_w_out', 'm_mlp_w1', 'm_mlp_w2', 'm_final_norm', 'v_ada_w', 'v_ada_b', 'v_norm_mix', 'v_norm_mlp', 'v_a_w_in', 'v_a_lb_logits', 'v_a_out_gain', 'v_a_w_out', 'v_kv_ada_w', 'v_kv_ada_b', 'v_kv_norm', 'v_w_kv', 'v_b_w_q', 'v_b_w_out', 'v_mlp_w1', 'v_mlp_w2', 'v_final_norm']
TWIN_OUTPUTS = ['loss', 'grad_x', 'grad_ada_w', 'grad_ada_b', 'grad_norm_mix', 'grad_norm_mlp', 'grad_a_w_in', 'grad_a_lb_logits', 'grad_a_out_gain', 'grad_a_w_out', 'grad_kv_ada_w', 'grad_kv_ada_b', 'grad_kv_norm', 'grad_w_kv', 'grad_b_w_q', 'grad_b_w_out', 'grad_mlp_w1', 'grad_mlp_w2', 'grad_final_norm', 'delta_ada_w', 'delta_ada_b', 'delta_norm_mix', 'delta_norm_mlp', 'delta_a_w_in', 'delta_a_lb_logits', 'delta_a_out_gain', 'delta_a_w_out', 'delta_kv_ada_w', 'delta_kv_ada_b', 'delta_kv_norm', 'delta_w_kv', 'delta_b_w_q', 'delta_b_w_out', 'delta_mlp_w1', 'delta_mlp_w2', 'delta_final_norm', 'new_m_ada_w', 'new_m_ada_b', 'new_m_norm_mix', 'new_m_norm_mlp', 'new_m_a_w_in', 'new_m_a_lb_logits', 'new_m_a_out_gain', 'new_m_a_w_out', 'new_m_kv_ada_w', 'new_m_kv_ada_b', 'new_m_kv_norm', 'new_m_w_kv', 'new_m_b_w_q', 'new_m_b_w_out', 'new_m_mlp_w1', 'new_m_mlp_w2', 'new_m_final_norm', 'new_v_ada_w', 'new_v_ada_b', 'new_v_norm_mix', 'new_v_norm_mlp', 'new_v_a_w_in', 'new_v_a_lb_logits', 'new_v_a_out_gain', 'new_v_a_w_out', 'new_v_kv_ada_w', 'new_v_kv_ada_b', 'new_v_kv_norm', 'new_v_w_kv', 'new_v_b_w_q', 'new_v_b_w_out', 'new_v_mlp_w1', 'new_v_mlp_w2', 'new_v_final_norm']
TWIN_LEAF_KINDS = {'loss': 'loss', 'grad_x': 'grad_x', 'grad_ada_w': 'grad_w', 'grad_ada_b': 'grad_w', 'grad_norm_mix': 'grad_w', 'grad_norm_mlp': 'grad_w', 'grad_a_w_in': 'grad_w', 'grad_a_lb_logits': 'grad_w', 'grad_a_out_gain': 'grad_w', 'grad_a_w_out': 'grad_w', 'grad_kv_ada_w': 'grad_w', 'grad_kv_ada_b': 'grad_w', 'grad_kv_norm': 'grad_w', 'grad_w_kv': 'grad_w', 'grad_b_w_q': 'grad_w', 'grad_b_w_out': 'grad_w', 'grad_mlp_w1': 'grad_w', 'grad_mlp_w2': 'grad_w', 'grad_final_norm': 'grad_w', 'delta_ada_w': 'delta_w', 'delta_ada_b': 'delta_w', 'delta_norm_mix': 'delta_w', 'delta_norm_mlp': 'delta_w', 'delta_a_w_in': 'delta_w', 'delta_a_lb_logits': 'delta_w', 'delta_a_out_gain': 'delta_w', 'delta_a_w_out': 'delta_w', 'delta_kv_ada_w': 'delta_w', 'delta_kv_ada_b': 'delta_w', 'delta_kv_norm': 'delta_w', 'delta_w_kv': 'delta_w', 'delta_b_w_q': 'delta_w', 'delta_b_w_out': 'delta_w', 'delta_mlp_w1': 'delta_w', 'delta_mlp_w2': 'delta_w', 'delta_final_norm': 'delta_w', 'new_m_ada_w': 'new_m', 'new_m_ada_b': 'new_m', 'new_m_norm_mix': 'new_m', 'new_m_norm_mlp': 'new_m', 'new_m_a_w_in': 'new_m', 'new_m_a_lb_logits': 'new_m', 'new_m_a_out_gain': 'new_m', 'new_m_a_w_out': 'new_m', 'new_m_kv_ada_w': 'new_m', 'new_m_kv_ada_b': 'new_m', 'new_m_kv_norm': 'new_m', 'new_m_w_kv': 'new_m', 'new_m_b_w_q': 'new_m', 'new_m_b_w_out': 'new_m', 'new_m_mlp_w1': 'new_m', 'new_m_mlp_w2': 'new_m', 'new_m_final_norm': 'new_m', 'new_v_ada_w': 'new_v', 'new_v_ada_b': 'new_v', 'new_v_norm_mix': 'new_v', 'new_v_norm_mlp': 'new_v', 'new_v_a_w_in': 'new_v', 'new_v_a_lb_logits': 'new_v', 'new_v_a_out_gain': 'new_v', 'new_v_a_w_out': 'new_v', 'new_v_kv_ada_w': 'new_v', 'new_v_kv_ada_b': 'new_v', 'new_v_kv_norm': 'new_v', 'new_v_w_kv': 'new_v', 'new_v_b_w_q': 'new_v', 'new_v_b_w_out': 'new_v', 'new_v_mlp_w1': 'new_v', 'new_v_mlp_w2': 'new_v', 'new_v_final_norm': 'new_v'}


def _forward(args):
    return _fwd_reference(*[args[k] for k in FWD_PARAMS])


def _output_shape():
    out = _jax.eval_shape(lambda: _forward(_fwd_setup_inputs(0)))
    return out.shape, out.dtype

N_MICROBATCH = 1
ADAM_LR = 0.001
ADAM_B1 = 0.9
ADAM_B2 = 0.999
ADAM_EPS = 1e-08
ADAM_WD = 0.01
ADAM_STEP = 10
PER_EXAMPLE_BATCH_AXIS = {'x': 0, 'c': 0, 'loss_target': 0}
SHARED_INPUTS = []
_WEIGHT_DTYPES = {'ada_w': _jnp.float32, 'ada_b': _jnp.float32, 'norm_mix': _jnp.float32, 'norm_mlp': _jnp.float32, 'a_w_in': _jnp.float32, 'a_lb_logits': _jnp.float32, 'a_out_gain': _jnp.float32, 'a_w_out': _jnp.float32, 'kv_ada_w': _jnp.float32, 'kv_ada_b': _jnp.float32, 'kv_norm': _jnp.float32, 'w_kv': _jnp.float32, 'b_w_q': _jnp.float32, 'b_w_out': _jnp.float32, 'mlp_w1': _jnp.float32, 'mlp_w2': _jnp.float32, 'final_norm': _jnp.float32}
MOMENT_SCALE = {'ada_w': 4.009111e-02, 'ada_b': 7.192483e-02, 'norm_mix': 1.754439e-02, 'norm_mlp': 4.019781e-02, 'a_w_in': 1.311472e-02, 'a_lb_logits': 1.546537e-03, 'a_out_gain': 1.846450e-02, 'a_w_out': 1.808401e-02, 'kv_ada_w': 1.333263e-02, 'kv_ada_b': 2.246580e-02, 'kv_norm': 1.846851e-02, 'w_kv': 2.888390e-02, 'b_w_q': 8.344489e-03, 'b_w_out': 1.913785e-02, 'mlp_w1': 1.968635e-02, 'mlp_w2': 3.558160e-02, 'final_norm': 1.616008e+01}


def _to_microbatches(a, axis):
    t = _jnp.moveaxis(a, axis, 0)
    t = t.reshape((N_MICROBATCH, t.shape[0] // N_MICROBATCH) + t.shape[1:])
    return _jnp.moveaxis(t, 1, axis + 1)


def setup_inputs(seed: int = 0) -> dict:
    inp = _fwd_setup_inputs(seed)
    key = _jax.random.fold_in(_jax.random.key(seed), 7919)
    shape, _ = _output_shape()
    out = dict(inp)
    out["loss_target"] = _jax.random.normal(_jax.random.fold_in(key, 0), shape, _jnp.float32)
    for i, name in enumerate(TWIN_WEIGHTS):
        w = inp[name].astype(_jnp.float32)
        if MOMENT_SCALE is None:
            s = _jnp.sqrt(_jnp.mean(_jnp.square(w)) + 1e-30)
        else:
            s = MOMENT_SCALE[name]
        km, kv = _jax.random.split(_jax.random.fold_in(key, i + 1))
        out[name] = w
        out["m_" + name] = s * _jax.random.normal(km, w.shape, _jnp.float32)
        out["v_" + name] = (s * s) * _jax.random.uniform(kv, w.shape, _jnp.float32, 0.5, 1.5)
    if N_MICROBATCH > 1:
        for name, axis in PER_EXAMPLE_BATCH_AXIS.items():
            out[name] = _to_microbatches(out[name], axis)
    return {'x': out['x'], 'c': out['c'], 'ada_w': out['ada_w'], 'ada_b': out['ada_b'], 'norm_mix': out['norm_mix'], 'norm_mlp': out['norm_mlp'], 'a_w_in': out['a_w_in'], 'a_lb_logits': out['a_lb_logits'], 'a_out_gain': out['a_out_gain'], 'a_w_out': out['a_w_out'], 'kv_ada_w': out['kv_ada_w'], 'kv_ada_b': out['kv_ada_b'], 'kv_norm': out['kv_norm'], 'w_kv': out['w_kv'], 'b_w_q': out['b_w_q'], 'b_w_out': out['b_w_out'], 'mlp_w1': out['mlp_w1'], 'mlp_w2': out['mlp_w2'], 'final_norm': out['final_norm'], 'loss_target': out['loss_target'], 'm_ada_w': out['m_ada_w'], 'm_ada_b': out['m_ada_b'], 'm_norm_mix': out['m_norm_mix'], 'm_norm_mlp': out['m_norm_mlp'], 'm_a_w_in': out['m_a_w_in'], 'm_a_lb_logits': out['m_a_lb_logits'], 'm_a_out_gain': out['m_a_out_gain'], 'm_a_w_out': out['m_a_w_out'], 'm_kv_ada_w': out['m_kv_ada_w'], 'm_kv_ada_b': out['m_kv_ada_b'], 'm_kv_norm': out['m_kv_norm'], 'm_w_kv': out['m_w_kv'], 'm_b_w_q': out['m_b_w_q'], 'm_b_w_out': out['m_b_w_out'], 'm_mlp_w1': out['m_mlp_w1'], 'm_mlp_w2': out['m_mlp_w2'], 'm_final_norm': out['m_final_norm'], 'v_ada_w': out['v_ada_w'], 'v_ada_b': out['v_ada_b'], 'v_norm_mix': out['v_norm_mix'], 'v_norm_mlp': out['v_norm_mlp'], 'v_a_w_in': out['v_a_w_in'], 'v_a_lb_logits': out['v_a_lb_logits'], 'v_a_out_gain': out['v_a_out_gain'], 'v_a_w_out': out['v_a_w_out'], 'v_kv_ada_w': out['v_kv_ada_w'], 'v_kv_ada_b': out['v_kv_ada_b'], 'v_kv_norm': out['v_kv_norm'], 'v_w_kv': out['v_w_kv'], 'v_b_w_q': out['v_b_w_q'], 'v_b_w_out': out['v_b_w_out'], 'v_mlp_w1': out['v_mlp_w1'], 'v_mlp_w2': out['v_mlp_w2'], 'v_final_norm': out['v_final_norm']}


def _loss(weights, diff, rest, loss_target):
    with _jax.named_scope("forward"):
        args = {**rest, TWIN_DIFF_INPUT: diff, **{k: w.astype(_WEIGHT_DTYPES[k]) for k, w in weights.items()}}
        y = _forward(args)
    with _jax.named_scope("loss_head"):
        err = _jnp.square(y.astype(_jnp.float32) - loss_target)
        return 0.5 * _jnp.sum(_jnp.mean(err, axis=-1)) if err.ndim else 0.5 * err


def _adamw(w, g, m, v):
    m = ADAM_B1 * m + (1.0 - ADAM_B1) * g
    v = ADAM_B2 * v + (1.0 - ADAM_B2) * _jnp.square(g)
    m_hat = m / (1.0 - ADAM_B1 ** ADAM_STEP)
    v_hat = v / (1.0 - ADAM_B2 ** ADAM_STEP)
    delta = -ADAM_LR * (m_hat / (_jnp.sqrt(v_hat) + ADAM_EPS) + ADAM_WD * w)
    return delta, m, v


def reference(x, c, ada_w, ada_b, norm_mix, norm_mlp, a_w_in, a_lb_logits, a_out_gain, a_w_out, kv_ada_w, kv_ada_b, kv_norm, w_kv, b_w_q, b_w_out, mlp_w1, mlp_w2, final_norm, loss_target, m_ada_w, m_ada_b, m_norm_mix, m_norm_mlp, m_a_w_in, m_a_lb_logits, m_a_out_gain, m_a_w_out, m_kv_ada_w, m_kv_ada_b, m_kv_norm, m_w_kv, m_b_w_q, m_b_w_out, m_mlp_w1, m_mlp_w2, m_final_norm, v_ada_w, v_ada_b, v_norm_mix, v_norm_mlp, v_a_w_in, v_a_lb_logits, v_a_out_gain, v_a_w_out, v_kv_ada_w, v_kv_ada_b, v_kv_norm, v_w_kv, v_b_w_q, v_b_w_out, v_mlp_w1, v_mlp_w2, v_final_norm):
    given = dict(x=x, c=c, ada_w=ada_w, ada_b=ada_b, norm_mix=norm_mix, norm_mlp=norm_mlp, a_w_in=a_w_in, a_lb_logits=a_lb_logits, a_out_gain=a_out_gain, a_w_out=a_w_out, kv_ada_w=kv_ada_w, kv_ada_b=kv_ada_b, kv_norm=kv_norm, w_kv=w_kv, b_w_q=b_w_q, b_w_out=b_w_out, mlp_w1=mlp_w1, mlp_w2=mlp_w2, final_norm=final_norm, loss_target=loss_target, m_ada_w=m_ada_w, m_ada_b=m_ada_b, m_norm_mix=m_norm_mix, m_norm_mlp=m_norm_mlp, m_a_w_in=m_a_w_in, m_a_lb_logits=m_a_lb_logits, m_a_out_gain=m_a_out_gain, m_a_w_out=m_a_w_out, m_kv_ada_w=m_kv_ada_w, m_kv_ada_b=m_kv_ada_b, m_kv_norm=m_kv_norm, m_w_kv=m_w_kv, m_b_w_q=m_b_w_q, m_b_w_out=m_b_w_out, m_mlp_w1=m_mlp_w1, m_mlp_w2=m_mlp_w2, m_final_norm=m_final_norm, v_ada_w=v_ada_w, v_ada_b=v_ada_b, v_norm_mix=v_norm_mix, v_norm_mlp=v_norm_mlp, v_a_w_in=v_a_w_in, v_a_lb_logits=v_a_lb_logits, v_a_out_gain=v_a_out_gain, v_a_w_out=v_a_w_out, v_kv_ada_w=v_kv_ada_w, v_kv_ada_b=v_kv_ada_b, v_kv_norm=v_kv_norm, v_w_kv=v_w_kv, v_b_w_q=v_b_w_q, v_b_w_out=v_b_w_out, v_mlp_w1=v_mlp_w1, v_mlp_w2=v_mlp_w2, v_final_norm=v_final_norm)
    weights = {n: given[n] for n in TWIN_WEIGHTS}
    shared = {n: given[n] for n in SHARED_INPUTS}
    per_example = {n: given[n] for n in ['x', 'c']}
    grad_fn = _jax.value_and_grad(_loss, argnums=(0, 1))

    def one_microbatch(ex, loss_target):
        ex = dict(ex)
        diff = ex.pop(TWIN_DIFF_INPUT)
        return grad_fn(weights, diff, {**shared, **ex}, loss_target)

    if N_MICROBATCH == 1:
        loss, (grad_w, grad_x) = one_microbatch(per_example, given["loss_target"])
    else:
        def body(carry, xs):
            loss_sum, grad_sum = carry
            l_k, (gw_k, gx_k) = one_microbatch(xs[0], xs[1])
            with _jax.named_scope("update"):
                return (loss_sum + l_k, _jax.tree.map(_jnp.add, grad_sum, gw_k)), gx_k

        init = (_jnp.zeros((), _jnp.float32), _jax.tree.map(_jnp.zeros_like, weights))
        (loss, grad_w), grad_x = _jax.lax.scan(body, init, (per_example, given["loss_target"]))
    with _jax.named_scope("update"):
        delta_w, new_m, new_v = {}, {}, {}
        for n in TWIN_WEIGHTS:
            delta_w[n], new_m[n], new_v[n] = _adamw(weights[n], grad_w[n], given["m_" + n], given["v_" + n])
    return (loss, grad_x, *[grad_w[n] for n in TWIN_WEIGHTS], *[delta_w[n] for n in TWIN_WEIGHTS],
            *[new_m[n] for n in TWIN_WEIGHTS], *[new_v[n] for n in TWIN_WEIGHTS])
```

```python
import functools

import jax
import jax.numpy as jnp
from jax import lax
from jax.experimental import pallas as pl
from jax.experimental.pallas import tpu as pltpu

F32 = jnp.float32
BF16 = jnp.bfloat16

N_DEV = 8
MESH_AXES = ("x", "y", "c")
HEAD_DIM = 128
A_CHUNK = 16
Q_BLOCK = 128
NORM_EPS = 1e-6
ADAM_LR = 0.001
ADAM_B1 = 0.9
ADAM_B2 = 0.999
ADAM_EPS = 1e-08
ADAM_WD = 0.01
ADAM_STEP = 10
VMEM_LIMIT = 48 * 1024 * 1024


def _pick(dim, target, align):
    t = (min(dim, target) // align) * align
    while t >= align:
        if dim % t == 0:
            return t
        t -= align
    return dim


def _sigmoid(x):
    return 1.0 / (1.0 + jnp.exp(-x))


def _silu(x):
    return x * _sigmoid(x)


def _dsilu(x):
    s = _sigmoid(x)
    return s * (1.0 + x * (1.0 - s))


def _dot(a, b, dims):
    return lax.dot_general(a, b, (dims, ((), ())), preferred_element_type=F32)


def _dot_nn(a, b):
    return _dot(a, b, ((1,), (0,)))


def _dot_nt(a, b):
    return _dot(a, b, ((1,), (1,)))


def _dot_tn(a, b):
    return _dot(a, b, ((0,), (0,)))


def _mesh_pos():
    return lax.axis_index("x"), lax.axis_index("y"), lax.axis_index("c")


def _flip(v, d):
    return 1 - v if d else v


def _all_gather(x, out_shape, place, name):
    def body(x_ref, out_ref, send_sems, recv_sems, local_sem):
        x, y, c = _mesh_pos()
        me, sibling = (x, y, c), (x, y, 1 - c)
        chips = [(1 - x, y), (x, 1 - y), (1 - x, 1 - y)]

        def rows(px, py, pc):
            return out_ref.at[place(4 * px + 2 * py + pc)]

        def copy(k, block, to, src=None):
            return pltpu.make_async_remote_copy(
                src_ref=rows(*block) if src is None else src,
                dst_ref=rows(*block),
                send_sem=send_sems.at[k],
                recv_sem=recv_sems.at[k],
                device_id=to,
                device_id_type=pl.DeviceIdType.MESH,
            )

        mine = pltpu.make_async_copy(x_ref, rows(*me), local_sem)
        mine.start()
        first = [copy(0, me, sibling, src=x_ref)]
        first += [copy(1 + j, me, (*chip, c), src=x_ref) for j, chip in enumerate(chips)]
        for cp in first:
            cp.start()
        passed = [copy(4 + j, (*chip, c), sibling) for j, chip in enumerate(chips)]
        for j, chip in enumerate(chips):
            copy(1 + j, (*chip, c), me).wait_recv()
            passed[j].start()
        copy(0, sibling, me).wait_recv()
        for j, chip in enumerate(chips):
            copy(4 + j, (*chip, 1 - c), me).wait_recv()
        for cp in first + passed:
            cp.wait_send()
        mine.wait()

    return pl.pallas_call(
        body,
        name=name,
        out_shape=jax.ShapeDtypeStruct(out_shape, x.dtype),
        in_specs=[pl.BlockSpec(memory_space=pl.ANY)],
        out_specs=pl.BlockSpec(memory_space=pl.ANY),
        scratch_shapes=[
            pltpu.SemaphoreType.DMA((7,)),
            pltpu.SemaphoreType.DMA((7,)),
            pltpu.SemaphoreType.DMA(()),
        ],
    )(x)


def _gather_major(x, name):
    return _all_gather(x, (N_DEV,) + x.shape, lambda d: (d,), name)


def _gather_cols(x, name):
    n = x.shape[-1]
    lead = (slice(None),) * (x.ndim - 1)
    return _all_gather(x, x.shape[:-1] + (N_DEV * n,), lambda d: lead + (pl.ds(d * n, n),), name)


_RELATIONS = [(dx, dy, dc) for dx in (0, 1) for dy in (0, 1) for dc in (0, 1) if (dx, dy, dc) != (0, 0, 0)]


def _exchange(src, blk_shape, src_place, name):
    def body(src_ref, land_ref, send_sems, recv_sems, local_sem):
        x, y, c = _mesh_pos()
        me_id = 4 * x + 2 * y + c
        mine = pltpu.make_async_copy(src_ref.at[src_place(me_id)], land_ref.at[me_id], local_sem)
        mine.start()
        sends, recvs = [], []
        for k, (dx, dy, dc) in enumerate(_RELATIONS):
            px, py, pc = _flip(x, dx), _flip(y, dy), _flip(c, dc)
            pid = 4 * px + 2 * py + pc
            sends.append(pltpu.make_async_remote_copy(
                src_ref=src_ref.at[src_place(pid)], dst_ref=land_ref.at[me_id],
                send_sem=send_sems.at[k], recv_sem=recv_sems.at[k],
                device_id=(px, py, pc), device_id_type=pl.DeviceIdType.MESH))
            recvs.append(pltpu.make_async_remote_copy(
                src_ref=src_ref.at[src_place(me_id)], dst_ref=land_ref.at[pid],
                send_sem=send_sems.at[k], recv_sem=recv_sems.at[k],
                device_id=(px, py, pc), device_id_type=pl.DeviceIdType.MESH))
        for cp in sends:
            cp.start()
        for cp in recvs:
            cp.wait_recv()
        for cp in sends:
            cp.wait_send()
        mine.wait()

    return pl.pallas_call(
        body,
        name=name,
        out_shape=jax.ShapeDtypeStruct((N_DEV,) + tuple(blk_shape), src.dtype),
        in_specs=[pl.BlockSpec(memory_space=pl.ANY)],
        out_specs=pl.BlockSpec(memory_space=pl.ANY),
        scratch_shapes=[
            pltpu.SemaphoreType.DMA((7,)),
            pltpu.SemaphoreType.DMA((7,)),
            pltpu.SemaphoreType.DMA(()),
        ],
    )(src)


def _exchange_cols(g, name):
    n = g.shape[-1] // N_DEV
    lead = (slice(None),) * (g.ndim - 1)
    return _exchange(g, g.shape[:-1] + (n,), lambda d: lead + (pl.ds(d * n, n),), name)


def _exchange_rows(g, name):
    r = g.shape[0] // N_DEV
    g3 = g.reshape(N_DEV, r, g.shape[1])
    return _exchange(g3, (r, g.shape[1]), lambda d: (d,), name)


def _matmul(a, b, mode, out_dtypes, name, epilogue=None, tiles=(), rows=(), tm=512, tn=1024, tk=512):
    if mode == "nn":
        (M, K), (K2, N) = a.shape, b.shape
    elif mode == "nt":
        (M, K), (N, K2) = a.shape, b.shape
    else:
        (K, M), (K2, N) = a.shape, b.shape
    assert K == K2, (a.shape, b.shape, mode)
    tm, tn, tk = _pick(M, tm, 128), _pick(N, tn, 128), _pick(K, tk, 128)
    nk = K // tk
    n_extra = len(tiles) + len(rows)
    n_out = len(out_dtypes)
    if epilogue is None:
        epilogue = lambda acc: (acc,)
    dims = {"nn": ((1,), (0,)), "nt": ((1,), (1,)), "tn": ((0,), (0,))}[mode]

    def kern(*refs):
        a_ref, b_ref = refs[0], refs[1]
        extra = refs[2:2 + n_extra]
        outs = refs[2 + n_extra:2 + n_extra + n_out]
        acc_ref = refs[-1]
        k = pl.program_id(2)

        @pl.when(k == 0)
        def _():
            acc_ref[...] = jnp.zeros_like(acc_ref)

        acc_ref[...] += _dot(a_ref[...].astype(BF16), b_ref[...].astype(BF16), dims)

        @pl.when(k == nk - 1)
        def _():
            vals = epilogue(acc_ref[...], *[r[...] for r in extra])
            for o_ref, v in zip(outs, vals):
                o_ref[...] = v.astype(o_ref.dtype)

    a_spec = {"nn": pl.BlockSpec((tm, tk), lambda i, j, k: (i, k)),
              "nt": pl.BlockSpec((tm, tk), lambda i, j, k: (i, k)),
              "tn": pl.BlockSpec((tk, tm), lambda i, j, k: (k, i))}[mode]
    b_spec = {"nn": pl.BlockSpec((tk, tn), lambda i, j, k: (k, j)),
              "nt": pl.BlockSpec((tn, tk), lambda i, j, k: (j, k)),
              "tn": pl.BlockSpec((tk, tn), lambda i, j, k: (k, j))}[mode]
    tile_spec = pl.BlockSpec((tm, tn), lambda i, j, k: (i, j))
    row_spec = pl.BlockSpec((1, tn), lambda i, j, k: (0, j))
    return pl.pallas_call(
        kern,
        name=name,
        grid=(M // tm, N // tn, nk),
        in_specs=[a_spec, b_spec] + [tile_spec] * len(tiles) + [row_spec] * len(rows),
        out_specs=[tile_spec] * n_out,
        out_shape=[jax.ShapeDtypeStruct((M, N), dt) for dt in out_dtypes],
        scratch_shapes=[pltpu.VMEM((tm, tn), F32)],
        compiler_params=pltpu.CompilerParams(
            dimension_semantics=("parallel", "parallel", "arbitrary"), vmem_limit_bytes=VMEM_LIMIT),
    )(a, b, *tiles, *rows)


def _norm_mod_fwd(x, gain, sc, sh, name):
    S, D = x.shape
    ts = _pick(S, 256, 8)

    def kern(x_ref, g_ref, sc_ref, sh_ref, h_ref):
        xv = x_ref[...]
        inv = lax.rsqrt(jnp.mean(xv * xv, axis=-1, keepdims=True) + NORM_EPS)
        h = (xv * inv) * g_ref[...] * (1.0 + sc_ref[...]) + sh_ref[...]
        h_ref[...] = h.astype(h_ref.dtype)

    row = pl.BlockSpec((1, D), lambda i: (0, 0))
    blk = pl.BlockSpec((ts, D), lambda i: (i, 0))
    return pl.pallas_call(
        kern, name=name, grid=(S // ts,),
        in_specs=[blk, row, row, row], out_specs=blk,
        out_shape=jax.ShapeDtypeStruct((S, D), BF16),
        compiler_params=pltpu.CompilerParams(dimension_semantics=("parallel",), vmem_limit_bytes=VMEM_LIMIT),
    )(x, gain, sc, sh)


def _norm_mod_bwd(x, gain, sc, dh, name):
    S, D = x.shape
    ts = _pick(S, 256, 8)

    def kern(x_ref, g_ref, sc_ref, dh_ref, dx_ref, dg_ref, dsc_ref, dsh_ref):
        @pl.when(pl.program_id(0) == 0)
        def _():
            dg_ref[...] = jnp.zeros_like(dg_ref)
            dsc_ref[...] = jnp.zeros_like(dsc_ref)
            dsh_ref[...] = jnp.zeros_like(dsh_ref)

        xv = x_ref[...]
        dh = dh_ref[...].astype(F32)
        inv = lax.rsqrt(jnp.mean(xv * xv, axis=-1, keepdims=True) + NORM_EPS)
        xn = xv * inv
        g = g_ref[...]
        dsh_ref[...] += jnp.sum(dh, axis=0, keepdims=True)
        dsc_ref[...] += jnp.sum(dh * (xn * g), axis=0, keepdims=True)
        dn = dh * (1.0 + sc_ref[...])
        dg_ref[...] += jnp.sum(dn * xn, axis=0, keepdims=True)
        dxn = dn * g
        dx_ref[...] = inv * (dxn - xn * jnp.mean(dxn * xn, axis=-1, keepdims=True))

    row = pl.BlockSpec((1, D), lambda i: (0, 0))
    blk = pl.BlockSpec((ts, D), lambda i: (i, 0))
    vec = jax.ShapeDtypeStruct((1, D), F32)
    return pl.pallas_call(
        kern, name=name, grid=(S // ts,),
        in_specs=[blk, row, row, blk], out_specs=[blk, row, row, row],
        out_shape=[jax.ShapeDtypeStruct((S, D), F32), vec, vec, vec],
        compiler_params=pltpu.CompilerParams(dimension_semantics=("arbitrary",), vmem_limit_bytes=VMEM_LIMIT),
    )(x, gain, sc, dh)


def _relu2_bwd(du, a, name):
    S, N = a.shape
    ts = _pick(S, 256, 16)

    def kern(du_ref, a_ref, dz_ref):
        dz_ref[...] = (du_ref[...].astype(F32) * (2.0 * a_ref[...].astype(F32))).astype(dz_ref.dtype)

    blk = pl.BlockSpec((ts, N), lambda i: (i, 0))
    return pl.pallas_call(
        kern, name=name, grid=(S // ts,), in_specs=[blk, blk], out_specs=blk,
        out_shape=jax.ShapeDtypeStruct((S, N), BF16),
        compiler_params=pltpu.CompilerParams(dimension_semantics=("parallel",), vmem_limit_bytes=VMEM_LIMIT),
    )(du, a)


def _gate_bwd(d, y, g, name):
    S, D = d.shape
    ts = _pick(S, 256, 16)

    def kern(d_ref, y_ref, g_ref, dy_ref, dg_ref):
        @pl.when(pl.program_id(0) == 0)
        def _():
            dg_ref[...] = jnp.zeros_like(dg_ref)

        dv = d_ref[...]
        dy_ref[...] = (g_ref[...] * dv).astype(dy_ref.dtype)
        dg_ref[...] += jnp.sum(dv * y_ref[...].astype(F32), axis=0, keepdims=True)

    row = pl.BlockSpec((1, D), lambda i: (0, 0))
    blk = pl.BlockSpec((ts, D), lambda i: (i, 0))
    return pl.pallas_call(
        kern, name=name, grid=(S // ts,), in_specs=[blk, blk, row], out_specs=[blk, row],
        out_shape=[jax.ShapeDtypeStruct((S, D), BF16), jax.ShapeDtypeStruct((1, D), F32)],
        compiler_params=pltpu.CompilerParams(dimension_semantics=("arbitrary",), vmem_limit_bytes=VMEM_LIMIT),
    )(d, y, g)


def _final_loss(x, gain, target, name):
    S, D = x.shape
    ts = _pick(S, 256, 8)

    def kern(x_ref, g_ref, t_ref, loss_ref, dx_ref, dg_ref):
        @pl.when(pl.program_id(0) == 0)
        def _():
            loss_ref[...] = jnp.zeros_like(loss_ref)
            dg_ref[...] = jnp.zeros_like(dg_ref)

        xv = x_ref[...]
        g = g_ref[...]
        inv = lax.rsqrt(jnp.mean(xv * xv, axis=-1, keepdims=True) + NORM_EPS)
        xn = xv * inv
        err = xn * g - t_ref[...]
        row_loss = jnp.mean(err * err, axis=-1, keepdims=True)
        loss_ref[...] += 0.5 * jnp.sum(row_loss, axis=0, keepdims=True)
        dy = err * (1.0 / D)
        dg_ref[...] += jnp.sum(dy * xn, axis=0, keepdims=True)
        dxn = dy * g
        dx_ref[...] = inv * (dxn - xn * jnp.mean(dxn * xn, axis=-1, keepdims=True))

    row = pl.BlockSpec((1, D), lambda i: (0, 0))
    blk = pl.BlockSpec((ts, D), lambda i: (i, 0))
    return pl.pallas_call(
        kern, name=name, grid=(S // ts,),
        in_specs=[blk, row, blk],
        out_specs=[pl.BlockSpec((1, 128), lambda i: (0, 0)), blk, row],
        out_shape=[jax.ShapeDtypeStruct((1, 128), F32), jax.ShapeDtypeStruct((S, D), F32),
                   jax.ShapeDtypeStruct((1, D), F32)],
        compiler_params=pltpu.CompilerParams(dimension_semantics=("arbitrary",), vmem_limit_bytes=VMEM_LIMIT),
    )(x, gain, target)


def _chunk_scan(v, ric, reverse):
    n = v.shape[0]
    d = 1
    while d < A_CHUNK:
        if reverse:
            v = v + jnp.where(ric < A_CHUNK - d, pltpu.roll(v, n - d, 0), 0.0)
        else:
            v = v + jnp.where(ric >= d, pltpu.roll(v, d, 0), 0.0)
        d *= 2
    return v


def _hgrn2_gates(q_raw, f_logit, lbv, ric):
    qs = _silu(q_raw)
    sig = _sigmoid(f_logit)
    f = lbv + (1.0 - lbv) * sig
    log_f = jnp.log(f)
    kk = (1.0 - lbv) * _sigmoid(-f_logit)
    cum = _chunk_scan(log_f, ric, False)
    cl = cum + _chunk_scan(log_f, ric, True) - log_f
    e_cum = jnp.exp(cum)
    e_neg = jnp.exp(-cum)
    e_end = jnp.exp(cl - cum)
    dec = jnp.exp(cl)
    return qs, sig, f, kk, e_cum, e_neg, e_end, dec


def _hgrn2_specs(S, D, TB, reverse):
    H = D // HEAD_DIM
    NB = S // TB
    pos = (lambda nb: NB - 1 - nb) if reverse else (lambda nb: nb)
    col = lambda j: pl.BlockSpec((TB, HEAD_DIM), lambda h, nb: (pos(nb), j * H + h))
    head = pl.BlockSpec((TB, HEAD_DIM), lambda h, nb: (pos(nb), h))
    vec = pl.BlockSpec((1, HEAD_DIM), lambda h, nb: (0, h))
    state = pl.BlockSpec((TB // A_CHUNK, None, HEAD_DIM, HEAD_DIM), lambda h, nb: (pos(nb), h, 0, 0))
    return H, NB, col, head, vec, state


def _hgrn2_fwd(proj, lb, gain, name):
    S, D4 = proj.shape
    D = D4 // 4
    TB = _pick(S, 512, A_CHUNK)
    H, NB, col, head, vec, state = _hgrn2_specs(S, D, TB, False)
    NCB = TB // A_CHUNK

    def kern(q_ref, f_ref, i_ref, g_ref, lb_ref, gain_ref, og_ref, oraw_ref, st_ref,
             a_s, b_s, k_s, v_s, dec_s, o_s, st_s):
        @pl.when(pl.program_id(1) == 0)
        def _():
            st_s[...] = jnp.zeros_like(st_s)

        ric = lax.broadcasted_iota(jnp.int32, (TB, HEAD_DIM), 0) % A_CHUNK
        qs, _, _, kk, e_cum, e_neg, e_end, dec = _hgrn2_gates(q_ref[...], f_ref[...], lb_ref[...], ric)
        a_s[...] = (qs * e_cum).astype(BF16)
        b_s[...] = (kk * e_neg).astype(BF16)
        k_s[...] = (kk * e_end).astype(BF16)
        v_s[...] = i_ref[...].astype(BF16)
        dec_s[...] = dec
        tril = (lax.broadcasted_iota(jnp.int32, (A_CHUNK, A_CHUNK), 0)
                >= lax.broadcasted_iota(jnp.int32, (A_CHUNK, A_CHUNK), 1))

        def chunk(ci, carry):
            r = pl.multiple_of(ci * A_CHUNK, A_CHUNK)
            rows = pl.ds(r, A_CHUNK)
            a, b, k, v = a_s[rows, :], b_s[rows, :], k_s[rows, :], v_s[rows, :]
            st = st_s[...]
            st_bf = st.astype(BF16)
            st_ref[ci] = st_bf
            p = jnp.where(tril, _dot_nt(a, b), 0.0).astype(BF16)
            o_s[rows, :] = _dot_nn(p, v) + _dot_nt(a, st_bf)
            st_s[...] = dec_s[pl.ds(r, 1), :] * st + _dot_tn(v, k)
            return carry

        lax.fori_loop(0, NCB, chunk, 0)
        o = o_s[...]
        oraw_ref[...] = o
        on = o * lax.rsqrt(jnp.mean(o * o, axis=-1, keepdims=True) + NORM_EPS)
        og_ref[...] = ((on * gain_ref[...]) * _silu(g_ref[...])).astype(og_ref.dtype)

    tb_bf = pltpu.VMEM((TB, HEAD_DIM), BF16)
    tb_f = pltpu.VMEM((TB, HEAD_DIM), F32)
    return pl.pallas_call(
        kern, name=name, grid=(H, NB),
        in_specs=[col(0), col(1), col(2), col(3), vec, vec],
        out_specs=[head, head, state],
        out_shape=[jax.ShapeDtypeStruct((S, D), BF16), jax.ShapeDtypeStruct((S, D), F32),
                   jax.ShapeDtypeStruct((S // A_CHUNK, H, HEAD_DIM, HEAD_DIM), BF16)],
        scratch_shapes=[tb_bf, tb_bf, tb_bf, tb_bf, tb_f, tb_f, pltpu.VMEM((HEAD_DIM, HEAD_DIM), F32)],
        compiler_params=pltpu.CompilerParams(
            dimension_semantics=("parallel", "arbitrary"), vmem_limit_bytes=VMEM_LIMIT),
    )(proj, proj, proj, proj, lb, gain)


def _hgrn2_bwd(proj, lb, gain, oraw, states, dog, name):
    S, D4 = proj.shape
    D = D4 // 4
    TB = _pick(S, 512, A_CHUNK)
    H, NB, col, head, vec, state = _hgrn2_specs(S, D, TB, True)
    NCB = TB // A_CHUNK

    def kern(q_ref, f_ref, i_ref, g_ref, lb_ref, gain_ref, oraw_ref, st_ref, dog_ref,
             dq_ref, df_ref, di_ref, dg_ref, dlb_ref, dgain_ref,
             a_s, b_s, k_s, v_s, do_s, dec_s, da_s, db_s, dk_s, ddec_s, dst_s):
        @pl.when(pl.program_id(1) == 0)
        def _():
            dst_s[...] = jnp.zeros_like(dst_s)
            dlb_ref[...] = jnp.zeros_like(dlb_ref)
            dgain_ref[...] = jnp.zeros_like(dgain_ref)

        ric = lax.broadcasted_iota(jnp.int32, (TB, HEAD_DIM), 0) % A_CHUNK
        lbv = lb_ref[...]
        q_raw = q_ref[...]
        qs, sig, f, kk, e_cum, e_neg, e_end, dec = _hgrn2_gates(q_raw, f_ref[...], lbv, ric)
        a32, b32, k32 = qs * e_cum, kk * e_neg, kk * e_end
        a_s[...] = a32.astype(BF16)
        b_s[...] = b32.astype(BF16)
        k_s[...] = k32.astype(BF16)
        v_s[...] = i_ref[...].astype(BF16)
        dec_s[...] = dec

        o = oraw_ref[...]
        gain_v = gain_ref[...]
        g_raw = g_ref[...]
        rinv = lax.rsqrt(jnp.mean(o * o, axis=-1, keepdims=True) + NORM_EPS)
        on = o * rinv
        dog_v = dog_ref[...].astype(F32)
        dg_ref[...] = dog_v * (on * gain_v) * _dsilu(g_raw)
        dog2 = dog_v * _silu(g_raw)
        dgain_ref[...] += jnp.sum(dog2 * on, axis=0, keepdims=True)
        don = dog2 * gain_v
        do_s[...] = (rinv * (don - on * jnp.mean(don * on, axis=-1, keepdims=True))).astype(BF16)

        tril = (lax.broadcasted_iota(jnp.int32, (A_CHUNK, A_CHUNK), 0)
                >= lax.broadcasted_iota(jnp.int32, (A_CHUNK, A_CHUNK), 1))

        def chunk(it, carry):
            ci = NCB - 1 - it
            r = pl.multiple_of(ci * A_CHUNK, A_CHUNK)
            rows = pl.ds(r, A_CHUNK)
            a, b, k, v, do = a_s[rows, :], b_s[rows, :], k_s[rows, :], v_s[rows, :], do_s[rows, :]
            st_prev = st_ref[ci]
            dst = dst_s[...]
            dst_bf = dst.astype(BF16)
            p = jnp.where(tril, _dot_nt(a, b), 0.0).astype(BF16)
            dp = jnp.where(tril, _dot_nt(do, v), 0.0).astype(BF16)
            di_ref[rows, :] = _dot_tn(p, do) + _dot_nt(k, dst_bf)
            da_s[rows, :] = _dot_nn(dp, b) + _dot_nn(do, st_prev)
            db_s[rows, :] = _dot_tn(dp, a)
            dk_s[rows, :] = _dot_nn(v, dst_bf)
            ddec = jnp.sum(dst * st_prev.astype(F32), axis=0, keepdims=True)
            ddec_s[rows, :] = jnp.broadcast_to(ddec, (A_CHUNK, HEAD_DIM))
            dst_s[...] = dec_s[pl.ds(r, 1), :] * dst + _dot_tn(do, a)
            return carry

        lax.fori_loop(0, NCB, chunk, 0)

        da, db, dk = da_s[...], db_s[...], dk_s[...]
        dqs = da * e_cum
        dkk = db * e_neg + dk * e_end
        w = dk * k32
        dlog_f = (_chunk_scan(da * a32 - db * b32, ric, True) + (_chunk_scan(w, ric, False) - w)
                  + ddec_s[...] * dec)
        dfg = dlog_f / f - dkk
        df_ref[...] = dfg * (1.0 - lbv) * sig * (1.0 - sig)
        dlb_ref[...] += jnp.sum(dfg * (1.0 - sig), axis=0, keepdims=True)
        dq_ref[...] = dqs * _dsilu(q_raw)

    tb_bf = pltpu.VMEM((TB, HEAD_DIM), BF16)
    tb_f = pltpu.VMEM((TB, HEAD_DIM), F32)
    full = jax.ShapeDtypeStruct((S, D), F32)
    vec_shape = jax.ShapeDtypeStruct((1, D), F32)
    return pl.pallas_call(
        kern, name=name, grid=(H, NB),
        in_specs=[col(0), col(1), col(2), col(3), vec, vec, head, state, head],
        out_specs=[head, head, head, head, vec, vec],
        out_shape=[full, full, full, full, vec_shape, vec_shape],
        scratch_shapes=[tb_bf, tb_bf, tb_bf, tb_bf, tb_bf, tb_f, tb_f, tb_f, tb_f, tb_f,
                        pltpu.VMEM((HEAD_DIM, HEAD_DIM), F32)],
        compiler_params=pltpu.CompilerParams(
            dimension_semantics=("parallel", "arbitrary"), vmem_limit_bytes=VMEM_LIMIT),
    )(proj, proj, proj, proj, lb, gain, oraw, states, dog)


def _split_sum(v, tri):
    hi = v.astype(BF16)
    lo = (v - hi.astype(F32)).astype(BF16)
    return _dot_nn(hi, tri) + _dot_nn(lo, tri)


def _sb_logits(qs, kj, scale):
    z = _dot_nt(qs, kj) * scale
    log_beta = jnp.minimum(z, 0.0) - jnp.log1p(jnp.exp(-jnp.abs(z)))
    return z, log_beta


def _stack(ref, G):
    return jnp.concatenate([ref[:, g * HEAD_DIM:(g + 1) * HEAD_DIM] for g in range(G)], axis=0)


def _unstack(v, G):
    return jnp.concatenate([v[g * Q_BLOCK:(g + 1) * Q_BLOCK, :] for g in range(G)], axis=1)


def _sb_specs(S, D, KVH):
    G = D // HEAD_DIM // KVH
    qblk = pl.BlockSpec((Q_BLOCK, G * HEAD_DIM), lambda h, qi: (qi, h))
    kblk = pl.BlockSpec((S, HEAD_DIM), lambda h, qi: (0, h))
    vblk = pl.BlockSpec((S, HEAD_DIM), lambda h, qi: (0, KVH + h))
    return G, qblk, kblk, vblk


def _sb_fwd(q, kv, name):
    S, D = q.shape
    KVH = kv.shape[1] // (2 * HEAD_DIM)
    G, qblk, kblk, vblk = _sb_specs(S, D, KVH)
    R = G * Q_BLOCK
    scale = HEAD_DIM ** -0.5

    def kern(q_ref, k_ref, v_ref, o_ref, tot_ref, acc_s, run_s):
        qi = pl.program_id(1)
        qs = _stack(q_ref, G)
        row = lax.broadcasted_iota(jnp.int32, (R, Q_BLOCK), 0) % Q_BLOCK
        colm = lax.broadcasted_iota(jnp.int32, (R, Q_BLOCK), 1)
        mask = colm < row
        ti = lax.broadcasted_iota(jnp.int32, (Q_BLOCK, Q_BLOCK), 0)
        tj = lax.broadcasted_iota(jnp.int32, (Q_BLOCK, Q_BLOCK), 1)
        after = (ti > tj).astype(BF16)
        acc_s[...] = jnp.zeros_like(acc_s)
        run_s[...] = jnp.zeros_like(run_s)

        def tile(j, masked):
            ks = pl.ds(pl.multiple_of(j * Q_BLOCK, Q_BLOCK), Q_BLOCK)
            kj, vj = k_ref[ks, :], v_ref[ks, :]
            z, log_beta = _sb_logits(qs, kj, scale)
            log_rest = log_beta - z
            if masked:
                log_rest = jnp.where(mask, log_rest, 0.0)
            between = _split_sum(log_rest, after) + run_s[...]
            w = jnp.exp(log_beta + between)
            if masked:
                w = jnp.where(mask, w, 0.0)
            acc_s[...] += _dot_nn(w.astype(BF16), vj)
            run_s[...] += jnp.sum(log_rest, axis=1, keepdims=True)

        tile(qi, True)

        def body(it, carry):
            tile(qi - 1 - it, False)
            return carry

        lax.fori_loop(0, qi, body, 0)
        o_ref[...] = _unstack(acc_s[...], G).astype(o_ref.dtype)
        tot_ref[...] = _unstack(run_s[...], G)

    return pl.pallas_call(
        kern, name=name, grid=(KVH, S // Q_BLOCK),
        in_specs=[qblk, kblk, vblk], out_specs=[qblk, qblk],
        out_shape=[jax.ShapeDtypeStruct((S, D), BF16), jax.ShapeDtypeStruct((S, D), F32)],
        scratch_shapes=[pltpu.VMEM((R, HEAD_DIM), F32), pltpu.VMEM((R, Q_BLOCK), F32)],
        compiler_params=pltpu.CompilerParams(
            dimension_semantics=("parallel", "arbitrary"), vmem_limit_bytes=VMEM_LIMIT),
    )(q, kv, kv)


def _sb_bwd(q, kv, tot, do, name):
    S, D = q.shape
    KVH = kv.shape[1] // (2 * HEAD_DIM)
    G, qblk, kblk, vblk = _sb_specs(S, D, KVH)
    R = G * Q_BLOCK
    scale = HEAD_DIM ** -0.5

    def kern(q_ref, k_ref, v_ref, tot_ref, do_ref, dq_ref, dk_ref, dv_ref, dq_s, pre_s, esum_s):
        qi = pl.program_id(1)

        @pl.when(qi == 0)
        def _():
            dk_ref[...] = jnp.zeros_like(dk_ref)
            dv_ref[...] = jnp.zeros_like(dv_ref)

        qs = _stack(q_ref, G)
        dos = _stack(do_ref, G)
        total = _stack(tot_ref, G)
        row = lax.broadcasted_iota(jnp.int32, (R, Q_BLOCK), 0) % Q_BLOCK
        colm = lax.broadcasted_iota(jnp.int32, (R, Q_BLOCK), 1)
        mask = colm < row
        ti = lax.broadcasted_iota(jnp.int32, (Q_BLOCK, Q_BLOCK), 0)
        tj = lax.broadcasted_iota(jnp.int32, (Q_BLOCK, Q_BLOCK), 1)
        upto = (ti <= tj).astype(BF16)
        before = (ti < tj).astype(BF16)
        dq_s[...] = jnp.zeros_like(dq_s)
        pre_s[...] = jnp.zeros_like(pre_s)
        esum_s[...] = jnp.zeros_like(esum_s)

        def tile(j, masked):
            ks = pl.ds(pl.multiple_of(j * Q_BLOCK, Q_BLOCK), Q_BLOCK)
            kj, vj = k_ref[ks, :], v_ref[ks, :]
            z, log_beta = _sb_logits(qs, kj, scale)
            log_rest = log_beta - z
            if masked:
                log_rest = jnp.where(mask, log_rest, 0.0)
            between = total - (pre_s[...] + _split_sum(log_rest, upto))
            w = jnp.exp(log_beta + between)
            if masked:
                w = jnp.where(mask, w, 0.0)
            e = _dot_nt(dos, vj) * w
            e_before = esum_s[...] + _split_sum(e, before)
            beta = jnp.exp(log_beta)
            dz = e * (1.0 - beta) - e_before * beta
            if masked:
                dz = jnp.where(mask, dz, 0.0)
            dzs = (dz * scale).astype(BF16)
            dq_s[...] += _dot_nn(dzs, kj)
            dk_ref[ks, :] += _dot_tn(dzs, qs)
            dv_ref[ks, :] += _dot_tn(w.astype(BF16), dos)
            pre_s[...] += jnp.sum(log_rest, axis=1, keepdims=True)
            esum_s[...] += jnp.sum(e, axis=1, keepdims=True)

        def body(j, carry):
            tile(j, False)
            return carry

        lax.fori_loop(0, qi, body, 0)
        tile(qi, True)
        dq_ref[...] = _unstack(dq_s[...], G).astype(dq_ref.dtype)

    kvout = pl.BlockSpec((S, HEAD_DIM), lambda h, qi: (0, h))
    wide = pltpu.VMEM((R, Q_BLOCK), F32)
    return pl.pallas_call(
        kern, name=name, grid=(KVH, S // Q_BLOCK),
        in_specs=[qblk, kblk, vblk, qblk, qblk], out_specs=[qblk, kvout, kvout],
        out_shape=[jax.ShapeDtypeStruct((S, D), BF16), jax.ShapeDtypeStruct((S, KVH * HEAD_DIM), F32),
                   jax.ShapeDtypeStruct((S, KVH * HEAD_DIM), F32)],
        scratch_shapes=[pltpu.VMEM((R, HEAD_DIM), F32), wide, wide],
        compiler_params=pltpu.CompilerParams(
            dimension_semantics=("parallel", "arbitrary"), vmem_limit_bytes=VMEM_LIMIT),
    )(q, kv, kv, tot, do)


def _adamw(parts, w, m, v, name):
    shape = w.shape
    C = shape[-1]
    R = w.size // C
    P = parts.shape[0]
    parts2, w2, m2, v2 = parts.reshape(P, R, C), w.reshape(R, C), m.reshape(R, C), v.reshape(R, C)
    tr = _pick(R, max(8, (1 << 18) // C), 16)

    def kern(p_ref, w_ref, m_ref, v_ref, g_ref, d_ref, nm_ref, nv_ref):
        g = p_ref[0].astype(F32)
        for i in range(1, P):
            g = g + p_ref[i].astype(F32)
        nm = ADAM_B1 * m_ref[...] + (1.0 - ADAM_B1) * g
        nv = ADAM_B2 * v_ref[...] + (1.0 - ADAM_B2) * (g * g)
        m_hat = nm / (1.0 - ADAM_B1 ** ADAM_STEP)
        v_hat = nv / (1.0 - ADAM_B2 ** ADAM_STEP)
        g_ref[...] = g
        d_ref[...] = -ADAM_LR * (m_hat / (jnp.sqrt(v_hat) + ADAM_EPS) + ADAM_WD * w_ref[...])
        nm_ref[...] = nm
        nv_ref[...] = nv

    blk = pl.BlockSpec((tr, C), lambda i: (i, 0))
    out = jax.ShapeDtypeStruct((R, C), F32)
    res = pl.pallas_call(
        kern, name=name, grid=(R // tr,),
        in_specs=[pl.BlockSpec((P, tr, C), lambda i: (0, i, 0)), blk, blk, blk],
        out_specs=[blk, blk, blk, blk], out_shape=[out, out, out, out],
        compiler_params=pltpu.CompilerParams(dimension_semantics=("parallel",), vmem_limit_bytes=VMEM_LIMIT),
    )(parts2, w2, m2, v2)
    return tuple(r.reshape(shape) for r in res)


def _relu2_epilogue(acc):
    a = jnp.maximum(acc, 0.0)
    return a * a, a


def _make_norm_linear(name, relu2, out_dtype):
    def fwd(x, gain, sc, sh, w):
        h = _norm_mod_fwd(x, gain, sc, sh, name + "_norm")
        if relu2:
            y, a = _matmul(h, w, "nn", (BF16, BF16), name + "_mm", epilogue=_relu2_epilogue)
        else:
            (y,), a = _matmul(h, w, "nn", (out_dtype,), name + "_mm"), None
        return y, (x, gain, sc, h, w, a)

    def bwd(res, dy):
        x, gain, sc, h, w, a = res
        dz = _relu2_bwd(dy, a, name + "_dact") if relu2 else dy
        (dh,) = _matmul(dz, w, "nt", (F32,), name + "_dh")
        (dw,) = _matmul(h, dz, "tn", (BF16,), name + "_dw")
        dx, dgain, dsc, dsh = _norm_mod_bwd(x, gain, sc, dh, name + "_dnorm")
        return dx, dgain, dsc, dsh, dw

    f = jax.custom_vjp(lambda x, gain, sc, sh, w: fwd(x, gain, sc, sh, w)[0])
    f.defvjp(fwd, bwd)
    return f


def _make_linear_residual(name):
    def fwd(u, w, x, g):
        x_new, y = _matmul(u, w, "nn", (F32, BF16), name + "_mm",
                           epilogue=lambda acc, xt, gr: (xt + gr * acc, acc), tiles=(x,), rows=(g,))
        return x_new, (u, w, g, y)

    def bwd(res, d):
        u, w, g, y = res
        dy, dg = _gate_bwd(d, y, g, name + "_dgate")
        (du,) = _matmul(dy, w, "nt", (BF16,), name + "_du")
        (dw,) = _matmul(u, dy, "tn", (BF16,), name + "_dw")
        return du, dw, d, dg

    f = jax.custom_vjp(lambda u, w, x, g: fwd(u, w, x, g)[0])
    f.defvjp(fwd, bwd)
    return f


def _make_hgrn2(name):
    def fwd(proj, lb, gain):
        og, oraw, states = _hgrn2_fwd(proj, lb, gain, name + "_fwd")
        return og, (proj, lb, gain, oraw, states)

    def bwd(res, dog):
        proj, lb, gain, oraw, states = res
        dq, df, di, dg, dlb, dgain = _hgrn2_bwd(proj, lb, gain, oraw, states, dog, name + "_bwd")
        return jnp.concatenate([dq, df, di, dg], axis=1), dlb, dgain

    f = jax.custom_vjp(lambda proj, lb, gain: fwd(proj, lb, gain)[0])
    f.defvjp(fwd, bwd)
    return f


def _make_attention(name):
    def fwd(q, kv):
        o, tot = _sb_fwd(q, kv, name + "_fwd")
        return o, (q, kv, tot)

    def bwd(res, do):
        q, kv, tot = res
        dq, dk, dv = _sb_bwd(q, kv, tot, do, name + "_bwd")
        return dq, jnp.concatenate([dk, dv], axis=1).astype(BF16)

    f = jax.custom_vjp(lambda q, kv: fwd(q, kv)[0])
    f.defvjp(fwd, bwd)
    return f


def _trunk(x, vecs, lb, out_gain, norms, weights):
    (sh1a, sc1a, g1a, sh2a, sc2a, g2a, sh1b, sc1b, g1b, sh2b, sc2b, g2b, kv_sh, kv_sc) = vecs
    nmix0, nmix1, nmlp0, nmlp1, kvn = norms
    w_in, w_aout, w_kv, w_q, w_bout, w1a, w1b, w2a, w2b = weights

    proj = _make_norm_linear("a_in", False, F32)(x, nmix0, sc1a, sh1a, w_in)
    og = _make_hgrn2("hgrn2")(proj, lb, out_gain)
    x = _make_linear_residual("a_out")(og, w_aout, x, g1a)
    u = _make_norm_linear("mlp0_up", True, BF16)(x, nmlp0, sc2a, sh2a, w1a)
    x = _make_linear_residual("mlp0_down")(u, w2a, x, g2a)

    kv = _make_norm_linear("kv", False, BF16)(x, kvn, kv_sc, kv_sh, w_kv)
    q = _make_norm_linear("b_q", False, BF16)(x, nmix1, sc1b, sh1b, w_q)
    o = _make_attention("attn")(q, kv)
    x = _make_linear_residual("b_out")(o, w_bout, x, g1b)
    u = _make_norm_linear("mlp1_up", True, BF16)(x, nmlp1, sc2b, sh2b, w1b)
    x = _make_linear_residual("mlp1_down")(u, w2b, x, g2b)
    return x


def _lower_bound(logits):
    return jnp.cumsum(jax.nn.softmax(logits.astype(F32), axis=0), axis=0)[0:1]


def _pad_rows(a, rows):
    return jnp.zeros((rows, a.shape[1]), a.dtype).at[:a.shape[0]].set(a)


def kernel(x, c, ada_w, ada_b, norm_mix, norm_mlp, a_w_in, a_lb_logits, a_out_gain, a_w_out, kv_ada_w, kv_ada_b, kv_norm, w_kv, b_w_q, b_w_out, mlp_w1, mlp_w2, final_norm, loss_target, m_ada_w, m_ada_b, m_norm_mix, m_norm_mlp, m_a_w_in, m_a_lb_logits, m_a_out_gain, m_a_w_out, m_kv_ada_w, m_kv_ada_b, m_kv_norm, m_w_kv, m_b_w_q, m_b_w_out, m_mlp_w1, m_mlp_w2, m_final_norm, v_ada_w, v_ada_b, v_norm_mix, v_norm_mlp, v_a_w_in, v_a_lb_logits, v_a_out_gain, v_a_w_out, v_kv_ada_w, v_kv_ada_b, v_kv_norm, v_w_kv, v_b_w_q, v_b_w_out, v_mlp_w1, v_mlp_w2, v_final_norm):
    S, D = x.shape[1], x.shape[2]
    assert x.shape[0] == 1 and ada_w.shape[0] == 2 and a_w_in.shape[0] == 1 and b_w_q.shape[0] == 1
    me = 4 * lax.axis_index("x") + 2 * lax.axis_index("y") + lax.axis_index("c")
    dl = D // N_DEV
    na = ada_w.shape[2]
    nk = kv_ada_w.shape[1]

    small = jnp.concatenate([c.reshape(1, D), a_lb_logits.reshape(1, 2 * dl), a_out_gain.reshape(1, dl)], axis=1)
    small_all = _gather_major(small, "gather_small")[:, 0, :]
    c_all = small_all[:, :D]
    lb_logits = small_all[:, D:D + 2 * dl].reshape(N_DEV, 2, dl).transpose(1, 0, 2).reshape(2, D)
    out_gain = small_all[:, D + 2 * dl:].reshape(1, D)

    c_act = jax.nn.silu(c_all)
    c_act_rows = _pad_rows(c_act.astype(BF16), 128)
    mod_cols = jnp.concatenate([
        _matmul(c_act_rows, ada_w[0].astype(BF16), "nn", (F32,), "ada0")[0][:N_DEV],
        _matmul(c_act_rows, ada_w[1].astype(BF16), "nn", (F32,), "ada1")[0][:N_DEV],
        _matmul(c_act_rows, kv_ada_w.astype(BF16), "nn", (F32,), "ada_kv")[0][:N_DEV]], axis=1)
    mod_all = _gather_major(mod_cols, "gather_mod")
    mod_mine = lax.dynamic_index_in_dim(mod_all, me, axis=1, keepdims=False)
    mod0 = mod_mine[:, :na].reshape(1, 6 * D) + ada_b[0:1]
    mod1 = mod_mine[:, na:2 * na].reshape(1, 6 * D) + ada_b[1:2]
    modk = mod_mine[:, 2 * na:].reshape(1, 2 * D) + kv_ada_b.reshape(1, 2 * D)
    vecs = tuple(jnp.split(mod0, 6, axis=1)) + tuple(jnp.split(mod1, 6, axis=1)) + tuple(jnp.split(modk, 2, axis=1))

    w_in = _gather_cols(a_w_in[0].astype(BF16), "gather_a_w_in")
    w_aout = _gather_major(a_w_out[0].astype(BF16), "gather_a_w_out").reshape(D, D)
    w_kvf = _gather_major(w_kv.astype(BF16), "gather_w_kv").reshape(D, w_kv.shape[1])
    w_q = _gather_major(b_w_q[0].astype(BF16), "gather_b_w_q").reshape(D, D)
    w_bout = _gather_major(b_w_out[0].astype(BF16), "gather_b_w_out").reshape(D, D)
    w1 = _gather_cols(mlp_w1.astype(BF16), "gather_mlp_w1")
    w2 = _gather_major(mlp_w2.astype(BF16).transpose(1, 0, 2), "gather_mlp_w2")
    F = w2.shape[0] * w2.shape[1]
    w2 = w2.reshape(F, 2, D)
    weights = (w_in, w_aout, w_kvf, w_q, w_bout, w1[0], w1[1], w2[:, 0, :], w2[:, 1, :])

    norms = (norm_mix[0:1], norm_mix[1:2], norm_mlp[0:1], norm_mlp[1:2], kv_norm.reshape(1, D))
    lb, lb_vjp = jax.vjp(_lower_bound, lb_logits)
    x_out, trunk_vjp = jax.vjp(_trunk, x[0], vecs, lb, out_gain, norms, weights)
    loss_vec, dx_out, d_final = _final_loss(x_out, final_norm.reshape(1, D), loss_target[0], "final_loss")
    dx, dvecs, dlb, d_out_gain, dnorms, dweights = trunk_vjp(dx_out)
    (d_lb_logits,) = lb_vjp(dlb)
    loss = lax.psum(loss_vec[0, 0], MESH_AXES)

    dmod0 = jnp.concatenate(dvecs[0:6], axis=1)
    dmod1 = jnp.concatenate(dvecs[6:12], axis=1)
    dmodk = jnp.concatenate(dvecs[12:14], axis=1)
    pieces = [dmod0, dmod1, dmodk, dnorms[0], dnorms[1], dnorms[2], dnorms[3], dnorms[4], d_final,
              d_lb_logits.reshape(1, 2 * D), d_out_gain]
    widths = [p.shape[1] for p in pieces]
    offs = [sum(widths[:i]) for i in range(len(widths))]
    part_all = _gather_major(jnp.concatenate(pieces, axis=1), "gather_dsmall")[:, 0, :]
    take = lambda i: part_all[:, offs[i]:offs[i] + widths[i]]
    dmod0_all, dmod1_all, dmodk_all = take(0), take(1), take(2)

    outs = {}

    def update(key, parts, w, m, v):
        outs[key] = _adamw(parts, w, m, v, "adamw_" + key)

    c_act_cols = _pad_rows(c_act.astype(BF16), 128).T
    my_cols = lambda a, n: _pad_rows(lax.dynamic_slice_in_dim(a, me * n, n, axis=1).astype(BF16), 128)
    g_ada0 = _matmul(c_act_cols, my_cols(dmod0_all, na), "nn", (F32,), "dada0")[0]
    g_ada1 = _matmul(c_act_cols, my_cols(dmod1_all, na), "nn", (F32,), "dada1")[0]
    g_adak = _matmul(c_act_cols, my_cols(dmodk_all, nk), "nn", (F32,), "dada_kv")[0]
    update("ada_w", jnp.stack([g_ada0, g_ada1])[None], ada_w, m_ada_w, v_ada_w)
    update("kv_ada_w", g_adak[None], kv_ada_w, m_kv_ada_w, v_kv_ada_w)
    update("ada_b", jnp.stack([dmod0_all, dmod1_all], axis=1), ada_b, m_ada_b, v_ada_b)
    update("kv_ada_b", dmodk_all, kv_ada_b, m_kv_ada_b, v_kv_ada_b)
    update("norm_mix", jnp.stack([take(3), take(4)], axis=1), norm_mix, m_norm_mix, v_norm_mix)
    update("norm_mlp", jnp.stack([take(5), take(6)], axis=1), norm_mlp, m_norm_mlp, v_norm_mlp)
    update("kv_norm", take(7), kv_norm, m_kv_norm, v_kv_norm)
    update("final_norm", take(8), final_norm, m_final_norm, v_final_norm)
    d_lb_all = take(9).reshape(N_DEV, 2, D)
    update("a_lb_logits", lax.dynamic_slice_in_dim(d_lb_all, me * dl, dl, axis=2), a_lb_logits, m_a_lb_logits,
           v_a_lb_logits)
    update("a_out_gain", lax.dynamic_slice_in_dim(take(10), me * dl, dl, axis=1)[:, None, :], a_out_gain,
           m_a_out_gain, v_a_out_gain)

    d_in, d_aout, d_kv, d_q, d_bout, d_w1a, d_w1b, d_w2a, d_w2b = dweights
    update("a_w_in", _exchange_cols(d_in, "scatter_a_w_in")[:, None], a_w_in, m_a_w_in, v_a_w_in)
    update("a_w_out", _exchange_rows(d_aout, "scatter_a_w_out")[:, None], a_w_out, m_a_w_out, v_a_w_out)
    update("w_kv", _exchange_rows(d_kv, "scatter_w_kv"), w_kv, m_w_kv, v_w_kv)
    update("b_w_q", _exchange_rows(d_q, "scatter_b_w_q")[:, None], b_w_q, m_b_w_q, v_b_w_q)
    update("b_w_out", _exchange_rows(d_bout, "scatter_b_w_out")[:, None], b_w_out, m_b_w_out, v_b_w_out)
    update("mlp_w1", _exchange_cols(jnp.stack([d_w1a, d_w1b]), "scatter_mlp_w1"), mlp_w1, m_mlp_w1, v_mlp_w1)
    d_w2 = jnp.stack([d_w2a, d_w2b], axis=1)
    d_w2 = _exchange_rows(d_w2.reshape(F, 2 * D), "scatter_mlp_w2")
    d_w2 = d_w2.reshape(N_DEV, F // N_DEV, 2, D).transpose(0, 2, 1, 3)
    update("mlp_w2", d_w2, mlp_w2, m_mlp_w2, v_mlp_w2)

    names = ["ada_w", "ada_b", "norm_mix", "norm_mlp", "a_w_in", "a_lb_logits", "a_out_gain", "a_w_out", "kv_ada_w",
             "kv_ada_b", "kv_norm", "w_kv", "b_w_q", "b_w_out", "mlp_w1", "mlp_w2", "final_norm"]
    result = [loss, dx[None]]
    for field in range(4):
        result += [outs[n][field] for n in names]
    return tuple(result)
```

```python
import jax
import jax.numpy as jnp
from jax import lax
from jax.experimental import pallas as pl
from jax.experimental.pallas import tpu as pltpu

F32 = jnp.float32
BF16 = jnp.bfloat16

N_DEV = 8
MESH_AXES = ("x", "y", "c")
HEAD_DIM = 128
A_CHUNK = 16
Q_BLOCK = 128
NORM_EPS = 1e-6
ADAM_LR = 0.001
ADAM_B1 = 0.9
ADAM_B2 = 0.999
ADAM_EPS = 1e-08
ADAM_WD = 0.01
ADAM_STEP = 10
VMEM_LIMIT = 56 * 1024 * 1024
MATMUL_VMEM = 40 * 1024 * 1024


def _pick(dim, target, align):
    t = (min(dim, target) // align) * align
    while t >= align:
        if dim % t == 0:
            return t
        t -= align
    return dim


def _sigmoid(x):
    return 1.0 / (1.0 + jnp.exp(-x))


def _silu(x):
    return x * _sigmoid(x)


def _dsilu(x):
    s = _sigmoid(x)
    return s * (1.0 + x * (1.0 - s))


def _dot(a, b, dims):
    return lax.dot_general(a, b, (dims, ((), ())), preferred_element_type=F32)


def _dot_nn(a, b):
    return _dot(a, b, ((1,), (0,)))


def _dot_nt(a, b):
    return _dot(a, b, ((1,), (1,)))


def _dot_tn(a, b):
    return _dot(a, b, ((0,), (0,)))


def _mesh_pos():
    return lax.axis_index("x"), lax.axis_index("y"), lax.axis_index("c")


def _flip(v, d):
    return 1 - v if d else v


def _all_gather(x, name):
    def body(x_ref, out_ref, send_sems, recv_sems, local_sem):
        x, y, c = _mesh_pos()
        me, sibling = (x, y, c), (x, y, 1 - c)
        chips = [(1 - x, y), (x, 1 - y), (1 - x, 1 - y)]

        def rows(px, py, pc):
            return out_ref.at[4 * px + 2 * py + pc]

        def copy(k, block, to, src=None):
            return pltpu.make_async_remote_copy(
                src_ref=rows(*block) if src is None else src,
                dst_ref=rows(*block),
                send_sem=send_sems.at[k],
                recv_sem=recv_sems.at[k],
                device_id=to,
                device_id_type=pl.DeviceIdType.MESH,
            )

        mine = pltpu.make_async_copy(x_ref, rows(*me), local_sem)
        mine.start()
        first = [copy(0, me, sibling, src=x_ref)]
        first += [copy(1 + j, me, (*chip, c), src=x_ref) for j, chip in enumerate(chips)]
        for cp in first:
            cp.start()
        passed = [copy(4 + j, (*chip, c), sibling) for j, chip in enumerate(chips)]
        for j, chip in enumerate(chips):
            copy(1 + j, (*chip, c), me).wait_recv()
            passed[j].start()
        copy(0, sibling, me).wait_recv()
        for j, chip in enumerate(chips):
            copy(4 + j, (*chip, 1 - c), me).wait_recv()
        for cp in first + passed:
            cp.wait_send()
        mine.wait()

    return pl.pallas_call(
        body,
        name=name,
        out_shape=jax.ShapeDtypeStruct((N_DEV,) + x.shape, x.dtype),
        in_specs=[pl.BlockSpec(memory_space=pl.ANY)],
        out_specs=pl.BlockSpec(memory_space=pl.ANY),
        scratch_shapes=[
            pltpu.SemaphoreType.DMA((7,)),
            pltpu.SemaphoreType.DMA((7,)),
            pltpu.SemaphoreType.DMA(()),
        ],
    )(x)


_RELATIONS = [(dx, dy, dc) for dx in (0, 1) for dy in (0, 1) for dc in (0, 1) if (dx, dy, dc) != (0, 0, 0)]
_HBM = pl.BlockSpec(memory_space=pltpu.HBM)
_SEM = pl.BlockSpec(memory_space=pltpu.SEMAPHORE)
_EFFECT = pltpu.SideEffectType.DATAFLOW_SIDE_EFFECTING


def _at(ref, idx):
    return ref.at[idx] if idx else ref


def _block(axis, ndim, n):
    return lambda d: (slice(None),) * axis + (pl.ds(d * n, n),) + (slice(None),) * (ndim - axis - 1)


class _Route:
    def __init__(self, land_shape, src_slice, dst_slice):
        self.land_shape, self.src_slice, self.dst_slice = tuple(land_shape), src_slice, dst_slice


def _gather_route(shard, axis):
    n = shard.shape[axis]
    shape = shard.shape[:axis] + (N_DEV * n,) + shard.shape[axis + 1:]
    return _Route(shape, lambda p: (), _block(axis, shard.ndim, n))


def _scatter_route(g, axis):
    n = g.shape[axis] // N_DEV
    shape = (N_DEV,) + g.shape[:axis] + (n,) + g.shape[axis + 1:]
    return _Route(shape, _block(axis, g.ndim, n), lambda p: (p,))


def _peers():
    x, y, c = _mesh_pos()
    out = []
    for k, (dx, dy, dc) in enumerate(_RELATIONS):
        px, py, pc = _flip(x, dx), _flip(y, dy), _flip(c, dc)
        out.append((k, (px, py, pc), 4 * px + 2 * py + pc))
    return 4 * x + 2 * y + c, out


def _copies_start(srcs, routes, name):
    n = len(srcs)

    def body(*refs):
        src_refs, land_refs = refs[:n], refs[n:2 * n]
        send, recv, token = refs[2 * n:3 * n], refs[3 * n:4 * n], refs[-1]
        me, peers = _peers()
        for i, route in enumerate(routes):
            for k, peer, pid in peers:
                pltpu.make_async_remote_copy(
                    src_ref=_at(src_refs[i], route.src_slice(pid)), dst_ref=_at(land_refs[i], route.dst_slice(me)),
                    send_sem=send[i].at[k], recv_sem=recv[i].at[k],
                    device_id=peer, device_id_type=pl.DeviceIdType.MESH).start()
        token[...] = jnp.zeros_like(token)

    lands = [lax.empty(r.land_shape, s.dtype) for s, r in zip(srcs, routes)]
    hbm = lambda a: pltpu.with_memory_space_constraint(a, pltpu.HBM)
    res = pl.pallas_call(
        body, name=name,
        out_shape=([pltpu.SemaphoreType.DMA((7,))] * (2 * n)
                   + [pltpu.HBM(s.shape, s.dtype) for s in srcs]
                   + [pltpu.HBM(l.shape, l.dtype) for l in lands]
                   + [jax.ShapeDtypeStruct((8, 128), F32)]),
        in_specs=[_HBM] * (2 * n),
        out_specs=[_SEM] * (2 * n) + [_HBM] * (2 * n) + [pl.BlockSpec(memory_space=pltpu.VMEM)],
        input_output_aliases={i: 2 * n + i for i in range(2 * n)},
        compiler_params=pltpu.CompilerParams(has_side_effects=_EFFECT),
    )(*[hbm(s) for s in srcs], *[hbm(l) for l in lands])
    handles = [(res[i], res[n + i], res[2 * n + i], res[3 * n + i]) for i in range(n)]
    return handles, res[-1]


def _copies_wait(handle, route, after, name):
    send_sems, recv_sems, src, land = handle

    def body(src_ref, land_ref, send_ref, recv_ref, after_ref, src_out, got_ref, local_sem):
        me, peers = _peers()
        mine = pltpu.make_async_copy(_at(src_ref, route.src_slice(me)), _at(got_ref, route.dst_slice(me)), local_sem)
        mine.start()
        for k, peer, pid in peers:
            cp = pltpu.make_async_remote_copy(
                src_ref=_at(src_ref, route.src_slice(pid)), dst_ref=_at(land_ref, route.dst_slice(pid)),
                send_sem=send_ref.at[k], recv_sem=recv_ref.at[k],
                device_id=peer, device_id_type=pl.DeviceIdType.MESH)
            cp.wait_send()
            cp.wait_recv()
        mine.wait()

    return pl.pallas_call(
        body, name=name,
        out_shape=(pltpu.HBM(src.shape, src.dtype), pltpu.HBM(land.shape, land.dtype)),
        in_specs=[_HBM, _HBM, _SEM, _SEM, pl.BlockSpec(memory_space=pl.ANY)],
        out_specs=(_HBM, _HBM),
        input_output_aliases={0: 0, 1: 1},
        scratch_shapes=[pltpu.SemaphoreType.DMA(())],
        compiler_params=pltpu.CompilerParams(has_side_effects=_EFFECT),
    )(src, land, send_sems, recv_sems, after)[1]


def _matmul_tiles(M, N, K, a_item, b_item, mn_bytes):
    tm, tk = _pick(M, 512, 128), _pick(K, 2048, 128)
    tn = _pick(N, 2048, 128)
    while True:
        cast = (tm * tk * 2 if a_item != 2 else 0) + (tk * tn * 2 if b_item != 2 else 0)
        need = 2 * (tm * tk * a_item + tk * tn * b_item + tm * tn * mn_bytes) + 2 * tm * tn * 4 + cast
        smaller = _pick(N, tn - 128, 128) if tn > 128 else tn
        if need <= MATMUL_VMEM or smaller >= tn:
            return tm, tn, tk
        tn = smaller


def _matmul(a, b, mode, out_dtypes, name, epilogue=None, tiles=(), rows=()):
    if mode == "nn":
        (M, K), (K2, N) = a.shape, b.shape
    elif mode == "nt":
        (M, K), (N, K2) = a.shape, b.shape
    else:
        (K, M), (K2, N) = a.shape, b.shape
    assert K == K2, (a.shape, b.shape, mode)
    mn_bytes = sum(t.dtype.itemsize for t in tiles) + sum(jnp.dtype(d).itemsize for d in out_dtypes)
    tm, tn, tk = _matmul_tiles(M, N, K, a.dtype.itemsize, b.dtype.itemsize, mn_bytes)
    nm, nn, nk = M // tm, N // tn, K // tk
    n_extra = len(tiles) + len(rows)
    n_out = len(out_dtypes)
    if epilogue is None:
        epilogue = lambda acc: (acc,)
    dims = {"nn": ((1,), (0,)), "nt": ((1,), (1,)), "tn": ((0,), (0,))}[mode]

    def finish(acc, extra, outs):
        vals = epilogue(acc, *[r[...] for r in extra])
        for o_ref, v in zip(outs, vals):
            o_ref[...] = v.astype(o_ref.dtype)

    def kern(*refs):
        a_ref, b_ref = refs[0], refs[1]
        extra = refs[2:2 + n_extra]
        outs = refs[2 + n_extra:2 + n_extra + n_out]
        part = _dot(a_ref[...].astype(BF16), b_ref[...].astype(BF16), dims)
        if nk == 1:
            finish(part, extra, outs)
            return
        acc_ref = refs[-1]
        k = pl.program_id(2)

        @pl.when(k == 0)
        def _():
            acc_ref[...] = part

        @pl.when(k > 0)
        def _():
            acc_ref[...] += part

        @pl.when(k == nk - 1)
        def _():
            finish(acc_ref[...], extra, outs)

    a_bytes, b_bytes = a.size * a.dtype.itemsize, b.size * b.dtype.itemsize
    m_outer = a_bytes + nm * b_bytes <= nn * a_bytes + b_bytes
    ij = (lambda g0, g1: (g0, g1)) if m_outer else (lambda g0, g1: (g1, g0))

    def spec(shape, fn):
        return pl.BlockSpec(shape, lambda g0, g1, k: fn(*ij(g0, g1), k))

    a_spec = {"nn": spec((tm, tk), lambda i, j, k: (i, k)),
              "nt": spec((tm, tk), lambda i, j, k: (i, k)),
              "tn": spec((tk, tm), lambda i, j, k: (k, i))}[mode]
    b_spec = {"nn": spec((tk, tn), lambda i, j, k: (k, j)),
              "nt": spec((tn, tk), lambda i, j, k: (j, k)),
              "tn": spec((tk, tn), lambda i, j, k: (k, j))}[mode]
    tile_spec = spec((tm, tn), lambda i, j, k: (i, j))
    row_spec = spec((1, tn), lambda i, j, k: (0, j))
    return pl.pallas_call(
        kern,
        name=name,
        grid=(nm, nn, nk) if m_outer else (nn, nm, nk),
        in_specs=[a_spec, b_spec] + [tile_spec] * len(tiles) + [row_spec] * len(rows),
        out_specs=[tile_spec] * n_out,
        out_shape=[jax.ShapeDtypeStruct((M, N), dt) for dt in out_dtypes],
        scratch_shapes=[pltpu.VMEM((tm, tn), F32)] if nk > 1 else [],
        compiler_params=pltpu.CompilerParams(
            dimension_semantics=("parallel", "parallel", "arbitrary"), vmem_limit_bytes=VMEM_LIMIT),
    )(a, b, *tiles, *rows)


def _relu2_epilogue(acc):
    a = jnp.maximum(acc, 0.0)
    return a * a, a


def _residual_epilogue(acc, x_tile, gate_row):
    return x_tile + gate_row * acc, acc


def _norm_mod_fwd(x, gain, sc, sh, name):
    S, D = x.shape
    ts = _pick(S, 256, 16)

    def kern(x_ref, g_ref, sc_ref, sh_ref, h_ref):
        xv = x_ref[...]
        inv = lax.rsqrt(jnp.mean(xv * xv, axis=-1, keepdims=True) + NORM_EPS)
        h = (xv * inv) * g_ref[...] * (1.0 + sc_ref[...]) + sh_ref[...]
        h_ref[...] = h.astype(h_ref.dtype)

    row = pl.BlockSpec((1, D), lambda i: (0, 0))
    blk = pl.BlockSpec((ts, D), lambda i: (i, 0))
    return pl.pallas_call(
        kern, name=name, grid=(S // ts,),
        in_specs=[blk, row, row, row], out_specs=blk,
        out_shape=jax.ShapeDtypeStruct((S, D), BF16),
        compiler_params=pltpu.CompilerParams(dimension_semantics=("parallel",), vmem_limit_bytes=VMEM_LIMIT),
    )(x, gain, sc, sh)


def _norm_mod_bwd(x, gain, sc, dh, d_in, name):
    S, D = x.shape
    ts = _pick(S, 256, 8)

    def kern(x_ref, g_ref, sc_ref, dh_ref, din_ref, dx_ref, dg_ref, dsc_ref, dsh_ref):
        @pl.when(pl.program_id(0) == 0)
        def _():
            dg_ref[...] = jnp.zeros_like(dg_ref)
            dsc_ref[...] = jnp.zeros_like(dsc_ref)
            dsh_ref[...] = jnp.zeros_like(dsh_ref)

        xv = x_ref[...]
        dh = dh_ref[...].astype(F32)
        inv = lax.rsqrt(jnp.mean(xv * xv, axis=-1, keepdims=True) + NORM_EPS)
        xn = xv * inv
        g = g_ref[...]
        dsh_ref[...] += jnp.sum(dh, axis=0, keepdims=True)
        dsc_ref[...] += jnp.sum(dh * (xn * g), axis=0, keepdims=True)
        dn = dh * (1.0 + sc_ref[...])
        dg_ref[...] += jnp.sum(dn * xn, axis=0, keepdims=True)
        dxn = dn * g
        dx_ref[...] = din_ref[...] + inv * (dxn - xn * jnp.mean(dxn * xn, axis=-1, keepdims=True))

    row = pl.BlockSpec((1, D), lambda i: (0, 0))
    blk = pl.BlockSpec((ts, D), lambda i: (i, 0))
    vec = jax.ShapeDtypeStruct((1, D), F32)
    return pl.pallas_call(
        kern, name=name, grid=(S // ts,),
        in_specs=[blk, row, row, blk, blk], out_specs=[blk, row, row, row],
        out_shape=[jax.ShapeDtypeStruct((S, D), F32), vec, vec, vec],
        compiler_params=pltpu.CompilerParams(dimension_semantics=("arbitrary",), vmem_limit_bytes=VMEM_LIMIT),
    )(x, gain, sc, dh, d_in)


def _relu2_bwd(du, a, after, name):
    S, N = a.shape
    ts = _pick(S, 256, 16)

    def kern(du_ref, a_ref, after_ref, dz_ref):
        dz_ref[...] = (du_ref[...].astype(F32) * (2.0 * a_ref[...].astype(F32))).astype(dz_ref.dtype)

    blk = pl.BlockSpec((ts, N), lambda i: (i, 0))
    return pl.pallas_call(
        kern, name=name, grid=(S // ts,),
        in_specs=[blk, blk, pl.BlockSpec(after.shape, lambda i: (0, 0))], out_specs=blk,
        out_shape=jax.ShapeDtypeStruct((S, N), BF16),
        compiler_params=pltpu.CompilerParams(dimension_semantics=("parallel",), vmem_limit_bytes=VMEM_LIMIT),
    )(du, a, after)


def _gate_bwd(d, y, g, name):
    S, D = d.shape
    ts = _pick(S, 256, 16)

    def kern(d_ref, y_ref, g_ref, dy_ref, dg_ref):
        @pl.when(pl.program_id(0) == 0)
        def _():
            dg_ref[...] = jnp.zeros_like(dg_ref)

        dv = d_ref[...]
        dy_ref[...] = (g_ref[...] * dv).astype(dy_ref.dtype)
        dg_ref[...] += jnp.sum(dv * y_ref[...].astype(F32), axis=0, keepdims=True)

    row = pl.BlockSpec((1, D), lambda i: (0, 0))
    blk = pl.BlockSpec((ts, D), lambda i: (i, 0))
    return pl.pallas_call(
        kern, name=name, grid=(S // ts,), in_specs=[blk, blk, row], out_specs=[blk, row],
        out_shape=[jax.ShapeDtypeStruct((S, D), BF16), jax.ShapeDtypeStruct((1, D), F32)],
        compiler_params=pltpu.CompilerParams(dimension_semantics=("arbitrary",), vmem_limit_bytes=VMEM_LIMIT),
    )(d, y, g)


def _final_loss(x, gain, target, name):
    S, D = x.shape
    ts = _pick(S, 256, 8)

    def kern(x_ref, g_ref, t_ref, loss_ref, dx_ref, dg_ref):
        @pl.when(pl.program_id(0) == 0)
        def _():
            loss_ref[...] = jnp.zeros_like(loss_ref)
            dg_ref[...] = jnp.zeros_like(dg_ref)

        xv = x_ref[...]
        g = g_ref[...]
        inv = lax.rsqrt(jnp.mean(xv * xv, axis=-1, keepdims=True) + NORM_EPS)
        xn = xv * inv
        err = xn * g - t_ref[...]
        row_loss = jnp.mean(err * err, axis=-1, keepdims=True)
        loss_ref[...] += 0.5 * jnp.sum(row_loss, axis=0, keepdims=True)
        dy = err * (1.0 / D)
        dg_ref[...] += jnp.sum(dy * xn, axis=0, keepdims=True)
        dxn = dy * g
        dx_ref[...] = inv * (dxn - xn * jnp.mean(dxn * xn, axis=-1, keepdims=True))

    row = pl.BlockSpec((1, D), lambda i: (0, 0))
    blk = pl.BlockSpec((ts, D), lambda i: (i, 0))
    return pl.pallas_call(
        kern, name=name, grid=(S // ts,),
        in_specs=[blk, row, blk],
        out_specs=[pl.BlockSpec((1, 128), lambda i: (0, 0)), blk, row],
        out_shape=[jax.ShapeDtypeStruct((1, 128), F32), jax.ShapeDtypeStruct((S, D), F32),
                   jax.ShapeDtypeStruct((1, D), F32)],
        compiler_params=pltpu.CompilerParams(dimension_semantics=("arbitrary",), vmem_limit_bytes=VMEM_LIMIT),
    )(x, gain, target)


def _regroup_columns(w, outer, inner, name):
    rows = w.shape[0]

    def kern(i_ref, o_ref):
        o_ref[...] = i_ref[...]

    return pl.pallas_call(
        kern, name=name, grid=(outer, inner),
        in_specs=[pl.BlockSpec((rows, HEAD_DIM), lambda a, b: (0, a * inner + b))],
        out_specs=pl.BlockSpec((rows, HEAD_DIM), lambda a, b: (0, b * outer + a)),
        out_shape=jax.ShapeDtypeStruct(w.shape, w.dtype),
        compiler_params=pltpu.CompilerParams(
            dimension_semantics=("parallel", "parallel"), vmem_limit_bytes=VMEM_LIMIT),
    )(w)


def _heads_major(w, name):
    return _regroup_columns(w, 4, w.shape[1] // (4 * HEAD_DIM), name)


def _heads_minor(w, name):
    return _regroup_columns(w, w.shape[1] // (4 * HEAD_DIM), 4, name)


def _part(ref, j, rows=slice(None)):
    return ref[rows, j * HEAD_DIM:(j + 1) * HEAD_DIM]


def _chunk_scan(v, ric, reverse):
    n = v.shape[0]
    d = 1
    while d < A_CHUNK:
        if reverse:
            v = v + jnp.where(ric < A_CHUNK - d, pltpu.roll(v, n - d, 0), 0.0)
        else:
            v = v + jnp.where(ric >= d, pltpu.roll(v, d, 0), 0.0)
        d *= 2
    return v


def _hgrn2_gates(q_raw, f_logit, lbv, ric):
    qs = _silu(q_raw)
    sig = _sigmoid(f_logit)
    f = lbv + (1.0 - lbv) * sig
    log_f = jnp.log(f)
    kk = (1.0 - lbv) * _sigmoid(-f_logit)
    cum = _chunk_scan(log_f, ric, False)
    cl = cum + _chunk_scan(log_f, ric, True) - log_f
    e_cum = jnp.exp(cum)
    e_neg = jnp.exp(-cum)
    e_end = jnp.exp(cl - cum)
    dec = jnp.exp(cl)
    return qs, sig, f, kk, e_cum, e_neg, e_end, dec


def _hgrn2_specs(S, D, TB, reverse):
    H = D // HEAD_DIM
    NB = S // TB
    pos = (lambda nb: NB - 1 - nb) if reverse else (lambda nb: nb)
    proj = pl.BlockSpec((TB, 4 * HEAD_DIM), lambda h, nb: (pos(nb), h))
    head = pl.BlockSpec((TB, HEAD_DIM), lambda h, nb: (pos(nb), h))
    vec = pl.BlockSpec((1, HEAD_DIM), lambda h, nb: (0, h))
    state = pl.BlockSpec((TB // A_CHUNK, None, HEAD_DIM, HEAD_DIM), lambda h, nb: (pos(nb), h, 0, 0))
    return H, NB, proj, head, vec, state


def _hgrn2_fwd(proj, lb, gain, name):
    S, D4 = proj.shape
    D = D4 // 4
    TB = _pick(S, 512, A_CHUNK)
    H, NB, pspec, head, vec, state = _hgrn2_specs(S, D, TB, False)
    NCB = TB // A_CHUNK

    def kern(p_ref, lb_ref, gain_ref, og_ref, oraw_ref, st_ref, a_s, b_s, k_s, v_s, dec_s, o_s, st_s):
        @pl.when(pl.program_id(1) == 0)
        def _():
            st_s[...] = jnp.zeros_like(st_s)

        ric = lax.broadcasted_iota(jnp.int32, (TB, HEAD_DIM), 0) % A_CHUNK
        qs, _, _, kk, e_cum, e_neg, e_end, dec = _hgrn2_gates(_part(p_ref, 0), _part(p_ref, 1), lb_ref[...], ric)
        a_s[...] = (qs * e_cum).astype(BF16)
        b_s[...] = (kk * e_neg).astype(BF16)
        k_s[...] = (kk * e_end).astype(BF16)
        v_s[...] = _part(p_ref, 2).astype(BF16)
        dec_s[...] = dec
        tril = (lax.broadcasted_iota(jnp.int32, (A_CHUNK, A_CHUNK), 0)
                >= lax.broadcasted_iota(jnp.int32, (A_CHUNK, A_CHUNK), 1))

        def chunk(ci, carry):
            r = pl.multiple_of(ci * A_CHUNK, A_CHUNK)
            rows = pl.ds(r, A_CHUNK)
            a, b, k, v = a_s[rows, :], b_s[rows, :], k_s[rows, :], v_s[rows, :]
            st = st_s[...]
            st_bf = st.astype(BF16)
            st_ref[ci] = st_bf
            p = jnp.where(tril, _dot_nt(a, b), 0.0).astype(BF16)
            o_s[rows, :] = _dot_nn(p, v) + _dot_nt(a, st_bf)
            st_s[...] = dec_s[pl.ds(r, 1), :] * st + _dot_tn(v, k)
            return carry

        lax.fori_loop(0, NCB, chunk, 0)
        o = o_s[...]
        oraw_ref[...] = o
        on = o * lax.rsqrt(jnp.mean(o * o, axis=-1, keepdims=True) + NORM_EPS)
        og_ref[...] = ((on * gain_ref[...]) * _silu(_part(p_ref, 3))).astype(og_ref.dtype)

    tb_bf = pltpu.VMEM((TB, HEAD_DIM), BF16)
    tb_f = pltpu.VMEM((TB, HEAD_DIM), F32)
    return pl.pallas_call(
        kern, name=name, grid=(H, NB),
        in_specs=[pspec, vec, vec],
        out_specs=[head, head, state],
        out_shape=[jax.ShapeDtypeStruct((S, D), BF16), jax.ShapeDtypeStruct((S, D), F32),
                   jax.ShapeDtypeStruct((S // A_CHUNK, H, HEAD_DIM, HEAD_DIM), BF16)],
        scratch_shapes=[tb_bf, tb_bf, tb_bf, tb_bf, tb_f, tb_f, pltpu.VMEM((HEAD_DIM, HEAD_DIM), F32)],
        compiler_params=pltpu.CompilerParams(
            dimension_semantics=("parallel", "arbitrary"), vmem_limit_bytes=VMEM_LIMIT),
    )(proj, lb, gain)


def _hgrn2_bwd(proj, lb, gain, oraw, states, dog, name):
    S, D4 = proj.shape
    D = D4 // 4
    TB = _pick(S, 512, A_CHUNK)
    H, NB, pspec, head, vec, state = _hgrn2_specs(S, D, TB, True)
    NCB = TB // A_CHUNK

    def kern(p_ref, lb_ref, gain_ref, oraw_ref, st_ref, dog_ref, dp_ref, dlb_ref, dgain_ref,
             a_s, b_s, k_s, v_s, do_s, dec_s, da_s, db_s, dk_s, dv_s, ddec_s, dst_s):
        @pl.when(pl.program_id(1) == 0)
        def _():
            dst_s[...] = jnp.zeros_like(dst_s)
            dlb_ref[...] = jnp.zeros_like(dlb_ref)
            dgain_ref[...] = jnp.zeros_like(dgain_ref)

        ric = lax.broadcasted_iota(jnp.int32, (TB, HEAD_DIM), 0) % A_CHUNK
        lbv = lb_ref[...]
        q_raw = _part(p_ref, 0)
        qs, sig, f, kk, e_cum, e_neg, e_end, dec = _hgrn2_gates(q_raw, _part(p_ref, 1), lbv, ric)
        a32, b32, k32 = qs * e_cum, kk * e_neg, kk * e_end
        a_s[...] = a32.astype(BF16)
        b_s[...] = b32.astype(BF16)
        k_s[...] = k32.astype(BF16)
        v_s[...] = _part(p_ref, 2).astype(BF16)
        dec_s[...] = dec

        o = oraw_ref[...]
        gain_v = gain_ref[...]
        g_raw = _part(p_ref, 3)
        rinv = lax.rsqrt(jnp.mean(o * o, axis=-1, keepdims=True) + NORM_EPS)
        on = o * rinv
        dog_v = dog_ref[...].astype(F32)
        dp_ref[:, 3 * HEAD_DIM:4 * HEAD_DIM] = (dog_v * (on * gain_v) * _dsilu(g_raw)).astype(dp_ref.dtype)
        dog2 = dog_v * _silu(g_raw)
        dgain_ref[...] += jnp.sum(dog2 * on, axis=0, keepdims=True)
        don = dog2 * gain_v
        do_s[...] = (rinv * (don - on * jnp.mean(don * on, axis=-1, keepdims=True))).astype(BF16)

        tril = (lax.broadcasted_iota(jnp.int32, (A_CHUNK, A_CHUNK), 0)
                >= lax.broadcasted_iota(jnp.int32, (A_CHUNK, A_CHUNK), 1))

        def chunk(it, carry):
            ci = NCB - 1 - it
            r = pl.multiple_of(ci * A_CHUNK, A_CHUNK)
            rows = pl.ds(r, A_CHUNK)
            a, b, k, v, do = a_s[rows, :], b_s[rows, :], k_s[rows, :], v_s[rows, :], do_s[rows, :]
            st_prev = st_ref[ci]
            dst = dst_s[...]
            dst_bf = dst.astype(BF16)
            p = jnp.where(tril, _dot_nt(a, b), 0.0).astype(BF16)
            dp = jnp.where(tril, _dot_nt(do, v), 0.0).astype(BF16)
            dv_s[rows, :] = _dot_tn(p, do) + _dot_nt(k, dst_bf)
            da_s[rows, :] = _dot_nn(dp, b) + _dot_nn(do, st_prev)
            db_s[rows, :] = _dot_tn(dp, a)
            dk_s[rows, :] = _dot_nn(v, dst_bf)
            ddec = jnp.sum(dst * st_prev.astype(F32), axis=0, keepdims=True)
            ddec_s[rows, :] = jnp.broadcast_to(ddec, (A_CHUNK, HEAD_DIM))
            dst_s[...] = dec_s[pl.ds(r, 1), :] * dst + _dot_tn(do, a)
            return carry

        lax.fori_loop(0, NCB, chunk, 0)

        da, db, dk = da_s[...], db_s[...], dk_s[...]
        dqs = da * e_cum
        dkk = db * e_neg + dk * e_end
        w = dk * k32
        dlog_f = (_chunk_scan(da * a32 - db * b32, ric, True) + (_chunk_scan(w, ric, False) - w)
                  + ddec_s[...] * dec)
        dfg = dlog_f / f - dkk
        dlb_ref[...] += jnp.sum(dfg * (1.0 - sig), axis=0, keepdims=True)
        dp_ref[:, 0:HEAD_DIM] = (dqs * _dsilu(q_raw)).astype(dp_ref.dtype)
        dp_ref[:, HEAD_DIM:2 * HEAD_DIM] = (dfg * (1.0 - lbv) * sig * (1.0 - sig)).astype(dp_ref.dtype)
        dp_ref[:, 2 * HEAD_DIM:3 * HEAD_DIM] = dv_s[...].astype(dp_ref.dtype)

    tb_bf = pltpu.VMEM((TB, HEAD_DIM), BF16)
    tb_f = pltpu.VMEM((TB, HEAD_DIM), F32)
    vec_shape = jax.ShapeDtypeStruct((1, D), F32)
    return pl.pallas_call(
        kern, name=name, grid=(H, NB),
        in_specs=[pspec, vec, vec, head, state, head],
        out_specs=[pspec, vec, vec],
        out_shape=[jax.ShapeDtypeStruct((S, D4), BF16), vec_shape, vec_shape],
        scratch_shapes=[tb_bf, tb_bf, tb_bf, tb_bf, tb_bf, tb_f, tb_f, tb_f, tb_f, tb_f, tb_f,
                        pltpu.VMEM((HEAD_DIM, HEAD_DIM), F32)],
        compiler_params=pltpu.CompilerParams(
            dimension_semantics=("parallel", "arbitrary"), vmem_limit_bytes=VMEM_LIMIT),
    )(proj, lb, gain, oraw, states, dog)


def _split_sum(v, tri):
    hi = v.astype(BF16)
    lo = (v - hi.astype(F32)).astype(BF16)
    return _dot_nn(hi, tri) + _dot_nn(lo, tri)


def _sb_logits(qs, kj, scale):
    z = _dot_nt(qs, kj) * scale
    log_beta = jnp.minimum(z, 0.0) - jnp.log1p(jnp.exp(-jnp.abs(z)))
    return z, log_beta


def _stack(ref, G):
    return jnp.concatenate([ref[:, g * HEAD_DIM:(g + 1) * HEAD_DIM] for g in range(G)], axis=0)


def _unstack(v, G):
    return jnp.concatenate([v[g * Q_BLOCK:(g + 1) * Q_BLOCK, :] for g in range(G)], axis=1)


def _sb_specs(S, D, KVH):
    G = D // HEAD_DIM // KVH
    qblk = pl.BlockSpec((Q_BLOCK, G * HEAD_DIM), lambda h, qi: (qi, h))
    kblk = pl.BlockSpec((S, HEAD_DIM), lambda h, qi: (0, h))
    vblk = pl.BlockSpec((S, HEAD_DIM), lambda h, qi: (0, KVH + h))
    return G, qblk, kblk, vblk


def _sb_fwd(q, kv, name):
    S, D = q.shape
    KVH = kv.shape[1] // (2 * HEAD_DIM)
    G, qblk, kblk, vblk = _sb_specs(S, D, KVH)
    R = G * Q_BLOCK
    scale = HEAD_DIM ** -0.5

    def kern(q_ref, k_ref, v_ref, o_ref, tot_ref, acc_s, run_s):
        qi = pl.program_id(1)
        qs = _stack(q_ref, G)
        row = lax.broadcasted_iota(jnp.int32, (R, Q_BLOCK), 0) % Q_BLOCK
        colm = lax.broadcasted_iota(jnp.int32, (R, Q_BLOCK), 1)
        mask = colm < row
        ti = lax.broadcasted_iota(jnp.int32, (Q_BLOCK, Q_BLOCK), 0)
        tj = lax.broadcasted_iota(jnp.int32, (Q_BLOCK, Q_BLOCK), 1)
        after = (ti > tj).astype(BF16)
        acc_s[...] = jnp.zeros_like(acc_s)
        run_s[...] = jnp.zeros_like(run_s)

        def tile(j, masked):
            ks = pl.ds(pl.multiple_of(j * Q_BLOCK, Q_BLOCK), Q_BLOCK)
            kj, vj = k_ref[ks, :], v_ref[ks, :]
            z, log_beta = _sb_logits(qs, kj, scale)
            log_rest = log_beta - z
            if masked:
                log_rest = jnp.where(mask, log_rest, 0.0)
            between = _split_sum(log_rest, after) + run_s[...]
            w = jnp.exp(log_beta + between)
            if masked:
                w = jnp.where(mask, w, 0.0)
            acc_s[...] += _dot_nn(w.astype(BF16), vj)
            run_s[...] += jnp.sum(log_rest, axis=1, keepdims=True)

        tile(qi, True)

        def body(it, carry):
            tile(qi - 1 - it, False)
            return carry

        lax.fori_loop(0, qi, body, 0)
        o_ref[...] = _unstack(acc_s[...], G).astype(o_ref.dtype)
        tot_ref[...] = _unstack(run_s[...], G)

    return pl.pallas_call(
        kern, name=name, grid=(KVH, S // Q_BLOCK),
        in_specs=[qblk, kblk, vblk], out_specs=[qblk, qblk],
        out_shape=[jax.ShapeDtypeStruct((S, D), BF16), jax.ShapeDtypeStruct((S, D), F32)],
        scratch_shapes=[pltpu.VMEM((R, HEAD_DIM), F32), pltpu.VMEM((R, Q_BLOCK), F32)],
        compiler_params=pltpu.CompilerParams(
            dimension_semantics=("parallel", "arbitrary"), vmem_limit_bytes=VMEM_LIMIT),
    )(q, kv, kv)


def _sb_bwd(q, kv, tot, do, after, name):
    S, D = q.shape
    KVH = kv.shape[1] // (2 * HEAD_DIM)
    G, qblk, kblk, vblk = _sb_specs(S, D, KVH)
    R = G * Q_BLOCK
    scale = HEAD_DIM ** -0.5

    def kern(q_ref, k_ref, v_ref, tot_ref, do_ref, after_ref, dq_ref, dk_ref, dv_ref, dq_s, pre_s, esum_s):
        qi = pl.program_id(1)

        @pl.when(qi == 0)
        def _():
            dk_ref[...] = jnp.zeros_like(dk_ref)
            dv_ref[...] = jnp.zeros_like(dv_ref)

        qs = _stack(q_ref, G)
        dos = _stack(do_ref, G)
        total = _stack(tot_ref, G)
        row = lax.broadcasted_iota(jnp.int32, (R, Q_BLOCK), 0) % Q_BLOCK
        colm = lax.broadcasted_iota(jnp.int32, (R, Q_BLOCK), 1)
        mask = colm < row
        ti = lax.broadcasted_iota(jnp.int32, (Q_BLOCK, Q_BLOCK), 0)
        tj = lax.broadcasted_iota(jnp.int32, (Q_BLOCK, Q_BLOCK), 1)
        upto = (ti <= tj).astype(BF16)
        before = (ti < tj).astype(BF16)
        dq_s[...] = jnp.zeros_like(dq_s)
        pre_s[...] = jnp.zeros_like(pre_s)
        esum_s[...] = jnp.zeros_like(esum_s)

        def tile(j, masked):
            ks = pl.ds(pl.multiple_of(j * Q_BLOCK, Q_BLOCK), Q_BLOCK)
            kj, vj = k_ref[ks, :], v_ref[ks, :]
            z, log_beta = _sb_logits(qs, kj, scale)
            log_rest = log_beta - z
            if masked:
                log_rest = jnp.where(mask, log_rest, 0.0)
            between = total - (pre_s[...] + _split_sum(log_rest, upto))
            w = jnp.exp(log_beta + between)
            if masked:
                w = jnp.where(mask, w, 0.0)
            e = _dot_nt(dos, vj) * w
            e_before = esum_s[...] + _split_sum(e, before)
            beta = jnp.exp(log_beta)
            dz = e * (1.0 - beta) - e_before * beta
            if masked:
                dz = jnp.where(mask, dz, 0.0)
            dzs = (dz * scale).astype(BF16)
            dq_s[...] += _dot_nn(dzs, kj)
            dk_ref[ks, :] += _dot_tn(dzs, qs)
            dv_ref[ks, :] += _dot_tn(w.astype(BF16), dos)
            pre_s[...] += jnp.sum(log_rest, axis=1, keepdims=True)
            esum_s[...] += jnp.sum(e, axis=1, keepdims=True)

        def body(j, carry):
            tile(j, False)
            return carry

        lax.fori_loop(0, qi, body, 0)
        tile(qi, True)
        dq_ref[...] = _unstack(dq_s[...], G).astype(dq_ref.dtype)

    kvout = pl.BlockSpec((S, HEAD_DIM), lambda h, qi: (0, h))
    wide = pltpu.VMEM((R, Q_BLOCK), F32)
    return pl.pallas_call(
        kern, name=name, grid=(KVH, S // Q_BLOCK),
        in_specs=[qblk, kblk, vblk, qblk, qblk, pl.BlockSpec(after.shape, lambda h, qi: (0, 0))],
        out_specs=[qblk, kvout, kvout],
        out_shape=[jax.ShapeDtypeStruct((S, D), BF16), jax.ShapeDtypeStruct((S, KVH * HEAD_DIM), F32),
                   jax.ShapeDtypeStruct((S, KVH * HEAD_DIM), F32)],
        scratch_shapes=[pltpu.VMEM((R, HEAD_DIM), F32), wide, wide],
        compiler_params=pltpu.CompilerParams(
            dimension_semantics=("parallel", "arbitrary"), vmem_limit_bytes=VMEM_LIMIT),
    )(q, kv, kv, tot, do, after)


def _adamw(parts, w, m, v, name, layer=None, filled=None):
    shape = w.shape
    L = 1 if layer is None else shape[0]
    l = 0 if layer is None else layer
    C = shape[-1]
    R = w.size // (C * L)
    P = parts.shape[0]
    parts3 = parts.reshape(P, R, C)
    w3, m3, v3 = w.reshape(L, R, C), m.reshape(L, R, C), v.reshape(L, R, C)
    tr = _pick(R, max(8, (1 << 18) // C), 16)

    def kern(p_ref, w_ref, m_ref, v_ref, *rest):
        g_ref, d_ref, nm_ref, nv_ref = rest[-4:]
        g = p_ref[0].astype(F32)
        for i in range(1, P):
            g = g + p_ref[i].astype(F32)
        nm = ADAM_B1 * m_ref[...] + (1.0 - ADAM_B1) * g
        nv = ADAM_B2 * v_ref[...] + (1.0 - ADAM_B2) * (g * g)
        m_hat = nm / (1.0 - ADAM_B1 ** ADAM_STEP)
        v_hat = nv / (1.0 - ADAM_B2 ** ADAM_STEP)
        g_ref[...] = g
        d_ref[...] = -ADAM_LR * (m_hat / (jnp.sqrt(v_hat) + ADAM_EPS) + ADAM_WD * w_ref[...])
        nm_ref[...] = nm
        nv_ref[...] = nv

    blk = pl.BlockSpec((None, tr, C), lambda i: (l, i, 0))
    out = jax.ShapeDtypeStruct((L, R, C), F32)
    extra = [] if filled is None else [f.reshape(L, R, C) for f in filled]
    res = pl.pallas_call(
        kern, name=name, grid=(R // tr,),
        in_specs=[pl.BlockSpec((P, tr, C), lambda i: (0, i, 0)), blk, blk, blk]
        + [pl.BlockSpec(memory_space=pl.ANY)] * len(extra),
        out_specs=[blk, blk, blk, blk], out_shape=[out, out, out, out],
        input_output_aliases={4 + j: j for j in range(len(extra))},
        compiler_params=pltpu.CompilerParams(dimension_semantics=("parallel",), vmem_limit_bytes=VMEM_LIMIT),
    )(parts3, w3, m3, v3, *extra)
    return tuple(r.reshape(shape) for r in res)


def _lower_bound(logits):
    return jnp.cumsum(jax.nn.softmax(logits.astype(F32), axis=0), axis=0)[0:1]


def _pad_rows(a, rows):
    return jnp.zeros((rows, a.shape[1]), a.dtype).at[:a.shape[0]].set(a)


def kernel(x, c, ada_w, ada_b, norm_mix, norm_mlp, a_w_in, a_lb_logits, a_out_gain, a_w_out, kv_ada_w, kv_ada_b, kv_norm, w_kv, b_w_q, b_w_out, mlp_w1, mlp_w2, final_norm, loss_target, m_ada_w, m_ada_b, m_norm_mix, m_norm_mlp, m_a_w_in, m_a_lb_logits, m_a_out_gain, m_a_w_out, m_kv_ada_w, m_kv_ada_b, m_kv_norm, m_w_kv, m_b_w_q, m_b_w_out, m_mlp_w1, m_mlp_w2, m_final_norm, v_ada_w, v_ada_b, v_norm_mix, v_norm_mlp, v_a_w_in, v_a_lb_logits, v_a_out_gain, v_a_w_out, v_kv_ada_w, v_kv_ada_b, v_kv_norm, v_w_kv, v_b_w_q, v_b_w_out, v_mlp_w1, v_mlp_w2, v_final_norm):
    S, D = x.shape[1], x.shape[2]
    assert x.shape[0] == 1 and ada_w.shape[0] == 2 and a_w_in.shape[0] == 1 and b_w_q.shape[0] == 1
    me = 4 * lax.axis_index("x") + 2 * lax.axis_index("y") + lax.axis_index("c")
    dl = D // N_DEV
    na = ada_w.shape[2]
    nk = kv_ada_w.shape[1]

    w_names = ["a_w_in", "a_w_out", "mlp0_w1", "mlp0_w2", "w_kv", "b_w_q", "b_w_out", "mlp1_w1", "mlp1_w2"]
    w_shards = [a_w_in[0], a_w_out[0], mlp_w1[0], mlp_w2[0], w_kv, b_w_q[0], b_w_out[0], mlp_w1[1], mlp_w2[1]]
    w_axes = [1, 0, 1, 0, 0, 0, 0, 1, 0]
    w_shards = [s.astype(BF16) for s in w_shards]
    w_routes = [_gather_route(s, ax) for s, ax in zip(w_shards, w_axes)]
    w_handles, w_token = _copies_start(w_shards, w_routes, "gather_weights_start")

    def weight(i, after):
        return _copies_wait(w_handles[i], w_routes[i], after, "gather_" + w_names[i] + "_wait")

    small = jnp.concatenate([c.reshape(1, D), a_lb_logits.reshape(1, 2 * dl), a_out_gain.reshape(1, dl)], axis=1)
    small_all = _all_gather(small + w_token[0:1, 0:1], "gather_small")[:, 0, :]
    c_all = small_all[:, :D]
    lb_logits = small_all[:, D:D + 2 * dl].reshape(N_DEV, 2, dl).transpose(1, 0, 2).reshape(2, D)
    out_gain = small_all[:, D + 2 * dl:].reshape(1, D)

    c_act = jax.nn.silu(c_all)
    c_act_rows = _pad_rows(c_act.astype(BF16), 128)
    mod_cols = jnp.concatenate([
        _matmul(c_act_rows, ada_w[0].astype(BF16), "nn", (F32,), "ada0")[0][:N_DEV],
        _matmul(c_act_rows, ada_w[1].astype(BF16), "nn", (F32,), "ada1")[0][:N_DEV],
        _matmul(c_act_rows, kv_ada_w.astype(BF16), "nn", (F32,), "ada_kv")[0][:N_DEV]], axis=1)
    mod_all = _all_gather(mod_cols, "gather_mod")
    mod_mine = lax.dynamic_index_in_dim(mod_all, me, axis=1, keepdims=False)
    mod0 = mod_mine[:, :na].reshape(1, 6 * D) + ada_b[0:1]
    mod1 = mod_mine[:, na:2 * na].reshape(1, 6 * D) + ada_b[1:2]
    modk = mod_mine[:, 2 * na:].reshape(1, 2 * D) + kv_ada_b.reshape(1, 2 * D)
    sh1a, sc1a, g1a, sh2a, sc2a, g2a = jnp.split(mod0, 6, axis=1)
    sh1b, sc1b, g1b, sh2b, sc2b, g2b = jnp.split(mod1, 6, axis=1)
    kv_sh, kv_sc = jnp.split(modk, 2, axis=1)
    nmix0, nmix1, nmlp0, nmlp1 = norm_mix[0:1], norm_mix[1:2], norm_mlp[0:1], norm_mlp[1:2]
    kvn = kv_norm.reshape(1, D)
    lb, lb_vjp = jax.vjp(_lower_bound, lb_logits)

    x0 = x[0]
    h1 = _norm_mod_fwd(x0, nmix0, sc1a, sh1a, "a_in_norm")
    w_in = _heads_major(weight(0, h1), "a_w_in_by_head")
    (proj,) = _matmul(h1, w_in, "nn", (F32,), "a_in_mm")
    og, oraw, states = _hgrn2_fwd(proj, lb, out_gain, "hgrn2_fwd")
    w_aout = weight(1, og)
    x1, y1 = _matmul(og, w_aout, "nn", (F32, BF16), "a_out_mm", _residual_epilogue, (x0,), (g1a,))
    h2 = _norm_mod_fwd(x1, nmlp0, sc2a, sh2a, "mlp0_up_norm")
    w1a = weight(2, h2)
    u0, a0 = _matmul(h2, w1a, "nn", (BF16, BF16), "mlp0_up_mm", _relu2_epilogue)
    w2a = weight(3, u0)
    x2, y2 = _matmul(u0, w2a, "nn", (F32, BF16), "mlp0_down_mm", _residual_epilogue, (x1,), (g2a,))

    hk = _norm_mod_fwd(x2, kvn, kv_sc, kv_sh, "kv_norm")
    w_kvf = weight(4, hk)
    (kv,) = _matmul(hk, w_kvf, "nn", (BF16,), "kv_mm")
    h3 = _norm_mod_fwd(x2, nmix1, sc1b, sh1b, "b_q_norm")
    w_q = weight(5, h3)
    (q,) = _matmul(h3, w_q, "nn", (BF16,), "b_q_mm")
    o, tot = _sb_fwd(q, kv, "attn_fwd")
    w_bout = weight(6, o)
    x3, y3 = _matmul(o, w_bout, "nn", (F32, BF16), "b_out_mm", _residual_epilogue, (x2,), (g1b,))
    h4 = _norm_mod_fwd(x3, nmlp1, sc2b, sh2b, "mlp1_up_norm")
    w1b = weight(7, h4)
    u1, a1 = _matmul(h4, w1b, "nn", (BF16, BF16), "mlp1_up_mm", _relu2_epilogue)
    w2b = weight(8, u1)
    x4, y4 = _matmul(u1, w2b, "nn", (F32, BF16), "mlp1_down_mm", _residual_epilogue, (x3,), (g2b,))

    loss_vec, d4, d_final = _final_loss(x4, final_norm.reshape(1, D), loss_target[0], "final_loss")
    loss = lax.psum(loss_vec[0, 0], MESH_AXES)

    sent = {}

    def send(key, g, axis):
        route = _scatter_route(g, axis)
        (handle,), token = _copies_start([g], [route], "scatter_" + key + "_start")
        sent[key] = (handle, route)
        return token

    def behind(vec, token):
        return vec + token[0:1, 0:1]

    def mlp_bwd(tag, d, y, g, u, a, w2, w1, h, x_in, gain, sc):
        dy, dg = _gate_bwd(d, y, g, tag + "_down_dgate")
        (du,) = _matmul(dy, w2, "nt", (BF16,), tag + "_down_du")
        (dw2,) = _matmul(u, dy, "tn", (BF16,), tag + "_down_dw")
        token = send(tag + "_w2", dw2, 0)
        dz = _relu2_bwd(du, a, token, tag + "_up_dact")
        (dh,) = _matmul(dz, w1, "nt", (F32,), tag + "_up_dh")
        (dw1,) = _matmul(h, dz, "tn", (BF16,), tag + "_up_dw")
        token = send(tag + "_w1", dw1, 1)
        d_out, dgain, dsc, dsh = _norm_mod_bwd(x_in, gain, behind(sc, token), dh, d, tag + "_up_dnorm")
        return d_out, dg, dgain, dsc, dsh

    d3, dg2b, dnmlp1, dsc2b, dsh2b = mlp_bwd("mlp1", d4, y4, g2b, u1, a1, w2b, w1b, h4, x3, nmlp1, sc2b)

    dy3, dg1b = _gate_bwd(d3, y3, g1b, "b_out_dgate")
    (do,) = _matmul(dy3, w_bout, "nt", (BF16,), "b_out_du")
    (dw_bout,) = _matmul(o, dy3, "tn", (BF16,), "b_out_dw")
    token = send("b_w_out", dw_bout, 0)
    dq, dk, dv = _sb_bwd(q, kv, tot, do, token, "attn_bwd")
    dkv = jnp.concatenate([dk, dv], axis=1).astype(BF16)
    (dh3,) = _matmul(dq, w_q, "nt", (F32,), "b_q_dh")
    (dw_q,) = _matmul(h3, dq, "tn", (BF16,), "b_q_dw")
    token = send("b_w_q", dw_q, 0)
    d2, dnmix1, dsc1b, dsh1b = _norm_mod_bwd(x2, nmix1, behind(sc1b, token), dh3, d3, "b_q_dnorm")
    (dhk,) = _matmul(dkv, w_kvf, "nt", (F32,), "kv_dh")
    (dw_kv,) = _matmul(hk, dkv, "tn", (BF16,), "kv_dw")
    token = send("w_kv", dw_kv, 0)
    d2, dkvn, dkv_sc, dkv_sh = _norm_mod_bwd(x2, kvn, behind(kv_sc, token), dhk, d2, "kv_dnorm")

    d1, dg2a, dnmlp0, dsc2a, dsh2a = mlp_bwd("mlp0", d2, y2, g2a, u0, a0, w2a, w1a, h2, x1, nmlp0, sc2a)

    dy1, dg1a = _gate_bwd(d1, y1, g1a, "a_out_dgate")
    (dog,) = _matmul(dy1, w_aout, "nt", (BF16,), "a_out_du")
    (dw_aout,) = _matmul(og, dy1, "tn", (BF16,), "a_out_dw")
    token = send("a_w_out", dw_aout, 0)
    dproj, dlb, d_out_gain = _hgrn2_bwd(proj, behind(lb, token), out_gain, oraw, states, dog, "hgrn2_bwd")
    (dh1,) = _matmul(dproj, w_in, "nt", (F32,), "a_in_dh")
    (dw_in,) = _matmul(h1, dproj, "tn", (BF16,), "a_in_dw")
    token = send("a_w_in", _heads_minor(dw_in, "a_w_in_grad_by_part"), 1)
    d0, dnmix0, dsc1a, dsh1a = _norm_mod_bwd(x0, nmix0, behind(sc1a, token), dh1, d1, "a_in_dnorm")
    (d_lb_logits,) = lb_vjp(dlb)

    dmod0 = jnp.concatenate([dsh1a, dsc1a, dg1a, dsh2a, dsc2a, dg2a], axis=1)
    dmod1 = jnp.concatenate([dsh1b, dsc1b, dg1b, dsh2b, dsc2b, dg2b], axis=1)
    dmodk = jnp.concatenate([dkv_sh, dkv_sc], axis=1)
    pieces = [dmod0, dmod1, dmodk, dnmix0, dnmix1, dnmlp0, dnmlp1, dkvn, d_final,
              d_lb_logits.reshape(1, 2 * D), d_out_gain]
    widths = [p.shape[1] for p in pieces]
    offs = [sum(widths[:i]) for i in range(len(widths))]
    part_all = _all_gather(jnp.concatenate(pieces, axis=1), "gather_dsmall")[:, 0, :]
    take = lambda i: part_all[:, offs[i]:offs[i] + widths[i]]
    dmod0_all, dmod1_all, dmodk_all = take(0), take(1), take(2)

    outs = {}

    def update(key, parts, w, m, v, **kw):
        outs[key] = _adamw(parts, w, m, v, "adamw_" + key + ("_%d" % kw["layer"] if "layer" in kw else ""), **kw)

    c_act_cols = _pad_rows(c_act.astype(BF16), 128).T
    my_cols = lambda a, n: _pad_rows(lax.dynamic_slice_in_dim(a, me * n, n, axis=1).astype(BF16), 128)
    g_ada0 = _matmul(c_act_cols, my_cols(dmod0_all, na), "nn", (F32,), "dada0")[0]
    g_ada1 = _matmul(c_act_cols, my_cols(dmod1_all, na), "nn", (F32,), "dada1")[0]
    g_adak = _matmul(c_act_cols, my_cols(dmodk_all, nk), "nn", (F32,), "dada_kv")[0]
    update("ada_w", g_ada1[None], ada_w, m_ada_w, v_ada_w, layer=1)
    update("ada_w", g_ada0[None], ada_w, m_ada_w, v_ada_w, layer=0, filled=outs["ada_w"])
    update("kv_ada_w", g_adak[None], kv_ada_w, m_kv_ada_w, v_kv_ada_w)
    update("ada_b", jnp.stack([dmod0_all, dmod1_all], axis=1), ada_b, m_ada_b, v_ada_b)
    update("kv_ada_b", dmodk_all, kv_ada_b, m_kv_ada_b, v_kv_ada_b)
    update("norm_mix", jnp.stack([take(3), take(4)], axis=1), norm_mix, m_norm_mix, v_norm_mix)
    update("norm_mlp", jnp.stack([take(5), take(6)], axis=1), norm_mlp, m_norm_mlp, v_norm_mlp)
    update("kv_norm", take(7), kv_norm, m_kv_norm, v_kv_norm)
    update("final_norm", take(8), final_norm, m_final_norm, v_final_norm)
    d_lb_all = take(9).reshape(N_DEV, 2, D)
    update("a_lb_logits", lax.dynamic_slice_in_dim(d_lb_all, me * dl, dl, axis=2), a_lb_logits, m_a_lb_logits,
           v_a_lb_logits)
    update("a_out_gain", lax.dynamic_slice_in_dim(take(10), me * dl, dl, axis=1)[:, None, :], a_out_gain,
           m_a_out_gain, v_a_out_gain)

    def landed(key, after):
        handle, route = sent[key]
        return _copies_wait(handle, route, after, "scatter_" + key + "_wait")

    update("mlp_w2", landed("mlp1_w2", d0), mlp_w2, m_mlp_w2, v_mlp_w2, layer=1)
    update("mlp_w1", landed("mlp1_w1", d0), mlp_w1, m_mlp_w1, v_mlp_w1, layer=1)
    update("b_w_out", landed("b_w_out", d0), b_w_out, m_b_w_out, v_b_w_out)
    update("b_w_q", landed("b_w_q", d0), b_w_q, m_b_w_q, v_b_w_q)
    update("w_kv", landed("w_kv", d0), w_kv, m_w_kv, v_w_kv)
    update("mlp_w2", landed("mlp0_w2", d0), mlp_w2, m_mlp_w2, v_mlp_w2, layer=0, filled=outs["mlp_w2"])
    update("mlp_w1", landed("mlp0_w1", d0), mlp_w1, m_mlp_w1, v_mlp_w1, layer=0, filled=outs["mlp_w1"])
    update("a_w_out", landed("a_w_out", d0), a_w_out, m_a_w_out, v_a_w_out)
    update("a_w_in", landed("a_w_in", outs["mlp_w1"][0]), a_w_in, m_a_w_in, v_a_w_in)

    names = ["ada_w", "ada_b", "norm_mix", "norm_mlp", "a_w_in", "a_lb_logits", "a_out_gain", "a_w_out", "kv_ada_w",
             "kv_ada_b", "kv_norm", "w_kv", "b_w_q", "b_w_out", "mlp_w1", "mlp_w2", "final_norm"]
    result = [loss, d0[None]]
    for field in range(4):
        result += [outs[n][field] for n in names]
    return tuple(result)
```

```python
import jax
import jax.numpy as jnp
from jax import lax
from jax.experimental import pallas as pl
from jax.experimental.pallas import tpu as pltpu

F32 = jnp.float32
BF16 = jnp.bfloat16

N_DEV = 8
MESH_AXES = ("x", "y", "c")
HEAD_DIM = 128
A_CHUNK = 16
HGRN2_UNROLL = 4
SB_TILE = 256
NORM_EPS = 1e-6
ADAM_LR = 0.001
ADAM_B1 = 0.9
ADAM_B2 = 0.999
ADAM_EPS = 1e-08
ADAM_WD = 0.01
ADAM_STEP = 10
VMEM_LIMIT = 56 * 1024 * 1024
MATMUL_VMEM = 40 * 1024 * 1024


def _pick(dim, target, align):
    t = (min(dim, target) // align) * align
    while t >= align:
        if dim % t == 0:
            return t
        t -= align
    return dim


def _sigmoid(x):
    return 1.0 / (1.0 + jnp.exp(-x))


def _silu(x):
    return x * _sigmoid(x)


def _dsilu(x):
    s = _sigmoid(x)
    return s * (1.0 + x * (1.0 - s))


def _dot(a, b, dims):
    return lax.dot_general(a, b, (dims, ((), ())), preferred_element_type=F32)


def _dot_nn(a, b):
    return _dot(a, b, ((1,), (0,)))


def _dot_nt(a, b):
    return _dot(a, b, ((1,), (1,)))


def _dot_tn(a, b):
    return _dot(a, b, ((0,), (0,)))


def _mesh_pos():
    return lax.axis_index("x"), lax.axis_index("y"), lax.axis_index("c")


def _flip(v, d):
    return 1 - v if d else v


def _all_gather(x, name):
    def body(x_ref, out_ref, send_sems, recv_sems, local_sem):
        x, y, c = _mesh_pos()
        me, sibling = (x, y, c), (x, y, 1 - c)
        chips = [(1 - x, y), (x, 1 - y), (1 - x, 1 - y)]

        def rows(px, py, pc):
            return out_ref.at[4 * px + 2 * py + pc]

        def copy(k, block, to, src=None):
            return pltpu.make_async_remote_copy(
                src_ref=rows(*block) if src is None else src,
                dst_ref=rows(*block),
                send_sem=send_sems.at[k],
                recv_sem=recv_sems.at[k],
                device_id=to,
                device_id_type=pl.DeviceIdType.MESH,
            )

        mine = pltpu.make_async_copy(x_ref, rows(*me), local_sem)
        mine.start()
        first = [copy(0, me, sibling, src=x_ref)]
        first += [copy(1 + j, me, (*chip, c), src=x_ref) for j, chip in enumerate(chips)]
        for cp in first:
            cp.start()
        passed = [copy(4 + j, (*chip, c), sibling) for j, chip in enumerate(chips)]
        for j, chip in enumerate(chips):
            copy(1 + j, (*chip, c), me).wait_recv()
            passed[j].start()
        copy(0, sibling, me).wait_recv()
        for j, chip in enumerate(chips):
            copy(4 + j, (*chip, 1 - c), me).wait_recv()
        for cp in first + passed:
            cp.wait_send()
        mine.wait()

    return pl.pallas_call(
        body,
        name=name,
        out_shape=jax.ShapeDtypeStruct((N_DEV,) + x.shape, x.dtype),
        in_specs=[pl.BlockSpec(memory_space=pl.ANY)],
        out_specs=pl.BlockSpec(memory_space=pl.ANY),
        scratch_shapes=[
            pltpu.SemaphoreType.DMA((7,)),
            pltpu.SemaphoreType.DMA((7,)),
            pltpu.SemaphoreType.DMA(()),
        ],
    )(x)


_RELATIONS = [(dx, dy, dc) for dx in (0, 1) for dy in (0, 1) for dc in (0, 1) if (dx, dy, dc) != (0, 0, 0)]
_HBM = pl.BlockSpec(memory_space=pltpu.HBM)
_SEM = pl.BlockSpec(memory_space=pltpu.SEMAPHORE)
_EFFECT = pltpu.SideEffectType.DATAFLOW_SIDE_EFFECTING


def _at(ref, idx):
    return ref.at[idx] if idx else ref


def _block(axis, ndim, n):
    return lambda d: (slice(None),) * axis + (pl.ds(d * n, n),) + (slice(None),) * (ndim - axis - 1)


class _Route:
    def __init__(self, land_shape, src_slice, dst_slice):
        self.land_shape, self.src_slice, self.dst_slice = tuple(land_shape), src_slice, dst_slice


def _gather_route(shard, axis):
    n = shard.shape[axis]
    shape = shard.shape[:axis] + (N_DEV * n,) + shard.shape[axis + 1:]
    return _Route(shape, lambda p: (), _block(axis, shard.ndim, n))


def _scatter_route(g, axis):
    n = g.shape[axis] // N_DEV
    shape = (N_DEV,) + g.shape[:axis] + (n,) + g.shape[axis + 1:]
    return _Route(shape, _block(axis, g.ndim, n), lambda p: (p,))


def _peers():
    x, y, c = _mesh_pos()
    out = []
    for k, (dx, dy, dc) in enumerate(_RELATIONS):
        px, py, pc = _flip(x, dx), _flip(y, dy), _flip(c, dc)
        out.append((k, (px, py, pc), 4 * px + 2 * py + pc))
    return 4 * x + 2 * y + c, out


_LOCAL = len(_RELATIONS)


def _copies_start(srcs, routes, after, name):
    n = len(srcs)

    def body(*refs):
        src_refs, land_refs = refs[:n], refs[n:2 * n]
        outs = refs[2 * n + 1:]
        send, recv, token = outs[:n], outs[n:2 * n], outs[-1]
        me, peers = _peers()
        for i, route in enumerate(routes):
            for k, peer, pid in peers:
                pltpu.make_async_remote_copy(
                    src_ref=_at(src_refs[i], route.src_slice(pid)), dst_ref=_at(land_refs[i], route.dst_slice(me)),
                    send_sem=send[i].at[k], recv_sem=recv[i].at[k],
                    device_id=peer, device_id_type=pl.DeviceIdType.MESH).start()
            pltpu.make_async_copy(_at(src_refs[i], route.src_slice(me)), _at(land_refs[i], route.dst_slice(me)),
                                  send[i].at[_LOCAL]).start()
        token[...] = jnp.zeros_like(token)

    lands = [lax.empty(r.land_shape, s.dtype) for s, r in zip(srcs, routes)]
    hbm = lambda a: pltpu.with_memory_space_constraint(a, pltpu.HBM)
    res = pl.pallas_call(
        body, name=name,
        out_shape=([pltpu.SemaphoreType.DMA((_LOCAL + 1,))] * n + [pltpu.SemaphoreType.DMA((_LOCAL,))] * n
                   + [pltpu.HBM(s.shape, s.dtype) for s in srcs]
                   + [pltpu.HBM(l.shape, l.dtype) for l in lands]
                   + [jax.ShapeDtypeStruct((8, 128), F32)]),
        in_specs=[_HBM] * (2 * n) + [pl.BlockSpec(memory_space=pl.ANY)],
        out_specs=[_SEM] * (2 * n) + [_HBM] * (2 * n) + [pl.BlockSpec(memory_space=pltpu.VMEM)],
        input_output_aliases={i: 2 * n + i for i in range(2 * n)},
        compiler_params=pltpu.CompilerParams(has_side_effects=_EFFECT),
    )(*[hbm(s) for s in srcs], *[hbm(l) for l in lands], after)
    handles = [(res[i], res[n + i], res[2 * n + i], res[3 * n + i]) for i in range(n)]
    return handles, res[-1]


def _copies_wait(handle, route, after, name):
    send_sems, recv_sems, src, land = handle

    def body(src_ref, land_ref, send_ref, recv_ref, after_ref, src_out, got_ref):
        me, peers = _peers()
        for k, peer, pid in peers:
            cp = pltpu.make_async_remote_copy(
                src_ref=_at(src_ref, route.src_slice(pid)), dst_ref=_at(land_ref, route.dst_slice(pid)),
                send_sem=send_ref.at[k], recv_sem=recv_ref.at[k],
                device_id=peer, device_id_type=pl.DeviceIdType.MESH)
            cp.wait_send()
            cp.wait_recv()
        pltpu.make_async_copy(_at(src_ref, route.src_slice(me)), _at(land_ref, route.dst_slice(me)),
                              send_ref.at[_LOCAL]).wait()

    return pl.pallas_call(
        body, name=name,
        out_shape=(pltpu.HBM(src.shape, src.dtype), pltpu.HBM(land.shape, land.dtype)),
        in_specs=[_HBM, _HBM, _SEM, _SEM, pl.BlockSpec(memory_space=pl.ANY)],
        out_specs=(_HBM, _HBM),
        input_output_aliases={0: 0, 1: 1},
        compiler_params=pltpu.CompilerParams(has_side_effects=_EFFECT),
    )(src, land, send_sems, recv_sems, after)[1]


def _matmul_tiles(M, N, K, a_item, b_item, mn_bytes):
    tm, tk = _pick(M, 512, 128), _pick(K, 2048, 128)
    tn = _pick(N, 2048, 128)
    while True:
        cast = (tm * tk * 2 if a_item != 2 else 0) + (tk * tn * 2 if b_item != 2 else 0)
        need = 2 * (tm * tk * a_item + tk * tn * b_item + tm * tn * mn_bytes) + 2 * tm * tn * 4 + cast
        smaller = _pick(N, tn - 128, 128) if tn > 128 else tn
        if need <= MATMUL_VMEM or smaller >= tn:
            return tm, tn, tk
        tn = smaller


def _matmul(a, b, mode, out_dtypes, name, epilogue=None, tiles=(), rows=()):
    if mode == "nn":
        (M, K), (K2, N) = a.shape, b.shape
    elif mode == "nt":
        (M, K), (N, K2) = a.shape, b.shape
    else:
        (K, M), (K2, N) = a.shape, b.shape
    assert K == K2, (a.shape, b.shape, mode)
    mn_bytes = sum(t.dtype.itemsize for t in tiles) + sum(jnp.dtype(d).itemsize for d in out_dtypes)
    tm, tn, tk = _matmul_tiles(M, N, K, a.dtype.itemsize, b.dtype.itemsize, mn_bytes)
    nm, nn, nk = M // tm, N // tn, K // tk
    n_extra = len(tiles) + len(rows)
    n_out = len(out_dtypes)
    if epilogue is None:
        epilogue = lambda acc: (acc,)
    dims = {"nn": ((1,), (0,)), "nt": ((1,), (1,)), "tn": ((0,), (0,))}[mode]

    def finish(acc, extra, outs):
        vals = epilogue(acc, *[r[...] for r in extra])
        for o_ref, v in zip(outs, vals):
            o_ref[...] = v.astype(o_ref.dtype)

    def kern(*refs):
        a_ref, b_ref = refs[0], refs[1]
        extra = refs[2:2 + n_extra]
        outs = refs[2 + n_extra:2 + n_extra + n_out]
        part = _dot(a_ref[...].astype(BF16), b_ref[...].astype(BF16), dims)
        if nk == 1:
            finish(part, extra, outs)
            return
        acc_ref = refs[-1]
        k = pl.program_id(2)

        @pl.when(k == 0)
        def _():
            acc_ref[...] = part

        @pl.when(k > 0)
        def _():
            acc_ref[...] += part

        @pl.when(k == nk - 1)
        def _():
            finish(acc_ref[...], extra, outs)

    a_bytes, b_bytes = a.size * a.dtype.itemsize, b.size * b.dtype.itemsize
    m_outer = a_bytes + nm * b_bytes <= nn * a_bytes + b_bytes
    ij = (lambda g0, g1: (g0, g1)) if m_outer else (lambda g0, g1: (g1, g0))

    def spec(shape, fn):
        return pl.BlockSpec(shape, lambda g0, g1, k: fn(*ij(g0, g1), k))

    a_spec = {"nn": spec((tm, tk), lambda i, j, k: (i, k)),
              "nt": spec((tm, tk), lambda i, j, k: (i, k)),
              "tn": spec((tk, tm), lambda i, j, k: (k, i))}[mode]
    b_spec = {"nn": spec((tk, tn), lambda i, j, k: (k, j)),
              "nt": spec((tn, tk), lambda i, j, k: (j, k)),
              "tn": spec((tk, tn), lambda i, j, k: (k, j))}[mode]
    tile_spec = spec((tm, tn), lambda i, j, k: (i, j))
    row_spec = spec((1, tn), lambda i, j, k: (0, j))
    return pl.pallas_call(
        kern,
        name=name,
        grid=(nm, nn, nk) if m_outer else (nn, nm, nk),
        in_specs=[a_spec, b_spec] + [tile_spec] * len(tiles) + [row_spec] * len(rows),
        out_specs=[tile_spec] * n_out,
        out_shape=[jax.ShapeDtypeStruct((M, N), dt) for dt in out_dtypes],
        scratch_shapes=[pltpu.VMEM((tm, tn), F32)] if nk > 1 else [],
        compiler_params=pltpu.CompilerParams(
            dimension_semantics=("parallel", "parallel", "arbitrary"), vmem_limit_bytes=VMEM_LIMIT),
    )(a, b, *tiles, *rows)


def _relu2_epilogue(acc):
    a = jnp.maximum(acc, 0.0)
    return a * a, a


def _residual_epilogue(acc, x_tile, gate_row):
    return x_tile + gate_row * acc, acc


def _norm_mod_fwd(x, gain, sc, sh, name):
    S, D = x.shape
    ts = _pick(S, 256, 16)

    def kern(x_ref, g_ref, sc_ref, sh_ref, h_ref):
        xv = x_ref[...]
        inv = lax.rsqrt(jnp.mean(xv * xv, axis=-1, keepdims=True) + NORM_EPS)
        h = (xv * inv) * g_ref[...] * (1.0 + sc_ref[...]) + sh_ref[...]
        h_ref[...] = h.astype(h_ref.dtype)

    row = pl.BlockSpec((1, D), lambda i: (0, 0))
    blk = pl.BlockSpec((ts, D), lambda i: (i, 0))
    return pl.pallas_call(
        kern, name=name, grid=(S // ts,),
        in_specs=[blk, row, row, row], out_specs=blk,
        out_shape=jax.ShapeDtypeStruct((S, D), BF16),
        compiler_params=pltpu.CompilerParams(dimension_semantics=("parallel",), vmem_limit_bytes=VMEM_LIMIT),
    )(x, gain, sc, sh)


def _norm_mod_bwd(x, gain, sc, dh, d_in, name):
    S, D = x.shape
    ts = _pick(S, 256, 8)

    def kern(x_ref, g_ref, sc_ref, dh_ref, din_ref, dx_ref, dg_ref, dsc_ref, dsh_ref):
        @pl.when(pl.program_id(0) == 0)
        def _():
            dg_ref[...] = jnp.zeros_like(dg_ref)
            dsc_ref[...] = jnp.zeros_like(dsc_ref)
            dsh_ref[...] = jnp.zeros_like(dsh_ref)

        xv = x_ref[...]
        dh = dh_ref[...].astype(F32)
        inv = lax.rsqrt(jnp.mean(xv * xv, axis=-1, keepdims=True) + NORM_EPS)
        xn = xv * inv
        g = g_ref[...]
        dsh_ref[...] += jnp.sum(dh, axis=0, keepdims=True)
        dsc_ref[...] += jnp.sum(dh * (xn * g), axis=0, keepdims=True)
        dn = dh * (1.0 + sc_ref[...])
        dg_ref[...] += jnp.sum(dn * xn, axis=0, keepdims=True)
        dxn = dn * g
        dx_ref[...] = din_ref[...] + inv * (dxn - xn * jnp.mean(dxn * xn, axis=-1, keepdims=True))

    row = pl.BlockSpec((1, D), lambda i: (0, 0))
    blk = pl.BlockSpec((ts, D), lambda i: (i, 0))
    vec = jax.ShapeDtypeStruct((1, D), F32)
    return pl.pallas_call(
        kern, name=name, grid=(S // ts,),
        in_specs=[blk, row, row, blk, blk], out_specs=[blk, row, row, row],
        out_shape=[jax.ShapeDtypeStruct((S, D), F32), vec, vec, vec],
        compiler_params=pltpu.CompilerParams(dimension_semantics=("arbitrary",), vmem_limit_bytes=VMEM_LIMIT),
    )(x, gain, sc, dh, d_in)


def _relu2_bwd(du, a, after, name):
    S, N = a.shape
    ts = _pick(S, 256, 16)

    def kern(du_ref, a_ref, after_ref, dz_ref):
        dz_ref[...] = (du_ref[...].astype(F32) * (2.0 * a_ref[...].astype(F32))).astype(dz_ref.dtype)

    blk = pl.BlockSpec((ts, N), lambda i: (i, 0))
    return pl.pallas_call(
        kern, name=name, grid=(S // ts,),
        in_specs=[blk, blk, pl.BlockSpec(after.shape, lambda i: (0, 0))], out_specs=blk,
        out_shape=jax.ShapeDtypeStruct((S, N), BF16),
        compiler_params=pltpu.CompilerParams(dimension_semantics=("parallel",), vmem_limit_bytes=VMEM_LIMIT),
    )(du, a, after)


def _gate_bwd(d, y, g, name):
    S, D = d.shape
    ts = _pick(S, 256, 16)

    def kern(d_ref, y_ref, g_ref, dy_ref, dg_ref):
        @pl.when(pl.program_id(0) == 0)
        def _():
            dg_ref[...] = jnp.zeros_like(dg_ref)

        dv = d_ref[...]
        dy_ref[...] = (g_ref[...] * dv).astype(dy_ref.dtype)
        dg_ref[...] += jnp.sum(dv * y_ref[...].astype(F32), axis=0, keepdims=True)

    row = pl.BlockSpec((1, D), lambda i: (0, 0))
    blk = pl.BlockSpec((ts, D), lambda i: (i, 0))
    return pl.pallas_call(
        kern, name=name, grid=(S // ts,), in_specs=[blk, blk, row], out_specs=[blk, row],
        out_shape=[jax.ShapeDtypeStruct((S, D), BF16), jax.ShapeDtypeStruct((1, D), F32)],
        compiler_params=pltpu.CompilerParams(dimension_semantics=("arbitrary",), vmem_limit_bytes=VMEM_LIMIT),
    )(d, y, g)


def _final_loss(x, gain, target, name):
    S, D = x.shape
    ts = _pick(S, 256, 8)

    def kern(x_ref, g_ref, t_ref, loss_ref, dx_ref, dg_ref):
        @pl.when(pl.program_id(0) == 0)
        def _():
            loss_ref[...] = jnp.zeros_like(loss_ref)
            dg_ref[...] = jnp.zeros_like(dg_ref)

        xv = x_ref[...]
        g = g_ref[...]
        inv = lax.rsqrt(jnp.mean(xv * xv, axis=-1, keepdims=True) + NORM_EPS)
        xn = xv * inv
        err = xn * g - t_ref[...]
        row_loss = jnp.mean(err * err, axis=-1, keepdims=True)
        loss_ref[...] += 0.5 * jnp.sum(row_loss, axis=0, keepdims=True)
        dy = err * (1.0 / D)
        dg_ref[...] += jnp.sum(dy * xn, axis=0, keepdims=True)
        dxn = dy * g
        dx_ref[...] = inv * (dxn - xn * jnp.mean(dxn * xn, axis=-1, keepdims=True))

    row = pl.BlockSpec((1, D), lambda i: (0, 0))
    blk = pl.BlockSpec((ts, D), lambda i: (i, 0))
    return pl.pallas_call(
        kern, name=name, grid=(S // ts,),
        in_specs=[blk, row, blk],
        out_specs=[pl.BlockSpec((1, 128), lambda i: (0, 0)), blk, row],
        out_shape=[jax.ShapeDtypeStruct((1, 128), F32), jax.ShapeDtypeStruct((S, D), F32),
                   jax.ShapeDtypeStruct((1, D), F32)],
        compiler_params=pltpu.CompilerParams(dimension_semantics=("arbitrary",), vmem_limit_bytes=VMEM_LIMIT),
    )(x, gain, target)


def _regroup_columns(w, outer, inner, name):
    rows = w.shape[0]

    def kern(i_ref, o_ref):
        o_ref[...] = i_ref[...]

    return pl.pallas_call(
        kern, name=name, grid=(outer, inner),
        in_specs=[pl.BlockSpec((rows, HEAD_DIM), lambda a, b: (0, a * inner + b))],
        out_specs=pl.BlockSpec((rows, HEAD_DIM), lambda a, b: (0, b * outer + a)),
        out_shape=jax.ShapeDtypeStruct(w.shape, w.dtype),
        compiler_params=pltpu.CompilerParams(
            dimension_semantics=("parallel", "parallel"), vmem_limit_bytes=VMEM_LIMIT),
    )(w)


def _heads_major(w, name):
    return _regroup_columns(w, 4, w.shape[1] // (4 * HEAD_DIM), name)


def _heads_minor(w, name):
    return _regroup_columns(w, w.shape[1] // (4 * HEAD_DIM), 4, name)


def _part(ref, j, rows=slice(None)):
    return ref[rows, j * HEAD_DIM:(j + 1) * HEAD_DIM]


def _chunk_scan(v, ric, reverse):
    n = v.shape[0]
    d = 1
    while d < A_CHUNK:
        if reverse:
            v = v + jnp.where(ric < A_CHUNK - d, pltpu.roll(v, n - d, 0), 0.0)
        else:
            v = v + jnp.where(ric >= d, pltpu.roll(v, d, 0), 0.0)
        d *= 2
    return v


def _hgrn2_gates(q_raw, f_logit, lbv, ric):
    qs = _silu(q_raw)
    sig = _sigmoid(f_logit)
    f = lbv + (1.0 - lbv) * sig
    log_f = jnp.log(f)
    kk = (1.0 - lbv) * _sigmoid(-f_logit)
    cum = _chunk_scan(log_f, ric, False)
    cl = cum + _chunk_scan(log_f, ric, True) - log_f
    e_cum = jnp.exp(cum)
    e_neg = jnp.exp(-cum)
    e_end = jnp.exp(cl - cum)
    dec = jnp.exp(cl)
    return qs, sig, f, kk, e_cum, e_neg, e_end, dec


def _hgrn2_specs(S, D, TB, reverse):
    H = D // HEAD_DIM
    NB = S // TB
    pos = (lambda nb: NB - 1 - nb) if reverse else (lambda nb: nb)
    proj = pl.BlockSpec((TB, 4 * HEAD_DIM), lambda h, nb: (pos(nb), h))
    head = pl.BlockSpec((TB, HEAD_DIM), lambda h, nb: (pos(nb), h))
    vec = pl.BlockSpec((1, HEAD_DIM), lambda h, nb: (0, h))
    state = pl.BlockSpec((TB // A_CHUNK, None, HEAD_DIM, HEAD_DIM), lambda h, nb: (pos(nb), h, 0, 0))
    return H, NB, proj, head, vec, state


def _hgrn2_fwd(proj, lb, gain, name):
    S, D4 = proj.shape
    D = D4 // 4
    TB = _pick(S, 512, A_CHUNK)
    H, NB, pspec, head, vec, state = _hgrn2_specs(S, D, TB, False)
    NCB = TB // A_CHUNK

    def kern(p_ref, lb_ref, gain_ref, og_ref, oraw_ref, st_ref, a_s, b_s, k_s, v_s, dec_s, o_s, st_s):
        @pl.when(pl.program_id(1) == 0)
        def _():
            st_s[...] = jnp.zeros_like(st_s)

        ric = lax.broadcasted_iota(jnp.int32, (TB, HEAD_DIM), 0) % A_CHUNK
        qs, _, _, kk, e_cum, e_neg, e_end, dec = _hgrn2_gates(_part(p_ref, 0), _part(p_ref, 1), lb_ref[...], ric)
        a_s[...] = (qs * e_cum).astype(BF16)
        b_s[...] = (kk * e_neg).astype(BF16)
        k_s[...] = (kk * e_end).astype(BF16)
        v_s[...] = _part(p_ref, 2).astype(BF16)
        dec_s[...] = dec
        tril = (lax.broadcasted_iota(jnp.int32, (A_CHUNK, A_CHUNK), 0)
                >= lax.broadcasted_iota(jnp.int32, (A_CHUNK, A_CHUNK), 1))

        def chunk(ci, st):
            r = pl.multiple_of(ci * A_CHUNK, A_CHUNK)
            rows = pl.ds(r, A_CHUNK)
            a, b, k, v = a_s[rows, :], b_s[rows, :], k_s[rows, :], v_s[rows, :]
            st_bf = st.astype(BF16)
            st_ref[ci] = st_bf
            p = jnp.where(tril, _dot_nt(a, b), 0.0).astype(BF16)
            o_s[rows, :] = _dot_nn(p, v) + _dot_nt(a, st_bf)
            return dec_s[pl.ds(r, 1), :] * st + _dot_tn(v, k)

        st_s[...] = lax.fori_loop(0, NCB, chunk, st_s[...], unroll=HGRN2_UNROLL)
        o = o_s[...]
        oraw_ref[...] = o
        on = o * lax.rsqrt(jnp.mean(o * o, axis=-1, keepdims=True) + NORM_EPS)
        og_ref[...] = ((on * gain_ref[...]) * _silu(_part(p_ref, 3))).astype(og_ref.dtype)

    tb_bf = pltpu.VMEM((TB, HEAD_DIM), BF16)
    tb_f = pltpu.VMEM((TB, HEAD_DIM), F32)
    return pl.pallas_call(
        kern, name=name, grid=(H, NB),
        in_specs=[pspec, vec, vec],
        out_specs=[head, head, state],
        out_shape=[jax.ShapeDtypeStruct((S, D), BF16), jax.ShapeDtypeStruct((S, D), F32),
                   jax.ShapeDtypeStruct((S // A_CHUNK, H, HEAD_DIM, HEAD_DIM), BF16)],
        scratch_shapes=[tb_bf, tb_bf, tb_bf, tb_bf, tb_f, tb_f, pltpu.VMEM((HEAD_DIM, HEAD_DIM), F32)],
        compiler_params=pltpu.CompilerParams(
            dimension_semantics=("parallel", "arbitrary"), vmem_limit_bytes=VMEM_LIMIT),
    )(proj, lb, gain)


def _hgrn2_bwd(proj, lb, gain, oraw, states, dog, name):
    S, D4 = proj.shape
    D = D4 // 4
    TB = _pick(S, 512, A_CHUNK)
    H, NB, pspec, head, vec, state = _hgrn2_specs(S, D, TB, True)
    NCB = TB // A_CHUNK

    def kern(p_ref, lb_ref, gain_ref, oraw_ref, st_ref, dog_ref, dp_ref, dlb_ref, dgain_ref,
             a_s, b_s, k_s, v_s, do_s, dec_s, da_s, db_s, dk_s, dv_s, ddec_s, dst_s):
        @pl.when(pl.program_id(1) == 0)
        def _():
            dst_s[...] = jnp.zeros_like(dst_s)
            dlb_ref[...] = jnp.zeros_like(dlb_ref)
            dgain_ref[...] = jnp.zeros_like(dgain_ref)

        ric = lax.broadcasted_iota(jnp.int32, (TB, HEAD_DIM), 0) % A_CHUNK
        lbv = lb_ref[...]
        q_raw = _part(p_ref, 0)
        qs, sig, f, kk, e_cum, e_neg, e_end, dec = _hgrn2_gates(q_raw, _part(p_ref, 1), lbv, ric)
        a32, b32, k32 = qs * e_cum, kk * e_neg, kk * e_end
        a_s[...] = a32.astype(BF16)
        b_s[...] = b32.astype(BF16)
        k_s[...] = k32.astype(BF16)
        v_s[...] = _part(p_ref, 2).astype(BF16)
        dec_s[...] = dec

        o = oraw_ref[...]
        gain_v = gain_ref[...]
        g_raw = _part(p_ref, 3)
        rinv = lax.rsqrt(jnp.mean(o * o, axis=-1, keepdims=True) + NORM_EPS)
        on = o * rinv
        dog_v = dog_ref[...].astype(F32)
        dp_ref[:, 3 * HEAD_DIM:4 * HEAD_DIM] = (dog_v * (on * gain_v) * _dsilu(g_raw)).astype(dp_ref.dtype)
        dog2 = dog_v * _silu(g_raw)
        dgain_ref[...] += jnp.sum(dog2 * on, axis=0, keepdims=True)
        don = dog2 * gain_v
        do_s[...] = (rinv * (don - on * jnp.mean(don * on, axis=-1, keepdims=True))).astype(BF16)

        tril = (lax.broadcasted_iota(jnp.int32, (A_CHUNK, A_CHUNK), 0)
                >= lax.broadcasted_iota(jnp.int32, (A_CHUNK, A_CHUNK), 1))

        def chunk(it, dst):
            ci = NCB - 1 - it
            r = pl.multiple_of(ci * A_CHUNK, A_CHUNK)
            rows = pl.ds(r, A_CHUNK)
            a, b, k, v, do = a_s[rows, :], b_s[rows, :], k_s[rows, :], v_s[rows, :], do_s[rows, :]
            st_prev = st_ref[ci]
            dst_bf = dst.astype(BF16)
            p = jnp.where(tril, _dot_nt(a, b), 0.0).astype(BF16)
            dp = jnp.where(tril, _dot_nt(do, v), 0.0).astype(BF16)
            dv_s[rows, :] = _dot_tn(p, do) + _dot_nt(k, dst_bf)
            da_s[rows, :] = _dot_nn(dp, b) + _dot_nn(do, st_prev)
            db_s[rows, :] = _dot_tn(dp, a)
            dk_s[rows, :] = _dot_nn(v, dst_bf)
            ddec = jnp.sum(dst * st_prev.astype(F32), axis=0, keepdims=True)
            ddec_s[rows, :] = jnp.broadcast_to(ddec, (A_CHUNK, HEAD_DIM))
            return dec_s[pl.ds(r, 1), :] * dst + _dot_tn(do, a)

        dst_s[...] = lax.fori_loop(0, NCB, chunk, dst_s[...], unroll=HGRN2_UNROLL)

        da, db, dk = da_s[...], db_s[...], dk_s[...]
        dqs = da * e_cum
        dkk = db * e_neg + dk * e_end
        w = dk * k32
        dlog_f = (_chunk_scan(da * a32 - db * b32, ric, True) + (_chunk_scan(w, ric, False) - w)
                  + ddec_s[...] * dec)
        dfg = dlog_f / f - dkk
        dlb_ref[...] += jnp.sum(dfg * (1.0 - sig), axis=0, keepdims=True)
        dp_ref[:, 0:HEAD_DIM] = (dqs * _dsilu(q_raw)).astype(dp_ref.dtype)
        dp_ref[:, HEAD_DIM:2 * HEAD_DIM] = (dfg * (1.0 - lbv) * sig * (1.0 - sig)).astype(dp_ref.dtype)
        dp_ref[:, 2 * HEAD_DIM:3 * HEAD_DIM] = dv_s[...].astype(dp_ref.dtype)

    tb_bf = pltpu.VMEM((TB, HEAD_DIM), BF16)
    tb_f = pltpu.VMEM((TB, HEAD_DIM), F32)
    vec_shape = jax.ShapeDtypeStruct((1, D), F32)
    return pl.pallas_call(
        kern, name=name, grid=(H, NB),
        in_specs=[pspec, vec, vec, head, state, head],
        out_specs=[pspec, vec, vec],
        out_shape=[jax.ShapeDtypeStruct((S, D4), BF16), vec_shape, vec_shape],
        scratch_shapes=[tb_bf, tb_bf, tb_bf, tb_bf, tb_bf, tb_f, tb_f, tb_f, tb_f, tb_f, tb_f,
                        pltpu.VMEM((HEAD_DIM, HEAD_DIM), F32)],
        compiler_params=pltpu.CompilerParams(
            dimension_semantics=("parallel", "arbitrary"), vmem_limit_bytes=VMEM_LIMIT),
    )(proj, lb, gain, oraw, states, dog)


def _split_sum(v, tri):
    hi = v.astype(BF16)
    lo = (v - hi.astype(F32)).astype(BF16)
    return _dot_nn(hi, tri) + _dot_nn(lo, tri)


def _sb_logits(qs, kj, scale):
    z = _dot_nt(qs, kj) * scale
    log_beta = jnp.minimum(z, 0.0) - jnp.log1p(jnp.exp(-jnp.abs(z)))
    return z, log_beta


def _stack(ref, G):
    return jnp.concatenate([ref[:, g * HEAD_DIM:(g + 1) * HEAD_DIM] for g in range(G)], axis=0)


def _unstack(v, G):
    return jnp.concatenate([v[g * SB_TILE:(g + 1) * SB_TILE, :] for g in range(G)], axis=1)


def _sb_specs(S, D, KVH):
    G = D // HEAD_DIM // KVH
    qblk = pl.BlockSpec((SB_TILE, G * HEAD_DIM), lambda h, qi: (qi, h))
    kblk = pl.BlockSpec((S, HEAD_DIM), lambda h, qi: (0, h))
    vblk = pl.BlockSpec((S, HEAD_DIM), lambda h, qi: (0, KVH + h))
    return G, qblk, kblk, vblk


def _sb_fwd(q, kv, name):
    S, D = q.shape
    KVH = kv.shape[1] // (2 * HEAD_DIM)
    G, qblk, kblk, vblk = _sb_specs(S, D, KVH)
    R = G * SB_TILE
    scale = HEAD_DIM ** -0.5

    def kern(q_ref, k_ref, v_ref, o_ref, tot_ref, acc_s, run_s):
        qi = pl.program_id(1)
        qs = _stack(q_ref, G)
        row = lax.broadcasted_iota(jnp.int32, (R, SB_TILE), 0) % SB_TILE
        colm = lax.broadcasted_iota(jnp.int32, (R, SB_TILE), 1)
        mask = colm < row
        ti = lax.broadcasted_iota(jnp.int32, (SB_TILE, SB_TILE), 0)
        tj = lax.broadcasted_iota(jnp.int32, (SB_TILE, SB_TILE), 1)
        after = (ti > tj).astype(BF16)
        acc_s[...] = jnp.zeros_like(acc_s)
        run_s[...] = jnp.zeros_like(run_s)

        def tile(j, masked):
            ks = pl.ds(pl.multiple_of(j * SB_TILE, SB_TILE), SB_TILE)
            kj, vj = k_ref[ks, :], v_ref[ks, :]
            z, log_beta = _sb_logits(qs, kj, scale)
            log_rest = log_beta - z
            if masked:
                log_rest = jnp.where(mask, log_rest, 0.0)
            between = _split_sum(log_rest, after) + run_s[...]
            w = jnp.exp(log_beta + between)
            if masked:
                w = jnp.where(mask, w, 0.0)
            acc_s[...] += _dot_nn(w.astype(BF16), vj)
            run_s[...] += jnp.sum(log_rest, axis=1, keepdims=True)

        tile(qi, True)

        def body(it, carry):
            tile(qi - 1 - it, False)
            return carry

        lax.fori_loop(0, qi, body, 0)
        o_ref[...] = _unstack(acc_s[...], G).astype(o_ref.dtype)
        tot_ref[...] = _unstack(run_s[:, 0:HEAD_DIM], G)

    return pl.pallas_call(
        kern, name=name, grid=(KVH, S // SB_TILE),
        in_specs=[qblk, kblk, vblk], out_specs=[qblk, qblk],
        out_shape=[jax.ShapeDtypeStruct((S, D), BF16), jax.ShapeDtypeStruct((S, D), F32)],
        scratch_shapes=[pltpu.VMEM((R, HEAD_DIM), F32), pltpu.VMEM((R, SB_TILE), F32)],
        compiler_params=pltpu.CompilerParams(
            dimension_semantics=("parallel", "arbitrary"), vmem_limit_bytes=VMEM_LIMIT),
    )(q, kv, kv)


def _sb_bwd(q, kv, tot, do, after, name):
    S, D = q.shape
    KVH = kv.shape[1] // (2 * HEAD_DIM)
    G, qblk, kblk, vblk = _sb_specs(S, D, KVH)
    R = G * SB_TILE
    scale = HEAD_DIM ** -0.5

    def kern(q_ref, k_ref, v_ref, tot_ref, do_ref, after_ref, dq_ref, dk_ref, dv_ref, dq_s, pre_s, esum_s):
        qi = pl.program_id(1)

        @pl.when(qi == 0)
        def _():
            dk_ref[...] = jnp.zeros_like(dk_ref)
            dv_ref[...] = jnp.zeros_like(dv_ref)

        qs = _stack(q_ref, G)
        dos = _stack(do_ref, G)
        total = jnp.concatenate([_stack(tot_ref, G)] * (SB_TILE // HEAD_DIM), axis=1)
        row = lax.broadcasted_iota(jnp.int32, (R, SB_TILE), 0) % SB_TILE
        colm = lax.broadcasted_iota(jnp.int32, (R, SB_TILE), 1)
        mask = colm < row
        ti = lax.broadcasted_iota(jnp.int32, (SB_TILE, SB_TILE), 0)
        tj = lax.broadcasted_iota(jnp.int32, (SB_TILE, SB_TILE), 1)
        upto = (ti <= tj).astype(BF16)
        before = (ti < tj).astype(BF16)
        dq_s[...] = jnp.zeros_like(dq_s)
        pre_s[...] = jnp.zeros_like(pre_s)
        esum_s[...] = jnp.zeros_like(esum_s)

        def tile(j, masked):
            ks = pl.ds(pl.multiple_of(j * SB_TILE, SB_TILE), SB_TILE)
            kj, vj = k_ref[ks, :], v_ref[ks, :]
            z, log_beta = _sb_logits(qs, kj, scale)
            log_rest = log_beta - z
            if masked:
                log_rest = jnp.where(mask, log_rest, 0.0)
            between = total - (pre_s[...] + _split_sum(log_rest, upto))
            w = jnp.exp(log_beta + between)
            if masked:
                w = jnp.where(mask, w, 0.0)
            e = _dot_nt(dos, vj) * w
            e_before = esum_s[...] + _dot_nn(e.astype(BF16), before)
            beta = jnp.exp(log_beta)
            dz = e * (1.0 - beta) - e_before * beta
            if masked:
                dz = jnp.where(mask, dz, 0.0)
            dzs = (dz * scale).astype(BF16)
            dq_s[...] += _dot_nn(dzs, kj)
            dk_ref[ks, :] += _dot_tn(dzs, qs)
            dv_ref[ks, :] += _dot_tn(w.astype(BF16), dos)
            pre_s[...] += jnp.sum(log_rest, axis=1, keepdims=True)
            esum_s[...] += jnp.sum(e, axis=1, keepdims=True)

        def body(j, carry):
            tile(j, False)
            return carry

        lax.fori_loop(0, qi, body, 0)
        tile(qi, True)
        dq_ref[...] = _unstack(dq_s[...], G).astype(dq_ref.dtype)

    kvout = pl.BlockSpec((S, HEAD_DIM), lambda h, qi: (0, h))
    wide = pltpu.VMEM((R, SB_TILE), F32)
    return pl.pallas_call(
        kern, name=name, grid=(KVH, S // SB_TILE),
        in_specs=[qblk, kblk, vblk, qblk, qblk, pl.BlockSpec(after.shape, lambda h, qi: (0, 0))],
        out_specs=[qblk, kvout, kvout],
        out_shape=[jax.ShapeDtypeStruct((S, D), BF16), jax.ShapeDtypeStruct((S, KVH * HEAD_DIM), F32),
                   jax.ShapeDtypeStruct((S, KVH * HEAD_DIM), F32)],
        scratch_shapes=[pltpu.VMEM((R, HEAD_DIM), F32), wide, wide],
        compiler_params=pltpu.CompilerParams(
            dimension_semantics=("parallel", "arbitrary"), vmem_limit_bytes=VMEM_LIMIT),
    )(q, kv, kv, tot, do, after)


def _adamw(parts, w, m, v, name, layer=None, filled=None):
    shape = w.shape
    L = 1 if layer is None else shape[0]
    l = 0 if layer is None else layer
    C = shape[-1]
    R = w.size // (C * L)
    P = parts.shape[0]
    parts3 = parts.reshape(P, R, C)
    w3, m3, v3 = w.reshape(L, R, C), m.reshape(L, R, C), v.reshape(L, R, C)
    tr = _pick(R, max(8, (1 << 18) // C), 16)

    def kern(p_ref, w_ref, m_ref, v_ref, *rest):
        g_ref, d_ref, nm_ref, nv_ref = rest[-4:]
        g = p_ref[0].astype(F32)
        for i in range(1, P):
            g = g + p_ref[i].astype(F32)
        nm = ADAM_B1 * m_ref[...] + (1.0 - ADAM_B1) * g
        nv = ADAM_B2 * v_ref[...] + (1.0 - ADAM_B2) * (g * g)
        m_hat = nm / (1.0 - ADAM_B1 ** ADAM_STEP)
        v_hat = nv / (1.0 - ADAM_B2 ** ADAM_STEP)
        g_ref[...] = g
        d_ref[...] = -ADAM_LR * (m_hat / (jnp.sqrt(v_hat) + ADAM_EPS) + ADAM_WD * w_ref[...])
        nm_ref[...] = nm
        nv_ref[...] = nv

    blk = pl.BlockSpec((None, tr, C), lambda i: (l, i, 0))
    out = jax.ShapeDtypeStruct((L, R, C), F32)
    extra = [] if filled is None else [f.reshape(L, R, C) for f in filled]
    res = pl.pallas_call(
        kern, name=name, grid=(R // tr,),
        in_specs=[pl.BlockSpec((P, tr, C), lambda i: (0, i, 0)), blk, blk, blk]
        + [pl.BlockSpec(memory_space=pl.ANY)] * len(extra),
        out_specs=[blk, blk, blk, blk], out_shape=[out, out, out, out],
        input_output_aliases={4 + j: j for j in range(len(extra))},
        compiler_params=pltpu.CompilerParams(dimension_semantics=("parallel",), vmem_limit_bytes=VMEM_LIMIT),
    )(parts3, w3, m3, v3, *extra)
    return tuple(r.reshape(shape) for r in res)


def _lower_bound(logits):
    return jnp.cumsum(jax.nn.softmax(logits.astype(F32), axis=0), axis=0)[0:1]


def _pad_rows(a, rows):
    return jnp.zeros((rows, a.shape[1]), a.dtype).at[:a.shape[0]].set(a)


def kernel(x, c, ada_w, ada_b, norm_mix, norm_mlp, a_w_in, a_lb_logits, a_out_gain, a_w_out, kv_ada_w, kv_ada_b, kv_norm, w_kv, b_w_q, b_w_out, mlp_w1, mlp_w2, final_norm, loss_target, m_ada_w, m_ada_b, m_norm_mix, m_norm_mlp, m_a_w_in, m_a_lb_logits, m_a_out_gain, m_a_w_out, m_kv_ada_w, m_kv_ada_b, m_kv_norm, m_w_kv, m_b_w_q, m_b_w_out, m_mlp_w1, m_mlp_w2, m_final_norm, v_ada_w, v_ada_b, v_norm_mix, v_norm_mlp, v_a_w_in, v_a_lb_logits, v_a_out_gain, v_a_w_out, v_kv_ada_w, v_kv_ada_b, v_kv_norm, v_w_kv, v_b_w_q, v_b_w_out, v_mlp_w1, v_mlp_w2, v_final_norm):
    S, D = x.shape[1], x.shape[2]
    assert x.shape[0] == 1 and ada_w.shape[0] == 2 and a_w_in.shape[0] == 1 and b_w_q.shape[0] == 1
    me = 4 * lax.axis_index("x") + 2 * lax.axis_index("y") + lax.axis_index("c")
    dl = D // N_DEV
    na = ada_w.shape[2]
    nk = kv_ada_w.shape[1]

    small = jnp.concatenate([c.reshape(1, D), a_lb_logits.reshape(1, 2 * dl), a_out_gain.reshape(1, dl)], axis=1)
    small_all = _all_gather(small, "gather_small")[:, 0, :]
    c_all = small_all[:, :D]
    lb_logits = small_all[:, D:D + 2 * dl].reshape(N_DEV, 2, dl).transpose(1, 0, 2).reshape(2, D)
    out_gain = small_all[:, D + 2 * dl:].reshape(1, D)

    c_act = jax.nn.silu(c_all)
    c_act_rows = _pad_rows(c_act.astype(BF16), 128)
    mod_cols = jnp.concatenate([
        _matmul(c_act_rows, ada_w[0].astype(BF16), "nn", (F32,), "ada0")[0][:N_DEV],
        _matmul(c_act_rows, ada_w[1].astype(BF16), "nn", (F32,), "ada1")[0][:N_DEV],
        _matmul(c_act_rows, kv_ada_w.astype(BF16), "nn", (F32,), "ada_kv")[0][:N_DEV]], axis=1)
    mod_all = _all_gather(mod_cols, "gather_mod")
    mod_mine = lax.dynamic_index_in_dim(mod_all, me, axis=1, keepdims=False)
    mod0 = mod_mine[:, :na].reshape(1, 6 * D) + ada_b[0:1]
    mod1 = mod_mine[:, na:2 * na].reshape(1, 6 * D) + ada_b[1:2]
    modk = mod_mine[:, 2 * na:].reshape(1, 2 * D) + kv_ada_b.reshape(1, 2 * D)
    sh1a, sc1a, g1a, sh2a, sc2a, g2a = jnp.split(mod0, 6, axis=1)
    sh1b, sc1b, g1b, sh2b, sc2b, g2b = jnp.split(mod1, 6, axis=1)
    kv_sh, kv_sc = jnp.split(modk, 2, axis=1)
    nmix0, nmix1, nmlp0, nmlp1 = norm_mix[0:1], norm_mix[1:2], norm_mlp[0:1], norm_mlp[1:2]
    kvn = kv_norm.reshape(1, D)
    lb, lb_vjp = jax.vjp(_lower_bound, lb_logits)

    w_names = ["a_w_in", "a_w_out", "mlp0_w1", "mlp0_w2", "w_kv", "b_w_q", "b_w_out", "mlp1_w1", "mlp1_w2"]
    w_shards = [a_w_in[0], a_w_out[0], mlp_w1[0], mlp_w2[0], w_kv, b_w_q[0], b_w_out[0], mlp_w1[1], mlp_w2[1]]
    w_axes = [1, 0, 1, 0, 0, 0, 0, 1, 0]
    w_shards = [s.astype(BF16) for s in w_shards]
    w_routes = [_gather_route(s, ax) for s, ax in zip(w_shards, w_axes)]
    w_handles, w_token = _copies_start(w_shards, w_routes, mod_all, "gather_weights_start")

    def weight(i, after):
        return _copies_wait(w_handles[i], w_routes[i], after, "gather_" + w_names[i] + "_wait")

    x0 = x[0]
    h1 = _norm_mod_fwd(x0, nmix0, sc1a + w_token[0:1, 0:1], sh1a, "a_in_norm")
    w_in = _heads_major(weight(0, h1), "a_w_in_by_head")
    (proj,) = _matmul(h1, w_in, "nn", (F32,), "a_in_mm")
    og, oraw, states = _hgrn2_fwd(proj, lb, out_gain, "hgrn2_fwd")
    w_aout = weight(1, og)
    x1, y1 = _matmul(og, w_aout, "nn", (F32, BF16), "a_out_mm", _residual_epilogue, (x0,), (g1a,))
    h2 = _norm_mod_fwd(x1, nmlp0, sc2a, sh2a, "mlp0_up_norm")
    w1a = weight(2, h2)
    u0, a0 = _matmul(h2, w1a, "nn", (BF16, BF16), "mlp0_up_mm", _relu2_epilogue)
    w2a = weight(3, u0)
    x2, y2 = _matmul(u0, w2a, "nn", (F32, BF16), "mlp0_down_mm", _residual_epilogue, (x1,), (g2a,))

    hk = _norm_mod_fwd(x2, kvn, kv_sc, kv_sh, "kv_norm")
    w_kvf = weight(4, hk)
    (kv,) = _matmul(hk, w_kvf, "nn", (BF16,), "kv_mm")
    h3 = _norm_mod_fwd(x2, nmix1, sc1b, sh1b, "b_q_norm")
    w_q = weight(5, h3)
    (q,) = _matmul(h3, w_q, "nn", (BF16,), "b_q_mm")
    o, tot = _sb_fwd(q, kv, "attn_fwd")
    w_bout = weight(6, o)
    x3, y3 = _matmul(o, w_bout, "nn", (F32, BF16), "b_out_mm", _residual_epilogue, (x2,), (g1b,))
    h4 = _norm_mod_fwd(x3, nmlp1, sc2b, sh2b, "mlp1_up_norm")
    w1b = weight(7, h4)
    u1, a1 = _matmul(h4, w1b, "nn", (BF16, BF16), "mlp1_up_mm", _relu2_epilogue)
    w2b = weight(8, u1)
    x4, y4 = _matmul(u1, w2b, "nn", (F32, BF16), "mlp1_down_mm", _residual_epilogue, (x3,), (g2b,))

    loss_vec, d4, d_final = _final_loss(x4, final_norm.reshape(1, D), loss_target[0], "final_loss")
    loss = lax.psum(loss_vec[0, 0], MESH_AXES)

    sent = {}

    def send(key, g, axis):
        route = _scatter_route(g, axis)
        (handle,), token = _copies_start([g], [route], g, "scatter_" + key + "_start")
        sent[key] = (handle, route)
        return token

    def behind(vec, token):
        return vec + token[0:1, 0:1]

    def mlp_bwd(tag, d, y, g, u, a, w2, w1, h, x_in, gain, sc):
        dy, dg = _gate_bwd(d, y, g, tag + "_down_dgate")
        (du,) = _matmul(dy, w2, "nt", (BF16,), tag + "_down_du")
        (dw2,) = _matmul(u, dy, "tn", (BF16,), tag + "_down_dw")
        token = send(tag + "_w2", dw2, 0)
        dz = _relu2_bwd(du, a, token, tag + "_up_dact")
        (dh,) = _matmul(dz, w1, "nt", (F32,), tag + "_up_dh")
        (dw1,) = _matmul(h, dz, "tn", (BF16,), tag + "_up_dw")
        token = send(tag + "_w1", dw1, 1)
        d_out, dgain, dsc, dsh = _norm_mod_bwd(x_in, gain, behind(sc, token), dh, d, tag + "_up_dnorm")
        return d_out, dg, dgain, dsc, dsh

    d3, dg2b, dnmlp1, dsc2b, dsh2b = mlp_bwd("mlp1", d4, y4, g2b, u1, a1, w2b, w1b, h4, x3, nmlp1, sc2b)

    dy3, dg1b = _gate_bwd(d3, y3, g1b, "b_out_dgate")
    (do,) = _matmul(dy3, w_bout, "nt", (BF16,), "b_out_du")
    (dw_bout,) = _matmul(o, dy3, "tn", (BF16,), "b_out_dw")
    token = send("b_w_out", dw_bout, 0)
    dq, dk, dv = _sb_bwd(q, kv, tot, do, token, "attn_bwd")
    dkv = jnp.concatenate([dk, dv], axis=1).astype(BF16)
    (dh3,) = _matmul(dq, w_q, "nt", (F32,), "b_q_dh")
    (dw_q,) = _matmul(h3, dq, "tn", (BF16,), "b_q_dw")
    token = send("b_w_q", dw_q, 0)
    d2, dnmix1, dsc1b, dsh1b = _norm_mod_bwd(x2, nmix1, behind(sc1b, token), dh3, d3, "b_q_dnorm")
    (dhk,) = _matmul(dkv, w_kvf, "nt", (F32,), "kv_dh")
    (dw_kv,) = _matmul(hk, dkv, "tn", (BF16,), "kv_dw")
    token = send("w_kv", dw_kv, 0)
    d2, dkvn, dkv_sc, dkv_sh = _norm_mod_bwd(x2, kvn, behind(kv_sc, token), dhk, d2, "kv_dnorm")

    d1, dg2a, dnmlp0, dsc2a, dsh2a = mlp_bwd("mlp0", d2, y2, g2a, u0, a0, w2a, w1a, h2, x1, nmlp0, sc2a)

    dy1, dg1a = _gate_bwd(d1, y1, g1a, "a_out_dgate")
    (dog,) = _matmul(dy1, w_aout, "nt", (BF16,), "a_out_du")
    (dw_aout,) = _matmul(og, dy1, "tn", (BF16,), "a_out_dw")
    token = send("a_w_out", dw_aout, 0)
    dproj, dlb, d_out_gain = _hgrn2_bwd(proj, behind(lb, token), out_gain, oraw, states, dog, "hgrn2_bwd")
    (dh1,) = _matmul(dproj, w_in, "nt", (F32,), "a_in_dh")
    (dw_in,) = _matmul(h1, dproj, "tn", (BF16,), "a_in_dw")
    token = send("a_w_in", _heads_minor(dw_in, "a_w_in_grad_by_part"), 1)
    d0, dnmix0, dsc1a, dsh1a = _norm_mod_bwd(x0, nmix0, behind(sc1a, token), dh1, d1, "a_in_dnorm")
    (d_lb_logits,) = lb_vjp(dlb)

    dmod0 = jnp.concatenate([dsh1a, dsc1a, dg1a, dsh2a, dsc2a, dg2a], axis=1)
    dmod1 = jnp.concatenate([dsh1b, dsc1b, dg1b, dsh2b, dsc2b, dg2b], axis=1)
    dmodk = jnp.concatenate([dkv_sh, dkv_sc], axis=1)
    pieces = [dmod0, dmod1, dmodk, dnmix0, dnmix1, dnmlp0, dnmlp1, dkvn, d_final,
              d_lb_logits.reshape(1, 2 * D), d_out_gain]
    widths = [p.shape[1] for p in pieces]
    offs = [sum(widths[:i]) for i in range(len(widths))]
    part_all = _all_gather(jnp.concatenate(pieces, axis=1), "gather_dsmall")[:, 0, :]
    take = lambda i: part_all[:, offs[i]:offs[i] + widths[i]]
    dmod0_all, dmod1_all, dmodk_all = take(0), take(1), take(2)

    outs = {}

    def update(key, parts, w, m, v, **kw):
        outs[key] = _adamw(parts, w, m, v, "adamw_" + key + ("_%d" % kw["layer"] if "layer" in kw else ""), **kw)

    c_act_cols = _pad_rows(c_act.astype(BF16), 128).T
    my_cols = lambda a, n: _pad_rows(lax.dynamic_slice_in_dim(a, me * n, n, axis=1).astype(BF16), 128)
    g_ada0 = _matmul(c_act_cols, my_cols(dmod0_all, na), "nn", (F32,), "dada0")[0]
    g_ada1 = _matmul(c_act_cols, my_cols(dmod1_all, na), "nn", (F32,), "dada1")[0]
    g_adak = _matmul(c_act_cols, my_cols(dmodk_all, nk), "nn", (F32,), "dada_kv")[0]
    update("ada_w", g_ada1[None], ada_w, m_ada_w, v_ada_w, layer=1)
    update("ada_w", g_ada0[None], ada_w, m_ada_w, v_ada_w, layer=0, filled=outs["ada_w"])
    update("kv_ada_w", g_adak[None], kv_ada_w, m_kv_ada_w, v_kv_ada_w)
    update("ada_b", jnp.stack([dmod0_all, dmod1_all], axis=1), ada_b, m_ada_b, v_ada_b)
    update("kv_ada_b", dmodk_all, kv_ada_b, m_kv_ada_b, v_kv_ada_b)
    update("norm_mix", jnp.stack([take(3), take(4)], axis=1), norm_mix, m_norm_mix, v_norm_mix)
    update("norm_mlp", jnp.stack([take(5), take(6)], axis=1), norm_mlp, m_norm_mlp, v_norm_mlp)
    update("kv_norm", take(7), kv_norm, m_kv_norm, v_kv_norm)
    update("final_norm", take(8), final_norm, m_final_norm, v_final_norm)
    d_lb_all = take(9).reshape(N_DEV, 2, D)
    update("a_lb_logits", lax.dynamic_slice_in_dim(d_lb_all, me * dl, dl, axis=2), a_lb_logits, m_a_lb_logits,
           v_a_lb_logits)
    update("a_out_gain", lax.dynamic_slice_in_dim(take(10), me * dl, dl, axis=1)[:, None, :], a_out_gain,
           m_a_out_gain, v_a_out_gain)

    def landed(key, after):
        handle, route = sent[key]
        return _copies_wait(handle, route, after, "scatter_" + key + "_wait")

    update("mlp_w2", landed("mlp1_w2", d0), mlp_w2, m_mlp_w2, v_mlp_w2, layer=1)
    update("mlp_w1", landed("mlp1_w1", d0), mlp_w1, m_mlp_w1, v_mlp_w1, layer=1)
    update("b_w_out", landed("b_w_out", d0), b_w_out, m_b_w_out, v_b_w_out)
    update("b_w_q", landed("b_w_q", d0), b_w_q, m_b_w_q, v_b_w_q)
    update("w_kv", landed("w_kv", d0), w_kv, m_w_kv, v_w_kv)
    update("mlp_w2", landed("mlp0_w2", d0), mlp_w2, m_mlp_w2, v_mlp_w2, layer=0, filled=outs["mlp_w2"])
    update("mlp_w1", landed("mlp0_w1", d0), mlp_w1, m_mlp_w1, v_mlp_w1, layer=0, filled=outs["mlp_w1"])
    update("a_w_out", landed("a_w_out", d0), a_w_out, m_a_w_out, v_a_w_out)
    update("a_w_in", landed("a_w_in", outs["mlp_w1"][0]), a_w_in, m_a_w_in, v_a_w_in)

    names = ["ada_w", "ada_b", "norm_mix", "norm_mlp", "a_w_in", "a_lb_logits", "a_out_gain", "a_w_out", "kv_ada_w",
             "kv_ada_b", "kv_norm", "w_kv", "b_w_q", "b_w_out", "mlp_w1", "mlp_w2", "final_norm"]
    result = [loss, d0[None]]
    for field in range(4):
        result += [outs[n][field] for n in names]
    return tuple(result)
```

```python
import jax
import jax.numpy as jnp
from jax import lax
from jax.experimental import pallas as pl
from jax.experimental.pallas import tpu as pltpu

F32 = jnp.float32
BF16 = jnp.bfloat16

N_DEV = 8
MESH_AXES = ("x", "y", "c")
HEAD_DIM = 128
A_CHUNK = 16
HGRN2_UNROLL = 4
HGRN2_GROUP = 128
SB_TILE = 256
NORM_EPS = 1e-6
ADAM_LR = 0.001
ADAM_B1 = 0.9
ADAM_B2 = 0.999
ADAM_EPS = 1e-08
ADAM_WD = 0.01
ADAM_STEP = 10
VMEM_LIMIT = 56 * 1024 * 1024
MATMUL_VMEM = 40 * 1024 * 1024


def _pick(dim, target, align):
    t = (min(dim, target) // align) * align
    while t >= align:
        if dim % t == 0:
            return t
        t -= align
    return dim


def _sigmoid(x):
    return 1.0 / (1.0 + jnp.exp(-x))


def _silu(x):
    return x * _sigmoid(x)


def _dsilu(x):
    s = _sigmoid(x)
    return s * (1.0 + x * (1.0 - s))


def _dot(a, b, dims):
    return lax.dot_general(a, b, (dims, ((), ())), preferred_element_type=F32)


def _dot_nn(a, b):
    return _dot(a, b, ((1,), (0,)))


def _dot_nt(a, b):
    return _dot(a, b, ((1,), (1,)))


def _dot_tn(a, b):
    return _dot(a, b, ((0,), (0,)))


def _mesh_pos():
    return lax.axis_index("x"), lax.axis_index("y"), lax.axis_index("c")


def _flip(v, d):
    return 1 - v if d else v


def _all_gather(x, name, axis=None):
    if axis is None:
        out_shape, place = (N_DEV,) + x.shape, lambda d: (d,)
    else:
        route = _gather_route(x, axis)
        out_shape, place = route.land_shape, route.dst_slice

    def body(x_ref, out_ref, send_sems, recv_sems, local_sem):
        x, y, c = _mesh_pos()
        me, sibling = (x, y, c), (x, y, 1 - c)
        chips = [(1 - x, y), (x, 1 - y), (1 - x, 1 - y)]

        def rows(px, py, pc):
            return out_ref.at[place(4 * px + 2 * py + pc)]

        def copy(k, block, to, src=None):
            return pltpu.make_async_remote_copy(
                src_ref=rows(*block) if src is None else src,
                dst_ref=rows(*block),
                send_sem=send_sems.at[k],
                recv_sem=recv_sems.at[k],
                device_id=to,
                device_id_type=pl.DeviceIdType.MESH,
            )

        mine = pltpu.make_async_copy(x_ref, rows(*me), local_sem)
        mine.start()
        first = [copy(0, me, sibling, src=x_ref)]
        first += [copy(1 + j, me, (*chip, c), src=x_ref) for j, chip in enumerate(chips)]
        for cp in first:
            cp.start()
        passed = [copy(4 + j, (*chip, c), sibling) for j, chip in enumerate(chips)]
        for j, chip in enumerate(chips):
            copy(1 + j, (*chip, c), me).wait_recv()
            passed[j].start()
        copy(0, sibling, me).wait_recv()
        for j, chip in enumerate(chips):
            copy(4 + j, (*chip, 1 - c), me).wait_recv()
        for cp in first + passed:
            cp.wait_send()
        mine.wait()

    return pl.pallas_call(
        body,
        name=name,
        out_shape=jax.ShapeDtypeStruct(out_shape, x.dtype),
        in_specs=[pl.BlockSpec(memory_space=pl.ANY)],
        out_specs=pl.BlockSpec(memory_space=pl.ANY),
        scratch_shapes=[
            pltpu.SemaphoreType.DMA((7,)),
            pltpu.SemaphoreType.DMA((7,)),
            pltpu.SemaphoreType.DMA(()),
        ],
    )(x)


_RELATIONS = [(dx, dy, dc) for dx in (0, 1) for dy in (0, 1) for dc in (0, 1) if (dx, dy, dc) != (0, 0, 0)]
_HBM = pl.BlockSpec(memory_space=pltpu.HBM)
_SEM = pl.BlockSpec(memory_space=pltpu.SEMAPHORE)
_EFFECT = pltpu.SideEffectType.DATAFLOW_SIDE_EFFECTING


def _at(ref, idx):
    return ref.at[idx] if idx else ref


def _block(axis, ndim, n):
    return lambda d: (slice(None),) * axis + (pl.ds(d * n, n),) + (slice(None),) * (ndim - axis - 1)


class _Route:
    def __init__(self, land_shape, src_slice, dst_slice):
        self.land_shape, self.src_slice, self.dst_slice = tuple(land_shape), src_slice, dst_slice


def _gather_route(shard, axis):
    n = shard.shape[axis]
    shape = shard.shape[:axis] + (N_DEV * n,) + shard.shape[axis + 1:]
    return _Route(shape, lambda p: (), _block(axis, shard.ndim, n))


def _scatter_route(g, axis):
    n = g.shape[axis] // N_DEV
    shape = (N_DEV,) + g.shape[:axis] + (n,) + g.shape[axis + 1:]
    return _Route(shape, _block(axis, g.ndim, n), lambda p: (p,))


def _peers():
    x, y, c = _mesh_pos()
    out = []
    for k, (dx, dy, dc) in enumerate(_RELATIONS):
        px, py, pc = _flip(x, dx), _flip(y, dy), _flip(c, dc)
        out.append((k, (px, py, pc), 4 * px + 2 * py + pc))
    return 4 * x + 2 * y + c, out


_LOCAL = len(_RELATIONS)


def _copies_start(srcs, routes, after, name):
    n = len(srcs)
    order = [] if after is None else [after]

    def body(*refs):
        src_refs, land_refs = refs[:n], refs[n:2 * n]
        outs = refs[2 * n + len(order):]
        send, recv, token = outs[:n], outs[n:2 * n], outs[-1]
        me, peers = _peers()
        for i, route in enumerate(routes):
            for k, peer, pid in peers:
                pltpu.make_async_remote_copy(
                    src_ref=_at(src_refs[i], route.src_slice(pid)), dst_ref=_at(land_refs[i], route.dst_slice(me)),
                    send_sem=send[i].at[k], recv_sem=recv[i].at[k],
                    device_id=peer, device_id_type=pl.DeviceIdType.MESH).start()
            pltpu.make_async_copy(_at(src_refs[i], route.src_slice(me)), _at(land_refs[i], route.dst_slice(me)),
                                  send[i].at[_LOCAL]).start()
        token[...] = jnp.zeros_like(token)

    lands = [lax.empty(r.land_shape, s.dtype) for s, r in zip(srcs, routes)]
    hbm = lambda a: pltpu.with_memory_space_constraint(a, pltpu.HBM)
    res = pl.pallas_call(
        body, name=name,
        out_shape=([pltpu.SemaphoreType.DMA((_LOCAL + 1,))] * n + [pltpu.SemaphoreType.DMA((_LOCAL,))] * n
                   + [pltpu.HBM(s.shape, s.dtype) for s in srcs]
                   + [pltpu.HBM(l.shape, l.dtype) for l in lands]
                   + [jax.ShapeDtypeStruct((8, 128), F32)]),
        in_specs=[_HBM] * (2 * n) + [pl.BlockSpec(memory_space=pl.ANY)] * len(order),
        out_specs=[_SEM] * (2 * n) + [_HBM] * (2 * n) + [pl.BlockSpec(memory_space=pltpu.VMEM)],
        input_output_aliases={i: 2 * n + i for i in range(2 * n)},
        compiler_params=pltpu.CompilerParams(has_side_effects=_EFFECT),
    )(*[hbm(s) for s in srcs], *[hbm(l) for l in lands], *order)
    handles = [(res[i], res[n + i], res[2 * n + i], res[3 * n + i]) for i in range(n)]
    return handles, res[-1]


def _copies_wait(handle, route, after, name):
    send_sems, recv_sems, src, land = handle

    def body(src_ref, land_ref, send_ref, recv_ref, after_ref, src_out, got_ref):
        me, peers = _peers()
        for k, peer, pid in peers:
            cp = pltpu.make_async_remote_copy(
                src_ref=_at(src_ref, route.src_slice(pid)), dst_ref=_at(land_ref, route.dst_slice(pid)),
                send_sem=send_ref.at[k], recv_sem=recv_ref.at[k],
                device_id=peer, device_id_type=pl.DeviceIdType.MESH)
            cp.wait_send()
            cp.wait_recv()
        pltpu.make_async_copy(_at(src_ref, route.src_slice(me)), _at(land_ref, route.dst_slice(me)),
                              send_ref.at[_LOCAL]).wait()

    return pl.pallas_call(
        body, name=name,
        out_shape=(pltpu.HBM(src.shape, src.dtype), pltpu.HBM(land.shape, land.dtype)),
        in_specs=[_HBM, _HBM, _SEM, _SEM, pl.BlockSpec(memory_space=pl.ANY)],
        out_specs=(_HBM, _HBM),
        input_output_aliases={0: 0, 1: 1},
        compiler_params=pltpu.CompilerParams(has_side_effects=_EFFECT),
    )(src, land, send_sems, recv_sems, after)[1]


def _matmul_tiles(M, N, K, a_item, b_item, mn_bytes):
    tm, tk = _pick(M, 512, 128), _pick(K, 2048, 128)
    tn = _pick(N, 2048, 128)
    while True:
        cast = (tm * tk * 2 if a_item != 2 else 0) + (tk * tn * 2 if b_item != 2 else 0)
        need = 2 * (tm * tk * a_item + tk * tn * b_item + tm * tn * mn_bytes) + 2 * tm * tn * 4 + cast
        smaller = _pick(N, tn - 128, 128) if tn > 128 else tn
        if need <= MATMUL_VMEM or smaller >= tn:
            return tm, tn, tk
        tn = smaller


def _matmul(a, b, mode, out_dtypes, name, epilogue=None, tiles=(), rows=()):
    if mode == "nn":
        (M, K), (K2, N) = a.shape, b.shape
    elif mode == "nt":
        (M, K), (N, K2) = a.shape, b.shape
    else:
        (K, M), (K2, N) = a.shape, b.shape
    assert K == K2, (a.shape, b.shape, mode)
    mn_bytes = sum(t.dtype.itemsize for t in tiles) + sum(jnp.dtype(d).itemsize for d in out_dtypes)
    tm, tn, tk = _matmul_tiles(M, N, K, a.dtype.itemsize, b.dtype.itemsize, mn_bytes)
    nm, nn, nk = M // tm, N // tn, K // tk
    n_extra = len(tiles) + len(rows)
    n_out = len(out_dtypes)
    if epilogue is None:
        epilogue = lambda acc: (acc,)
    dims = {"nn": ((1,), (0,)), "nt": ((1,), (1,)), "tn": ((0,), (0,))}[mode]

    def finish(acc, extra, outs):
        vals = epilogue(acc, *[r[...] for r in extra])
        for o_ref, v in zip(outs, vals):
            o_ref[...] = v.astype(o_ref.dtype)

    def kern(*refs):
        a_ref, b_ref = refs[0], refs[1]
        extra = refs[2:2 + n_extra]
        outs = refs[2 + n_extra:2 + n_extra + n_out]
        part = _dot(a_ref[...].astype(BF16), b_ref[...].astype(BF16), dims)
        if nk == 1:
            finish(part, extra, outs)
            return
        acc_ref = refs[-1]
        k = pl.program_id(2)

        @pl.when(k == 0)
        def _():
            acc_ref[...] = part

        @pl.when(k > 0)
        def _():
            acc_ref[...] += part

        @pl.when(k == nk - 1)
        def _():
            finish(acc_ref[...], extra, outs)

    a_bytes, b_bytes = a.size * a.dtype.itemsize, b.size * b.dtype.itemsize
    m_outer = a_bytes + nm * b_bytes <= nn * a_bytes + b_bytes
    ij = (lambda g0, g1: (g0, g1)) if m_outer else (lambda g0, g1: (g1, g0))

    def spec(shape, fn):
        return pl.BlockSpec(shape, lambda g0, g1, k: fn(*ij(g0, g1), k))

    a_spec = {"nn": spec((tm, tk), lambda i, j, k: (i, k)),
              "nt": spec((tm, tk), lambda i, j, k: (i, k)),
              "tn": spec((tk, tm), lambda i, j, k: (k, i))}[mode]
    b_spec = {"nn": spec((tk, tn), lambda i, j, k: (k, j)),
              "nt": spec((tn, tk), lambda i, j, k: (j, k)),
              "tn": spec((tk, tn), lambda i, j, k: (k, j))}[mode]
    tile_spec = spec((tm, tn), lambda i, j, k: (i, j))
    row_spec = spec((1, tn), lambda i, j, k: (0, j))
    return pl.pallas_call(
        kern,
        name=name,
        grid=(nm, nn, nk) if m_outer else (nn, nm, nk),
        in_specs=[a_spec, b_spec] + [tile_spec] * len(tiles) + [row_spec] * len(rows),
        out_specs=[tile_spec] * n_out,
        out_shape=[jax.ShapeDtypeStruct((M, N), dt) for dt in out_dtypes],
        scratch_shapes=[pltpu.VMEM((tm, tn), F32)] if nk > 1 else [],
        compiler_params=pltpu.CompilerParams(
            dimension_semantics=("parallel", "parallel", "arbitrary"), vmem_limit_bytes=VMEM_LIMIT),
    )(a, b, *tiles, *rows)


def _relu2_epilogue(acc):
    a = jnp.maximum(acc, 0.0)
    return a * a, a


def _residual_epilogue(acc, x_tile, gate_row):
    return x_tile + gate_row * acc, acc


def _norm_mod_fwd(x, gain, sc, sh, name):
    S, D = x.shape
    ts = _pick(S, 256, 16)

    def kern(x_ref, g_ref, sc_ref, sh_ref, h_ref):
        xv = x_ref[...]
        inv = lax.rsqrt(jnp.mean(xv * xv, axis=-1, keepdims=True) + NORM_EPS)
        h = (xv * inv) * g_ref[...] * (1.0 + sc_ref[...]) + sh_ref[...]
        h_ref[...] = h.astype(h_ref.dtype)

    row = pl.BlockSpec((1, D), lambda i: (0, 0))
    blk = pl.BlockSpec((ts, D), lambda i: (i, 0))
    return pl.pallas_call(
        kern, name=name, grid=(S // ts,),
        in_specs=[blk, row, row, row], out_specs=blk,
        out_shape=jax.ShapeDtypeStruct((S, D), BF16),
        compiler_params=pltpu.CompilerParams(dimension_semantics=("parallel",), vmem_limit_bytes=VMEM_LIMIT),
    )(x, gain, sc, sh)


def _norm_mod_bwd(x, gain, sc, dh, d_in, name):
    S, D = x.shape
    ts = _pick(S, 256, 8)

    def kern(x_ref, g_ref, sc_ref, dh_ref, din_ref, dx_ref, dg_ref, dsc_ref, dsh_ref):
        @pl.when(pl.program_id(0) == 0)
        def _():
            dg_ref[...] = jnp.zeros_like(dg_ref)
            dsc_ref[...] = jnp.zeros_like(dsc_ref)
            dsh_ref[...] = jnp.zeros_like(dsh_ref)

        xv = x_ref[...]
        dh = dh_ref[...].astype(F32)
        inv = lax.rsqrt(jnp.mean(xv * xv, axis=-1, keepdims=True) + NORM_EPS)
        xn = xv * inv
        g = g_ref[...]
        dsh_ref[...] += jnp.sum(dh, axis=0, keepdims=True)
        dsc_ref[...] += jnp.sum(dh * (xn * g), axis=0, keepdims=True)
        dn = dh * (1.0 + sc_ref[...])
        dg_ref[...] += jnp.sum(dn * xn, axis=0, keepdims=True)
        dxn = dn * g
        dx_ref[...] = din_ref[...] + inv * (dxn - xn * jnp.mean(dxn * xn, axis=-1, keepdims=True))

    row = pl.BlockSpec((1, D), lambda i: (0, 0))
    blk = pl.BlockSpec((ts, D), lambda i: (i, 0))
    vec = jax.ShapeDtypeStruct((1, D), F32)
    return pl.pallas_call(
        kern, name=name, grid=(S // ts,),
        in_specs=[blk, row, row, blk, blk], out_specs=[blk, row, row, row],
        out_shape=[jax.ShapeDtypeStruct((S, D), F32), vec, vec, vec],
        compiler_params=pltpu.CompilerParams(dimension_semantics=("arbitrary",), vmem_limit_bytes=VMEM_LIMIT),
    )(x, gain, sc, dh, d_in)


def _gate_bwd(d, y, g, name):
    S, D = d.shape
    ts = _pick(S, 256, 16)

    def kern(d_ref, y_ref, g_ref, dy_ref, dg_ref):
        @pl.when(pl.program_id(0) == 0)
        def _():
            dg_ref[...] = jnp.zeros_like(dg_ref)

        dv = d_ref[...]
        dy_ref[...] = (g_ref[...] * dv).astype(dy_ref.dtype)
        dg_ref[...] += jnp.sum(dv * y_ref[...].astype(F32), axis=0, keepdims=True)

    row = pl.BlockSpec((1, D), lambda i: (0, 0))
    blk = pl.BlockSpec((ts, D), lambda i: (i, 0))
    return pl.pallas_call(
        kern, name=name, grid=(S // ts,), in_specs=[blk, blk, row], out_specs=[blk, row],
        out_shape=[jax.ShapeDtypeStruct((S, D), BF16), jax.ShapeDtypeStruct((1, D), F32)],
        compiler_params=pltpu.CompilerParams(dimension_semantics=("arbitrary",), vmem_limit_bytes=VMEM_LIMIT),
    )(d, y, g)


def _final_loss(x, gain, target, name):
    S, D = x.shape
    ts = _pick(S, 256, 8)

    def kern(x_ref, g_ref, t_ref, loss_ref, dx_ref, dg_ref):
        @pl.when(pl.program_id(0) == 0)
        def _():
            loss_ref[...] = jnp.zeros_like(loss_ref)
            dg_ref[...] = jnp.zeros_like(dg_ref)

        xv = x_ref[...]
        g = g_ref[...]
        inv = lax.rsqrt(jnp.mean(xv * xv, axis=-1, keepdims=True) + NORM_EPS)
        xn = xv * inv
        err = xn * g - t_ref[...]
        row_loss = jnp.mean(err * err, axis=-1, keepdims=True)
        loss_ref[...] += 0.5 * jnp.sum(row_loss, axis=0, keepdims=True)
        dy = err * (1.0 / D)
        dg_ref[...] += jnp.sum(dy * xn, axis=0, keepdims=True)
        dxn = dy * g
        dx_ref[...] = inv * (dxn - xn * jnp.mean(dxn * xn, axis=-1, keepdims=True))

    row = pl.BlockSpec((1, D), lambda i: (0, 0))
    blk = pl.BlockSpec((ts, D), lambda i: (i, 0))
    return pl.pallas_call(
        kern, name=name, grid=(S // ts,),
        in_specs=[blk, row, blk],
        out_specs=[pl.BlockSpec((1, 128), lambda i: (0, 0)), blk, row],
        out_shape=[jax.ShapeDtypeStruct((1, 128), F32), jax.ShapeDtypeStruct((S, D), F32),
                   jax.ShapeDtypeStruct((1, D), F32)],
        compiler_params=pltpu.CompilerParams(dimension_semantics=("arbitrary",), vmem_limit_bytes=VMEM_LIMIT),
    )(x, gain, target)


def _regroup_columns(w, outer, inner, name):
    rows = w.shape[0]

    def kern(i_ref, o_ref):
        o_ref[...] = i_ref[...]

    return pl.pallas_call(
        kern, name=name, grid=(outer, inner),
        in_specs=[pl.BlockSpec((rows, HEAD_DIM), lambda a, b: (0, a * inner + b))],
        out_specs=pl.BlockSpec((rows, HEAD_DIM), lambda a, b: (0, b * outer + a)),
        out_shape=jax.ShapeDtypeStruct(w.shape, w.dtype),
        compiler_params=pltpu.CompilerParams(
            dimension_semantics=("parallel", "parallel"), vmem_limit_bytes=VMEM_LIMIT),
    )(w)


def _heads_major(w, name):
    return _regroup_columns(w, 4, w.shape[1] // (4 * HEAD_DIM), name)


def _heads_minor(w, name):
    return _regroup_columns(w, w.shape[1] // (4 * HEAD_DIM), 4, name)


def _part(ref, j, rows=slice(None)):
    return ref[rows, j * HEAD_DIM:(j + 1) * HEAD_DIM]


def _chunk_scan(v, ric, reverse):
    n = v.shape[0]
    d = 1
    while d < A_CHUNK:
        if reverse:
            v = v + jnp.where(ric < A_CHUNK - d, pltpu.roll(v, n - d, 0), 0.0)
        else:
            v = v + jnp.where(ric >= d, pltpu.roll(v, d, 0), 0.0)
        d *= 2
    return v


def _hgrn2_gates(q_raw, f_logit, lbv, ric):
    qs = _silu(q_raw)
    sig = _sigmoid(f_logit)
    f = lbv + (1.0 - lbv) * sig
    log_f = jnp.log(f)
    kk = (1.0 - lbv) * _sigmoid(-f_logit)
    cum = _chunk_scan(log_f, ric, False)
    cl = cum + _chunk_scan(log_f, ric, True) - log_f
    e_cum = jnp.exp(cum)
    e_neg = jnp.exp(-cum)
    e_end = jnp.exp(cl - cum)
    dec = jnp.exp(cl)
    return qs, sig, f, kk, e_cum, e_neg, e_end, dec


def _same_chunk_mask():
    r = lax.broadcasted_iota(jnp.int32, (HGRN2_GROUP, HGRN2_GROUP), 0)
    c = lax.broadcasted_iota(jnp.int32, (HGRN2_GROUP, HGRN2_GROUP), 1)
    return (r // A_CHUNK == c // A_CHUNK) & (r >= c)


def _hgrn2_specs(S, D, TB, reverse):
    H = D // HEAD_DIM
    NB = S // TB
    pos = (lambda nb: NB - 1 - nb) if reverse else (lambda nb: nb)
    proj = pl.BlockSpec((TB, 4 * HEAD_DIM), lambda h, nb: (pos(nb), h))
    head = pl.BlockSpec((TB, HEAD_DIM), lambda h, nb: (pos(nb), h))
    vec = pl.BlockSpec((1, HEAD_DIM), lambda h, nb: (0, h))
    state = pl.BlockSpec((TB // A_CHUNK, None, HEAD_DIM, HEAD_DIM), lambda h, nb: (pos(nb), h, 0, 0))
    return H, NB, proj, head, vec, state


def _hgrn2_fwd(proj, lb, gain, name):
    S, D4 = proj.shape
    D = D4 // 4
    TB = _pick(S, 512, A_CHUNK)
    H, NB, pspec, head, vec, state = _hgrn2_specs(S, D, TB, False)
    NCB = TB // A_CHUNK

    def kern(p_ref, lb_ref, gain_ref, og_ref, oraw_ref, st_ref, a_s, b_s, k_s, v_s, dec_s, o_s, st_s):
        @pl.when(pl.program_id(1) == 0)
        def _():
            st_s[...] = jnp.zeros_like(st_s)

        ric = lax.broadcasted_iota(jnp.int32, (TB, HEAD_DIM), 0) % A_CHUNK
        qs, _, _, kk, e_cum, e_neg, e_end, dec = _hgrn2_gates(_part(p_ref, 0), _part(p_ref, 1), lb_ref[...], ric)
        a_s[...] = (qs * e_cum).astype(BF16)
        b_s[...] = (kk * e_neg).astype(BF16)
        k_s[...] = (kk * e_end).astype(BF16)
        v_s[...] = _part(p_ref, 2).astype(BF16)
        dec_s[...] = dec

        same_chunk = _same_chunk_mask()
        for gi in range(TB // HGRN2_GROUP):
            rows = pl.ds(gi * HGRN2_GROUP, HGRN2_GROUP)
            p = jnp.where(same_chunk, _dot_nt(a_s[rows, :], b_s[rows, :]), 0.0).astype(BF16)
            o_s[rows, :] = _dot_nn(p, v_s[rows, :])

        def chunks(it, st):
            for u in range(HGRN2_UNROLL):
                ci = it * HGRN2_UNROLL + u
                r = pl.multiple_of(ci * A_CHUNK, A_CHUNK)
                rows = pl.ds(r, A_CHUNK)
                st_bf = st.astype(BF16)
                st_ref[ci] = st_bf
                o_s[rows, :] += _dot_nt(a_s[rows, :], st_bf)
                st = dec_s[pl.ds(r, 1), :] * st + _dot_tn(v_s[rows, :], k_s[rows, :])
            return st

        st_s[...] = lax.fori_loop(0, NCB // HGRN2_UNROLL, chunks, st_s[...])
        o = o_s[...]
        oraw_ref[...] = o
        on = o * lax.rsqrt(jnp.mean(o * o, axis=-1, keepdims=True) + NORM_EPS)
        og_ref[...] = ((on * gain_ref[...]) * _silu(_part(p_ref, 3))).astype(og_ref.dtype)

    tb_bf = pltpu.VMEM((TB, HEAD_DIM), BF16)
    tb_f = pltpu.VMEM((TB, HEAD_DIM), F32)
    return pl.pallas_call(
        kern, name=name, grid=(H, NB),
        in_specs=[pspec, vec, vec],
        out_specs=[head, head, state],
        out_shape=[jax.ShapeDtypeStruct((S, D), BF16), jax.ShapeDtypeStruct((S, D), F32),
                   jax.ShapeDtypeStruct((S // A_CHUNK, H, HEAD_DIM, HEAD_DIM), BF16)],
        scratch_shapes=[tb_bf, tb_bf, tb_bf, tb_bf, tb_f, tb_f, pltpu.VMEM((HEAD_DIM, HEAD_DIM), F32)],
        compiler_params=pltpu.CompilerParams(
            dimension_semantics=("parallel", "arbitrary"), vmem_limit_bytes=VMEM_LIMIT),
    )(proj, lb, gain)


def _hgrn2_bwd(proj, lb, gain, oraw, states, dog, name):
    S, D4 = proj.shape
    D = D4 // 4
    TB = _pick(S, 512, A_CHUNK)
    H, NB, pspec, head, vec, state = _hgrn2_specs(S, D, TB, True)
    NCB = TB // A_CHUNK

    def kern(p_ref, lb_ref, gain_ref, oraw_ref, st_ref, dog_ref, dp_ref, dlb_ref, dgain_ref,
             a_s, b_s, k_s, v_s, do_s, dec_s, da_s, db_s, dk_s, dv_s, ddec_s, dst_s):
        @pl.when(pl.program_id(1) == 0)
        def _():
            dst_s[...] = jnp.zeros_like(dst_s)
            dlb_ref[...] = jnp.zeros_like(dlb_ref)
            dgain_ref[...] = jnp.zeros_like(dgain_ref)

        ric = lax.broadcasted_iota(jnp.int32, (TB, HEAD_DIM), 0) % A_CHUNK
        lbv = lb_ref[...]
        q_raw = _part(p_ref, 0)
        qs, sig, f, kk, e_cum, e_neg, e_end, dec = _hgrn2_gates(q_raw, _part(p_ref, 1), lbv, ric)
        a32, b32, k32 = qs * e_cum, kk * e_neg, kk * e_end
        a_s[...] = a32.astype(BF16)
        b_s[...] = b32.astype(BF16)
        k_s[...] = k32.astype(BF16)
        v_s[...] = _part(p_ref, 2).astype(BF16)
        dec_s[...] = dec

        o = oraw_ref[...]
        gain_v = gain_ref[...]
        g_raw = _part(p_ref, 3)
        rinv = lax.rsqrt(jnp.mean(o * o, axis=-1, keepdims=True) + NORM_EPS)
        on = o * rinv
        dog_v = dog_ref[...].astype(F32)
        dp_ref[:, 3 * HEAD_DIM:4 * HEAD_DIM] = (dog_v * (on * gain_v) * _dsilu(g_raw)).astype(dp_ref.dtype)
        dog2 = dog_v * _silu(g_raw)
        dgain_ref[...] += jnp.sum(dog2 * on, axis=0, keepdims=True)
        don = dog2 * gain_v
        do_s[...] = (rinv * (don - on * jnp.mean(don * on, axis=-1, keepdims=True))).astype(BF16)

        same_chunk = _same_chunk_mask()
        for gi in range(TB // HGRN2_GROUP):
            rows = pl.ds(gi * HGRN2_GROUP, HGRN2_GROUP)
            a, b, do = a_s[rows, :], b_s[rows, :], do_s[rows, :]
            p = jnp.where(same_chunk, _dot_nt(a, b), 0.0).astype(BF16)
            dp = jnp.where(same_chunk, _dot_nt(do, v_s[rows, :]), 0.0).astype(BF16)
            dv_s[rows, :] = _dot_tn(p, do)
            da_s[rows, :] = _dot_nn(dp, b)
            db_s[rows, :] = _dot_tn(dp, a)

        def chunks(it, dst):
            for u in range(HGRN2_UNROLL):
                ci = NCB - 1 - (it * HGRN2_UNROLL + u)
                r = pl.multiple_of(ci * A_CHUNK, A_CHUNK)
                rows = pl.ds(r, A_CHUNK)
                do = do_s[rows, :]
                st_prev = st_ref[ci]
                dst_bf = dst.astype(BF16)
                dv_s[rows, :] += _dot_nt(k_s[rows, :], dst_bf)
                da_s[rows, :] += _dot_nn(do, st_prev)
                dk_s[rows, :] = _dot_nn(v_s[rows, :], dst_bf)
                ddec = jnp.sum(dst * st_prev.astype(F32), axis=0, keepdims=True)
                ddec_s[rows, :] = jnp.broadcast_to(ddec, (A_CHUNK, HEAD_DIM))
                dst = dec_s[pl.ds(r, 1), :] * dst + _dot_tn(do, a_s[rows, :])
            return dst

        dst_s[...] = lax.fori_loop(0, NCB // HGRN2_UNROLL, chunks, dst_s[...])

        da, db, dk = da_s[...], db_s[...], dk_s[...]
        dqs = da * e_cum
        dkk = db * e_neg + dk * e_end
        w = dk * k32
        dlog_f = (_chunk_scan(da * a32 - db * b32, ric, True) + (_chunk_scan(w, ric, False) - w)
                  + ddec_s[...] * dec)
        dfg = dlog_f / f - dkk
        dlb_ref[...] += jnp.sum(dfg * (1.0 - sig), axis=0, keepdims=True)
        dp_ref[:, 0:HEAD_DIM] = (dqs * _dsilu(q_raw)).astype(dp_ref.dtype)
        dp_ref[:, HEAD_DIM:2 * HEAD_DIM] = (dfg * (1.0 - lbv) * sig * (1.0 - sig)).astype(dp_ref.dtype)
        dp_ref[:, 2 * HEAD_DIM:3 * HEAD_DIM] = dv_s[...].astype(dp_ref.dtype)

    tb_bf = pltpu.VMEM((TB, HEAD_DIM), BF16)
    tb_f = pltpu.VMEM((TB, HEAD_DIM), F32)
    vec_shape = jax.ShapeDtypeStruct((1, D), F32)
    return pl.pallas_call(
        kern, name=name, grid=(H, NB),
        in_specs=[pspec, vec, vec, head, state, head],
        out_specs=[pspec, vec, vec],
        out_shape=[jax.ShapeDtypeStruct((S, D4), BF16), vec_shape, vec_shape],
        scratch_shapes=[tb_bf, tb_bf, tb_bf, tb_bf, tb_bf, tb_f, tb_f, tb_f, tb_f, tb_f, tb_f,
                        pltpu.VMEM((HEAD_DIM, HEAD_DIM), F32)],
        compiler_params=pltpu.CompilerParams(
            dimension_semantics=("parallel", "arbitrary"), vmem_limit_bytes=VMEM_LIMIT),
    )(proj, lb, gain, oraw, states, dog)


def _split_sum(v, tri):
    hi = v.astype(BF16)
    lo = (v - hi.astype(F32)).astype(BF16)
    return _dot_nn(hi, tri) + _dot_nn(lo, tri)


def _sb_logits(qs, kj, scale):
    z = _dot_nt(qs, kj) * scale
    log_beta = jnp.minimum(z, 0.0) - jnp.log1p(jnp.exp(-jnp.abs(z)))
    return z, log_beta


def _stack(ref, G):
    return jnp.concatenate([ref[:, g * HEAD_DIM:(g + 1) * HEAD_DIM] for g in range(G)], axis=0)


def _unstack(v, G):
    return jnp.concatenate([v[g * SB_TILE:(g + 1) * SB_TILE, :] for g in range(G)], axis=1)


def _sb_specs(S, D, KVH):
    G = D // HEAD_DIM // KVH
    qblk = pl.BlockSpec((SB_TILE, G * HEAD_DIM), lambda h, qi: (qi, h))
    kblk = pl.BlockSpec((S, HEAD_DIM), lambda h, qi: (0, h))
    vblk = pl.BlockSpec((S, HEAD_DIM), lambda h, qi: (0, KVH + h))
    return G, qblk, kblk, vblk


def _sb_fwd(q, kv, name):
    S, D = q.shape
    KVH = kv.shape[1] // (2 * HEAD_DIM)
    G, qblk, kblk, vblk = _sb_specs(S, D, KVH)
    R = G * SB_TILE
    scale = HEAD_DIM ** -0.5

    def kern(q_ref, k_ref, v_ref, o_ref, tot_ref, acc_s, run_s):
        qi = pl.program_id(1)
        qs = _stack(q_ref, G)
        row = lax.broadcasted_iota(jnp.int32, (R, SB_TILE), 0) % SB_TILE
        colm = lax.broadcasted_iota(jnp.int32, (R, SB_TILE), 1)
        mask = colm < row
        ti = lax.broadcasted_iota(jnp.int32, (SB_TILE, SB_TILE), 0)
        tj = lax.broadcasted_iota(jnp.int32, (SB_TILE, SB_TILE), 1)
        after = (ti > tj).astype(BF16)
        acc_s[...] = jnp.zeros_like(acc_s)
        run_s[...] = jnp.zeros_like(run_s)

        def tile(j, masked):
            ks = pl.ds(pl.multiple_of(j * SB_TILE, SB_TILE), SB_TILE)
            kj, vj = k_ref[ks, :], v_ref[ks, :]
            z, log_beta = _sb_logits(qs, kj, scale)
            log_rest = log_beta - z
            if masked:
                log_rest = jnp.where(mask, log_rest, 0.0)
            between = _split_sum(log_rest, after) + run_s[...]
            w = jnp.exp(log_beta + between)
            if masked:
                w = jnp.where(mask, w, 0.0)
            acc_s[...] += _dot_nn(w.astype(BF16), vj)
            run_s[...] += jnp.sum(log_rest, axis=1, keepdims=True)

        tile(qi, True)

        def body(it, carry):
            tile(qi - 1 - it, False)
            return carry

        lax.fori_loop(0, qi, body, 0)
        o_ref[...] = _unstack(acc_s[...], G).astype(o_ref.dtype)
        tot_ref[...] = _unstack(run_s[:, 0:HEAD_DIM], G)

    return pl.pallas_call(
        kern, name=name, grid=(KVH, S // SB_TILE),
        in_specs=[qblk, kblk, vblk], out_specs=[qblk, qblk],
        out_shape=[jax.ShapeDtypeStruct((S, D), BF16), jax.ShapeDtypeStruct((S, D), F32)],
        scratch_shapes=[pltpu.VMEM((R, HEAD_DIM), F32), pltpu.VMEM((R, SB_TILE), F32)],
        compiler_params=pltpu.CompilerParams(
            dimension_semantics=("parallel", "arbitrary"), vmem_limit_bytes=VMEM_LIMIT),
    )(q, kv, kv)


def _sb_bwd(q, kv, tot, do, after, name):
    S, D = q.shape
    KVH = kv.shape[1] // (2 * HEAD_DIM)
    G, qblk, kblk, vblk = _sb_specs(S, D, KVH)
    R = G * SB_TILE
    scale = HEAD_DIM ** -0.5

    def kern(q_ref, k_ref, v_ref, tot_ref, do_ref, after_ref, dq_ref, dk_ref, dv_ref, dq_s, pre_s, esum_s):
        qi = pl.program_id(1)

        @pl.when(qi == 0)
        def _():
            dk_ref[...] = jnp.zeros_like(dk_ref)
            dv_ref[...] = jnp.zeros_like(dv_ref)

        qs = _stack(q_ref, G)
        dos = _stack(do_ref, G)
        total = jnp.concatenate([_stack(tot_ref, G)] * (SB_TILE // HEAD_DIM), axis=1)
        row = lax.broadcasted_iota(jnp.int32, (R, SB_TILE), 0) % SB_TILE
        colm = lax.broadcasted_iota(jnp.int32, (R, SB_TILE), 1)
        mask = colm < row
        ti = lax.broadcasted_iota(jnp.int32, (SB_TILE, SB_TILE), 0)
        tj = lax.broadcasted_iota(jnp.int32, (SB_TILE, SB_TILE), 1)
        upto = (ti <= tj).astype(BF16)
        before = (ti < tj).astype(BF16)
        dq_s[...] = jnp.zeros_like(dq_s)
        pre_s[...] = jnp.zeros_like(pre_s)
        esum_s[...] = jnp.zeros_like(esum_s)

        def tile(j, masked):
            ks = pl.ds(pl.multiple_of(j * SB_TILE, SB_TILE), SB_TILE)
            kj, vj = k_ref[ks, :], v_ref[ks, :]
            z, log_beta = _sb_logits(qs, kj, scale)
            log_rest = log_beta - z
            if masked:
                log_rest = jnp.where(mask, log_rest, 0.0)
            between = total - (pre_s[...] + _split_sum(log_rest, upto))
            w = jnp.exp(log_beta + between)
            if masked:
                w = jnp.where(mask, w, 0.0)
            e = _dot_nt(dos, vj) * w
            e_before = esum_s[...] + _dot_nn(e.astype(BF16), before)
            beta = jnp.exp(log_beta)
            dz = e * (1.0 - beta) - e_before * beta
            if masked:
                dz = jnp.where(mask, dz, 0.0)
            dzs = (dz * scale).astype(BF16)
            dq_s[...] += _dot_nn(dzs, kj)
            dk_ref[ks, :] += _dot_tn(dzs, qs)
            dv_ref[ks, :] += _dot_tn(w.astype(BF16), dos)
            pre_s[...] += jnp.sum(log_rest, axis=1, keepdims=True)
            esum_s[...] += jnp.sum(e, axis=1, keepdims=True)

        def body(j, carry):
            tile(j, False)
            return carry

        lax.fori_loop(0, qi, body, 0)
        tile(qi, True)
        dq_ref[...] = _unstack(dq_s[...], G).astype(dq_ref.dtype)

    kvout = pl.BlockSpec((S, HEAD_DIM), lambda h, qi: (0, h))
    wide = pltpu.VMEM((R, SB_TILE), F32)
    return pl.pallas_call(
        kern, name=name, grid=(KVH, S // SB_TILE),
        in_specs=[qblk, kblk, vblk, qblk, qblk, pl.BlockSpec(after.shape, lambda h, qi: (0, 0))],
        out_specs=[qblk, kvout, kvout],
        out_shape=[jax.ShapeDtypeStruct((S, D), BF16), jax.ShapeDtypeStruct((S, KVH * HEAD_DIM), F32),
                   jax.ShapeDtypeStruct((S, KVH * HEAD_DIM), F32)],
        scratch_shapes=[pltpu.VMEM((R, HEAD_DIM), F32), wide, wide],
        compiler_params=pltpu.CompilerParams(
            dimension_semantics=("parallel", "arbitrary"), vmem_limit_bytes=VMEM_LIMIT),
    )(q, kv, kv, tot, do, after)


def _adamw(parts, w, m, v, name, layer=None, filled=None):
    shape = w.shape
    L = 1 if layer is None else shape[0]
    l = 0 if layer is None else layer
    C = shape[-1]
    R = w.size // (C * L)
    P = parts.shape[0]
    parts3 = parts.reshape(P, R, C)
    w3, m3, v3 = w.reshape(L, R, C), m.reshape(L, R, C), v.reshape(L, R, C)
    tr = _pick(R, max(8, (1 << 18) // C), 16)

    def kern(p_ref, w_ref, m_ref, v_ref, *rest):
        g_ref, d_ref, nm_ref, nv_ref = rest[-4:]
        g = p_ref[0].astype(F32)
        for i in range(1, P):
            g = g + p_ref[i].astype(F32)
        nm = ADAM_B1 * m_ref[...] + (1.0 - ADAM_B1) * g
        nv = ADAM_B2 * v_ref[...] + (1.0 - ADAM_B2) * (g * g)
        m_hat = nm / (1.0 - ADAM_B1 ** ADAM_STEP)
        v_hat = nv / (1.0 - ADAM_B2 ** ADAM_STEP)
        g_ref[...] = g
        d_ref[...] = -ADAM_LR * (m_hat / (jnp.sqrt(v_hat) + ADAM_EPS) + ADAM_WD * w_ref[...])
        nm_ref[...] = nm
        nv_ref[...] = nv

    blk = pl.BlockSpec((None, tr, C), lambda i: (l, i, 0))
    out = jax.ShapeDtypeStruct((L, R, C), F32)
    extra = [] if filled is None else [f.reshape(L, R, C) for f in filled]
    res = pl.pallas_call(
        kern, name=name, grid=(R // tr,),
        in_specs=[pl.BlockSpec((P, tr, C), lambda i: (0, i, 0)), blk, blk, blk]
        + [pl.BlockSpec(memory_space=pl.ANY)] * len(extra),
        out_specs=[blk, blk, blk, blk], out_shape=[out, out, out, out],
        input_output_aliases={4 + j: j for j in range(len(extra))},
        compiler_params=pltpu.CompilerParams(dimension_semantics=("parallel",), vmem_limit_bytes=VMEM_LIMIT),
    )(parts3, w3, m3, v3, *extra)
    return tuple(r.reshape(shape) for r in res)


def _lower_bound(logits):
    return jnp.cumsum(jax.nn.softmax(logits.astype(F32), axis=0), axis=0)[0:1]


def _pad_rows(a, rows):
    return jnp.zeros((rows, a.shape[1]), a.dtype).at[:a.shape[0]].set(a)


def kernel(x, c, ada_w, ada_b, norm_mix, norm_mlp, a_w_in, a_lb_logits, a_out_gain, a_w_out, kv_ada_w, kv_ada_b, kv_norm, w_kv, b_w_q, b_w_out, mlp_w1, mlp_w2, final_norm, loss_target, m_ada_w, m_ada_b, m_norm_mix, m_norm_mlp, m_a_w_in, m_a_lb_logits, m_a_out_gain, m_a_w_out, m_kv_ada_w, m_kv_ada_b, m_kv_norm, m_w_kv, m_b_w_q, m_b_w_out, m_mlp_w1, m_mlp_w2, m_final_norm, v_ada_w, v_ada_b, v_norm_mix, v_norm_mlp, v_a_w_in, v_a_lb_logits, v_a_out_gain, v_a_w_out, v_kv_ada_w, v_kv_ada_b, v_kv_norm, v_w_kv, v_b_w_q, v_b_w_out, v_mlp_w1, v_mlp_w2, v_final_norm):
    S, D = x.shape[1], x.shape[2]
    assert x.shape[0] == 1 and ada_w.shape[0] == 2 and a_w_in.shape[0] == 1 and b_w_q.shape[0] == 1
    me = 4 * lax.axis_index("x") + 2 * lax.axis_index("y") + lax.axis_index("c")
    dl = D // N_DEV
    na = ada_w.shape[2]
    nk = kv_ada_w.shape[1]

    small = jnp.concatenate([c.reshape(1, D), a_lb_logits.reshape(1, 2 * dl), a_out_gain.reshape(1, dl)], axis=1)
    small_all = _all_gather(small, "gather_small")[:, 0, :]
    c_all = small_all[:, :D]
    lb_logits = small_all[:, D:D + 2 * dl].reshape(N_DEV, 2, dl).transpose(1, 0, 2).reshape(2, D)
    out_gain = small_all[:, D + 2 * dl:].reshape(1, D)

    c_act = jax.nn.silu(c_all)
    c_act_rows = _pad_rows(c_act.astype(BF16), 128)
    mod_cols = jnp.concatenate([
        _matmul(c_act_rows, ada_w[0].astype(BF16), "nn", (F32,), "ada0")[0][:N_DEV],
        _matmul(c_act_rows, ada_w[1].astype(BF16), "nn", (F32,), "ada1")[0][:N_DEV],
        _matmul(c_act_rows, kv_ada_w.astype(BF16), "nn", (F32,), "ada_kv")[0][:N_DEV]], axis=1)
    mod_all = _all_gather(mod_cols, "gather_mod")
    mod_mine = lax.dynamic_index_in_dim(mod_all, me, axis=1, keepdims=False)
    mod0 = mod_mine[:, :na].reshape(1, 6 * D) + ada_b[0:1]
    mod1 = mod_mine[:, na:2 * na].reshape(1, 6 * D) + ada_b[1:2]
    modk = mod_mine[:, 2 * na:].reshape(1, 2 * D) + kv_ada_b.reshape(1, 2 * D)
    sh1a, sc1a, g1a, sh2a, sc2a, g2a = jnp.split(mod0, 6, axis=1)
    sh1b, sc1b, g1b, sh2b, sc2b, g2b = jnp.split(mod1, 6, axis=1)
    kv_sh, kv_sc = jnp.split(modk, 2, axis=1)
    nmix0, nmix1, nmlp0, nmlp1 = norm_mix[0:1], norm_mix[1:2], norm_mlp[0:1], norm_mlp[1:2]
    kvn = kv_norm.reshape(1, D)
    lb, lb_vjp = jax.vjp(_lower_bound, lb_logits)

    w_in = _all_gather(a_w_in[0].astype(BF16), "gather_a_w_in", axis=1)
    w_names = ["a_w_out", "mlp0_w1", "mlp0_w2", "w_kv", "b_w_q", "b_w_out", "mlp1_w1", "mlp1_w2"]
    w_shards = [a_w_out[0], mlp_w1[0], mlp_w2[0], w_kv, b_w_q[0], b_w_out[0], mlp_w1[1], mlp_w2[1]]
    w_axes = [0, 1, 0, 0, 0, 0, 1, 0]
    w_shards = [s.astype(BF16) for s in w_shards]
    w_routes = [_gather_route(s, ax) for s, ax in zip(w_shards, w_axes)]
    w_handles, w_token = _copies_start(w_shards, w_routes, w_in, "gather_weights_start")

    def weight(key, after):
        i = w_names.index(key)
        return _copies_wait(w_handles[i], w_routes[i], after, "gather_" + key + "_wait")

    x0 = x[0]
    h1 = _norm_mod_fwd(x0, nmix0, sc1a + w_token[0:1, 0:1], sh1a, "a_in_norm")
    w_in = _heads_major(w_in, "a_w_in_by_head")
    (proj,) = _matmul(h1, w_in, "nn", (F32,), "a_in_mm")
    og, oraw, states = _hgrn2_fwd(proj, lb, out_gain, "hgrn2_fwd")
    w_aout = weight("a_w_out", og)
    x1, y1 = _matmul(og, w_aout, "nn", (F32, BF16), "a_out_mm", _residual_epilogue, (x0,), (g1a,))
    h2 = _norm_mod_fwd(x1, nmlp0, sc2a, sh2a, "mlp0_up_norm")
    w1a = weight("mlp0_w1", h2)
    u0, a0 = _matmul(h2, w1a, "nn", (BF16, BF16), "mlp0_up_mm", _relu2_epilogue)
    w2a = weight("mlp0_w2", u0)
    x2, y2 = _matmul(u0, w2a, "nn", (F32, BF16), "mlp0_down_mm", _residual_epilogue, (x1,), (g2a,))

    hk = _norm_mod_fwd(x2, kvn, kv_sc, kv_sh, "kv_norm")
    w_kvf = weight("w_kv", hk)
    (kv,) = _matmul(hk, w_kvf, "nn", (BF16,), "kv_mm")
    h3 = _norm_mod_fwd(x2, nmix1, sc1b, sh1b, "b_q_norm")
    w_q = weight("b_w_q", h3)
    (q,) = _matmul(h3, w_q, "nn", (BF16,), "b_q_mm")
    o, tot = _sb_fwd(q, kv, "attn_fwd")
    w_bout = weight("b_w_out", o)
    x3, y3 = _matmul(o, w_bout, "nn", (F32, BF16), "b_out_mm", _residual_epilogue, (x2,), (g1b,))
    h4 = _norm_mod_fwd(x3, nmlp1, sc2b, sh2b, "mlp1_up_norm")
    w1b = weight("mlp1_w1", h4)
    u1, a1 = _matmul(h4, w1b, "nn", (BF16, BF16), "mlp1_up_mm", _relu2_epilogue)
    w2b = weight("mlp1_w2", u1)
    x4, y4 = _matmul(u1, w2b, "nn", (F32, BF16), "mlp1_down_mm", _residual_epilogue, (x3,), (g2b,))

    loss_vec, d4, d_final = _final_loss(x4, final_norm.reshape(1, D), loss_target[0], "final_loss")
    loss = lax.psum(loss_vec[0, 0], MESH_AXES)

    sent = {}

    def send(key, g, axis):
        route = _scatter_route(g, axis)
        (handle,), token = _copies_start([g], [route], None, "scatter_" + key + "_start")
        sent[key] = (handle, route)
        return token

    def behind(vec, *tokens):
        for token in tokens:
            vec = vec + token[0:1, 0:1]
        return vec

    def mlp_bwd(tag, d, y, g, u, a, w2, w1, h, x_in, gain, sc):
        dy, dg = _gate_bwd(d, y, g, tag + "_down_dgate")
        (dw2,) = _matmul(u, dy, "tn", (BF16,), tag + "_down_dw")
        token2 = send(tag + "_w2", dw2, 0)
        (dz,) = _matmul(dy, w2, "nt", (BF16,), tag + "_down_dz", lambda acc, a_tile: (acc * (2.0 * a_tile),), (a,))
        (dw1,) = _matmul(h, dz, "tn", (BF16,), tag + "_up_dw")
        token1 = send(tag + "_w1", dw1, 1)
        (dh,) = _matmul(dz, w1, "nt", (F32,), tag + "_up_dh")
        d_out, dgain, dsc, dsh = _norm_mod_bwd(x_in, gain, behind(sc, token2, token1), dh, d, tag + "_up_dnorm")
        return d_out, dg, dgain, dsc, dsh

    d3, dg2b, dnmlp1, dsc2b, dsh2b = mlp_bwd("mlp1", d4, y4, g2b, u1, a1, w2b, w1b, h4, x3, nmlp1, sc2b)

    dy3, dg1b = _gate_bwd(d3, y3, g1b, "b_out_dgate")
    (dw_bout,) = _matmul(o, dy3, "tn", (BF16,), "b_out_dw")
    token = send("b_w_out", dw_bout, 0)
    (do,) = _matmul(dy3, w_bout, "nt", (BF16,), "b_out_du")
    dq, dk, dv = _sb_bwd(q, kv, tot, do, token, "attn_bwd")
    dkv = jnp.concatenate([dk, dv], axis=1).astype(BF16)
    (dw_q,) = _matmul(h3, dq, "tn", (BF16,), "b_q_dw")
    token = send("b_w_q", dw_q, 0)
    (dh3,) = _matmul(dq, w_q, "nt", (F32,), "b_q_dh")
    d2, dnmix1, dsc1b, dsh1b = _norm_mod_bwd(x2, nmix1, behind(sc1b, token), dh3, d3, "b_q_dnorm")
    (dw_kv,) = _matmul(hk, dkv, "tn", (BF16,), "kv_dw")
    token = send("w_kv", dw_kv, 0)
    (dhk,) = _matmul(dkv, w_kvf, "nt", (F32,), "kv_dh")
    d2, dkvn, dkv_sc, dkv_sh = _norm_mod_bwd(x2, kvn, behind(kv_sc, token), dhk, d2, "kv_dnorm")

    d1, dg2a, dnmlp0, dsc2a, dsh2a = mlp_bwd("mlp0", d2, y2, g2a, u0, a0, w2a, w1a, h2, x1, nmlp0, sc2a)

    dy1, dg1a = _gate_bwd(d1, y1, g1a, "a_out_dgate")
    (dw_aout,) = _matmul(og, dy1, "tn", (BF16,), "a_out_dw")
    token = send("a_w_out", dw_aout, 0)
    (dog,) = _matmul(dy1, w_aout, "nt", (BF16,), "a_out_du")
    dproj, dlb, d_out_gain = _hgrn2_bwd(proj, behind(lb, token), out_gain, oraw, states, dog, "hgrn2_bwd")
    (dw_in,) = _matmul(h1, dproj, "tn", (BF16,), "a_in_dw")
    token = send("a_w_in", _heads_minor(dw_in, "a_w_in_grad_by_part"), 1)
    (dh1,) = _matmul(dproj, w_in, "nt", (F32,), "a_in_dh")
    d0, dnmix0, dsc1a, dsh1a = _norm_mod_bwd(x0, nmix0, behind(sc1a, token), dh1, d1, "a_in_dnorm")
    (d_lb_logits,) = lb_vjp(dlb)

    dmod0 = jnp.concatenate([dsh1a, dsc1a, dg1a, dsh2a, dsc2a, dg2a], axis=1)
    dmod1 = jnp.concatenate([dsh1b, dsc1b, dg1b, dsh2b, dsc2b, dg2b], axis=1)
    dmodk = jnp.concatenate([dkv_sh, dkv_sc], axis=1)
    pieces = [dmod0, dmod1, dmodk, dnmix0, dnmix1, dnmlp0, dnmlp1, dkvn, d_final,
              d_lb_logits.reshape(1, 2 * D), d_out_gain]
    widths = [p.shape[1] for p in pieces]
    offs = [sum(widths[:i]) for i in range(len(widths))]
    part_all = _all_gather(jnp.concatenate(pieces, axis=1), "gather_dsmall")[:, 0, :]
    take = lambda i: part_all[:, offs[i]:offs[i] + widths[i]]
    dmod0_all, dmod1_all, dmodk_all = take(0), take(1), take(2)

    outs = {}

    def update(key, parts, w, m, v, **kw):
        outs[key] = _adamw(parts, w, m, v, "adamw_" + key + ("_%d" % kw["layer"] if "layer" in kw else ""), **kw)

    c_act_cols = _pad_rows(c_act.astype(BF16), 128).T
    my_cols = lambda a, n: _pad_rows(lax.dynamic_slice_in_dim(a, me * n, n, axis=1).astype(BF16), 128)
    g_ada0 = _matmul(c_act_cols, my_cols(dmod0_all, na), "nn", (F32,), "dada0")[0]
    g_ada1 = _matmul(c_act_cols, my_cols(dmod1_all, na), "nn", (F32,), "dada1")[0]
    g_adak = _matmul(c_act_cols, my_cols(dmodk_all, nk), "nn", (F32,), "dada_kv")[0]
    update("ada_w", g_ada1[None], ada_w, m_ada_w, v_ada_w, layer=1)
    update("ada_w", g_ada0[None], ada_w, m_ada_w, v_ada_w, layer=0, filled=outs["ada_w"])
    update("kv_ada_w", g_adak[None], kv_ada_w, m_kv_ada_w, v_kv_ada_w)
    update("ada_b", jnp.stack([dmod0_all, dmod1_all], axis=1), ada_b, m_ada_b, v_ada_b)
    update("kv_ada_b", dmodk_all, kv_ada_b, m_kv_ada_b, v_kv_ada_b)
    update("norm_mix", jnp.stack([take(3), take(4)], axis=1), norm_mix, m_norm_mix, v_norm_mix)
    update("norm_mlp", jnp.stack([take(5), take(6)], axis=1), norm_mlp, m_norm_mlp, v_norm_mlp)
    update("kv_norm", take(7), kv_norm, m_kv_norm, v_kv_norm)
    update("final_norm", take(8), final_norm, m_final_norm, v_final_norm)
    d_lb_all = take(9).reshape(N_DEV, 2, D)
    update("a_lb_logits", lax.dynamic_slice_in_dim(d_lb_all, me * dl, dl, axis=2), a_lb_logits, m_a_lb_logits,
           v_a_lb_logits)
    update("a_out_gain", lax.dynamic_slice_in_dim(take(10), me * dl, dl, axis=1)[:, None, :], a_out_gain,
           m_a_out_gain, v_a_out_gain)

    def landed(key, after):
        handle, route = sent[key]
        return _copies_wait(handle, route, after, "scatter_" + key + "_wait")

    update("mlp_w2", landed("mlp1_w2", d0), mlp_w2, m_mlp_w2, v_mlp_w2, layer=1)
    update("mlp_w1", landed("mlp1_w1", d0), mlp_w1, m_mlp_w1, v_mlp_w1, layer=1)
    update("b_w_out", landed("b_w_out", d0), b_w_out, m_b_w_out, v_b_w_out)
    update("b_w_q", landed("b_w_q", d0), b_w_q, m_b_w_q, v_b_w_q)
    update("w_kv", landed("w_kv", d0), w_kv, m_w_kv, v_w_kv)
    update("mlp_w2", landed("mlp0_w2", d0), mlp_w2, m_mlp_w2, v_mlp_w2, layer=0, filled=outs["mlp_w2"])
    update("mlp_w1", landed("mlp0_w1", d0), mlp_w1, m_mlp_w1, v_mlp_w1, layer=0, filled=outs["mlp_w1"])
    update("a_w_out", landed("a_w_out", d0), a_w_out, m_a_w_out, v_a_w_out)
    update("a_w_in", landed("a_w_in", outs["mlp_w1"][0]), a_w_in, m_a_w_in, v_a_w_in)

    names = ["ada_w", "ada_b", "norm_mix", "norm_mlp", "a_w_in", "a_lb_logits", "a_out_gain", "a_w_out", "kv_ada_w",
             "kv_ada_b", "kv_norm", "w_kv", "b_w_q", "b_w_out", "mlp_w1", "mlp_w2", "final_norm"]
    result = [loss, d0[None]]
    for field in range(4):
        result += [outs[n][field] for n in names]
    return tuple(result)
```

```python
import jax
import jax.numpy as jnp
from jax import lax
from jax.experimental import pallas as pl
from jax.experimental.pallas import tpu as pltpu

F32 = jnp.float32
BF16 = jnp.bfloat16

N_DEV = 8
MESH_AXES = ("x", "y", "c")
HEAD_DIM = 128
A_CHUNK = 16
HGRN2_UNROLL = 8
HGRN2_GROUP = 128
SB_TILE = 256
NORM_EPS = 1e-6
ADAM_LR = 0.001
ADAM_B1 = 0.9
ADAM_B2 = 0.999
ADAM_EPS = 1e-08
ADAM_WD = 0.01
ADAM_STEP = 10
VMEM_LIMIT = 56 * 1024 * 1024
MATMUL_VMEM = 40 * 1024 * 1024


def _pick(dim, target, align):
    t = (min(dim, target) // align) * align
    while t >= align:
        if dim % t == 0:
            return t
        t -= align
    return dim


def _sigmoid(x):
    return 1.0 / (1.0 + jnp.exp(-x))


def _silu(x):
    return x * _sigmoid(x)


def _dsilu(x):
    s = _sigmoid(x)
    return s * (1.0 + x * (1.0 - s))


def _dot(a, b, dims):
    return lax.dot_general(a, b, (dims, ((), ())), preferred_element_type=F32)


def _dot_nn(a, b):
    return _dot(a, b, ((1,), (0,)))


def _dot_nt(a, b):
    return _dot(a, b, ((1,), (1,)))


def _dot_tn(a, b):
    return _dot(a, b, ((0,), (0,)))


def _mesh_pos():
    return lax.axis_index("x"), lax.axis_index("y"), lax.axis_index("c")


def _flip(v, d):
    return 1 - v if d else v


def _all_gather(x, name, axis=None):
    if axis is None:
        out_shape, place = (N_DEV,) + x.shape, lambda d: (d,)
    else:
        route = _gather_route(x, axis)
        out_shape, place = route.land_shape, route.dst_slice

    def body(x_ref, out_ref, send_sems, recv_sems, local_sem):
        x, y, c = _mesh_pos()
        me, sibling = (x, y, c), (x, y, 1 - c)
        chips = [(1 - x, y), (x, 1 - y), (1 - x, 1 - y)]

        def rows(px, py, pc):
            return out_ref.at[place(4 * px + 2 * py + pc)]

        def copy(k, block, to, src=None):
            return pltpu.make_async_remote_copy(
                src_ref=rows(*block) if src is None else src,
                dst_ref=rows(*block),
                send_sem=send_sems.at[k],
                recv_sem=recv_sems.at[k],
                device_id=to,
                device_id_type=pl.DeviceIdType.MESH,
            )

        mine = pltpu.make_async_copy(x_ref, rows(*me), local_sem)
        mine.start()
        first = [copy(0, me, sibling, src=x_ref)]
        first += [copy(1 + j, me, (*chip, c), src=x_ref) for j, chip in enumerate(chips)]
        for cp in first:
            cp.start()
        passed = [copy(4 + j, (*chip, c), sibling) for j, chip in enumerate(chips)]
        for j, chip in enumerate(chips):
            copy(1 + j, (*chip, c), me).wait_recv()
            passed[j].start()
        copy(0, sibling, me).wait_recv()
        for j, chip in enumerate(chips):
            copy(4 + j, (*chip, 1 - c), me).wait_recv()
        for cp in first + passed:
            cp.wait_send()
        mine.wait()

    return pl.pallas_call(
        body,
        name=name,
        out_shape=jax.ShapeDtypeStruct(out_shape, x.dtype),
        in_specs=[pl.BlockSpec(memory_space=pl.ANY)],
        out_specs=pl.BlockSpec(memory_space=pl.ANY),
        scratch_shapes=[
            pltpu.SemaphoreType.DMA((7,)),
            pltpu.SemaphoreType.DMA((7,)),
            pltpu.SemaphoreType.DMA(()),
        ],
    )(x)


_RELATIONS = [(dx, dy, dc) for dx in (0, 1) for dy in (0, 1) for dc in (0, 1) if (dx, dy, dc) != (0, 0, 0)]
_HBM = pl.BlockSpec(memory_space=pltpu.HBM)
_SEM = pl.BlockSpec(memory_space=pltpu.SEMAPHORE)
_EFFECT = pltpu.SideEffectType.DATAFLOW_SIDE_EFFECTING


def _at(ref, idx):
    return ref.at[idx] if idx else ref


def _block(axis, ndim, n):
    return lambda d: (slice(None),) * axis + (pl.ds(d * n, n),) + (slice(None),) * (ndim - axis - 1)


class _Route:
    def __init__(self, land_shape, src_slice, dst_slice):
        self.land_shape, self.src_slice, self.dst_slice = tuple(land_shape), src_slice, dst_slice


def _gather_route(shard, axis):
    n = shard.shape[axis]
    shape = shard.shape[:axis] + (N_DEV * n,) + shard.shape[axis + 1:]
    return _Route(shape, lambda p: (), _block(axis, shard.ndim, n))


def _scatter_route(g, axis):
    n = g.shape[axis] // N_DEV
    shape = (N_DEV,) + g.shape[:axis] + (n,) + g.shape[axis + 1:]
    return _Route(shape, _block(axis, g.ndim, n), lambda p: (p,))


def _peers():
    x, y, c = _mesh_pos()
    out = []
    for k, (dx, dy, dc) in enumerate(_RELATIONS):
        px, py, pc = _flip(x, dx), _flip(y, dy), _flip(c, dc)
        out.append((k, (px, py, pc), 4 * px + 2 * py + pc))
    return 4 * x + 2 * y + c, out


_LOCAL = len(_RELATIONS)


def _copies_start(srcs, routes, after, name):
    n = len(srcs)
    order = [] if after is None else [after]

    def body(*refs):
        src_refs, land_refs = refs[:n], refs[n:2 * n]
        outs = refs[2 * n + len(order):]
        send, recv, token = outs[:n], outs[n:2 * n], outs[-1]
        me, peers = _peers()
        for i, route in enumerate(routes):
            for k, peer, pid in peers:
                pltpu.make_async_remote_copy(
                    src_ref=_at(src_refs[i], route.src_slice(pid)), dst_ref=_at(land_refs[i], route.dst_slice(me)),
                    send_sem=send[i].at[k], recv_sem=recv[i].at[k],
                    device_id=peer, device_id_type=pl.DeviceIdType.MESH).start()
            pltpu.make_async_copy(_at(src_refs[i], route.src_slice(me)), _at(land_refs[i], route.dst_slice(me)),
                                  send[i].at[_LOCAL]).start()
        token[...] = jnp.zeros_like(token)

    lands = [lax.empty(r.land_shape, s.dtype) for s, r in zip(srcs, routes)]
    hbm = lambda a: pltpu.with_memory_space_constraint(a, pltpu.HBM)
    res = pl.pallas_call(
        body, name=name,
        out_shape=([pltpu.SemaphoreType.DMA((_LOCAL + 1,))] * n + [pltpu.SemaphoreType.DMA((_LOCAL,))] * n
                   + [pltpu.HBM(s.shape, s.dtype) for s in srcs]
                   + [pltpu.HBM(l.shape, l.dtype) for l in lands]
                   + [jax.ShapeDtypeStruct((8, 128), F32)]),
        in_specs=[_HBM] * (2 * n) + [pl.BlockSpec(memory_space=pl.ANY)] * len(order),
        out_specs=[_SEM] * (2 * n) + [_HBM] * (2 * n) + [pl.BlockSpec(memory_space=pltpu.VMEM)],
        input_output_aliases={i: 2 * n + i for i in range(2 * n)},
        compiler_params=pltpu.CompilerParams(has_side_effects=_EFFECT),
    )(*[hbm(s) for s in srcs], *[hbm(l) for l in lands], *order)
    handles = [(res[i], res[n + i], res[2 * n + i], res[3 * n + i]) for i in range(n)]
    return handles, res[-1]


def _copies_wait(handle, route, after, name):
    send_sems, recv_sems, src, land = handle

    def body(src_ref, land_ref, send_ref, recv_ref, after_ref, src_out, got_ref):
        me, peers = _peers()
        for k, peer, pid in peers:
            cp = pltpu.make_async_remote_copy(
                src_ref=_at(src_ref, route.src_slice(pid)), dst_ref=_at(land_ref, route.dst_slice(pid)),
                send_sem=send_ref.at[k], recv_sem=recv_ref.at[k],
                device_id=peer, device_id_type=pl.DeviceIdType.MESH)
            cp.wait_send()
            cp.wait_recv()
        pltpu.make_async_copy(_at(src_ref, route.src_slice(me)), _at(land_ref, route.dst_slice(me)),
                              send_ref.at[_LOCAL]).wait()

    return pl.pallas_call(
        body, name=name,
        out_shape=(pltpu.HBM(src.shape, src.dtype), pltpu.HBM(land.shape, land.dtype)),
        in_specs=[_HBM, _HBM, _SEM, _SEM, pl.BlockSpec(memory_space=pl.ANY)],
        out_specs=(_HBM, _HBM),
        input_output_aliases={0: 0, 1: 1},
        compiler_params=pltpu.CompilerParams(has_side_effects=_EFFECT),
    )(src, land, send_sems, recv_sems, after)[1]


def _matmul_tiles(M, N, K, a_item, b_item, mn_bytes):
    tm, tk = _pick(M, 512, 128), _pick(K, 2048, 128)
    tn = _pick(N, 2048, 128)
    while True:
        cast = (tm * tk * 2 if a_item != 2 else 0) + (tk * tn * 2 if b_item != 2 else 0)
        need = 2 * (tm * tk * a_item + tk * tn * b_item + tm * tn * mn_bytes) + 2 * tm * tn * 4 + cast
        smaller = _pick(N, tn - 128, 128) if tn > 128 else tn
        if need <= MATMUL_VMEM or smaller >= tn:
            return tm, tn, tk
        tn = smaller


def _matmul(a, b, mode, out_dtypes, name, epilogue=None, tiles=(), rows=()):
    if mode == "nn":
        (M, K), (K2, N) = a.shape, b.shape
    elif mode == "nt":
        (M, K), (N, K2) = a.shape, b.shape
    else:
        (K, M), (K2, N) = a.shape, b.shape
    assert K == K2, (a.shape, b.shape, mode)
    mn_bytes = sum(t.dtype.itemsize for t in tiles) + sum(jnp.dtype(d).itemsize for d in out_dtypes)
    tm, tn, tk = _matmul_tiles(M, N, K, a.dtype.itemsize, b.dtype.itemsize, mn_bytes)
    nm, nn, nk = M // tm, N // tn, K // tk
    n_extra = len(tiles) + len(rows)
    n_out = len(out_dtypes)
    if epilogue is None:
        epilogue = lambda acc: (acc,)
    dims = {"nn": ((1,), (0,)), "nt": ((1,), (1,)), "tn": ((0,), (0,))}[mode]

    def finish(acc, extra, outs):
        vals = epilogue(acc, *[r[...] for r in extra])
        for o_ref, v in zip(outs, vals):
            o_ref[...] = v.astype(o_ref.dtype)

    def kern(*refs):
        a_ref, b_ref = refs[0], refs[1]
        extra = refs[2:2 + n_extra]
        outs = refs[2 + n_extra:2 + n_extra + n_out]
        part = _dot(a_ref[...].astype(BF16), b_ref[...].astype(BF16), dims)
        if nk == 1:
            finish(part, extra, outs)
            return
        acc_ref = refs[-1]
        k = pl.program_id(2)

        @pl.when(k == 0)
        def _():
            acc_ref[...] = part

        @pl.when(k > 0)
        def _():
            acc_ref[...] += part

        @pl.when(k == nk - 1)
        def _():
            finish(acc_ref[...], extra, outs)

    a_bytes, b_bytes = a.size * a.dtype.itemsize, b.size * b.dtype.itemsize
    m_outer = a_bytes + nm * b_bytes <= nn * a_bytes + b_bytes
    ij = (lambda g0, g1: (g0, g1)) if m_outer else (lambda g0, g1: (g1, g0))

    def spec(shape, fn):
        return pl.BlockSpec(shape, lambda g0, g1, k: fn(*ij(g0, g1), k))

    a_spec = {"nn": spec((tm, tk), lambda i, j, k: (i, k)),
              "nt": spec((tm, tk), lambda i, j, k: (i, k)),
              "tn": spec((tk, tm), lambda i, j, k: (k, i))}[mode]
    b_spec = {"nn": spec((tk, tn), lambda i, j, k: (k, j)),
              "nt": spec((tn, tk), lambda i, j, k: (j, k)),
              "tn": spec((tk, tn), lambda i, j, k: (k, j))}[mode]
    tile_spec = spec((tm, tn), lambda i, j, k: (i, j))
    row_spec = spec((1, tn), lambda i, j, k: (0, j))
    return pl.pallas_call(
        kern,
        name=name,
        grid=(nm, nn, nk) if m_outer else (nn, nm, nk),
        in_specs=[a_spec, b_spec] + [tile_spec] * len(tiles) + [row_spec] * len(rows),
        out_specs=[tile_spec] * n_out,
        out_shape=[jax.ShapeDtypeStruct((M, N), dt) for dt in out_dtypes],
        scratch_shapes=[pltpu.VMEM((tm, tn), F32)] if nk > 1 else [],
        compiler_params=pltpu.CompilerParams(
            dimension_semantics=("parallel", "parallel", "arbitrary"), vmem_limit_bytes=VMEM_LIMIT),
    )(a, b, *tiles, *rows)


def _relu2_epilogue(acc):
    a = jnp.maximum(acc, 0.0)
    return a * a, a


def _residual_epilogue(acc, x_tile, gate_row):
    return x_tile + gate_row * acc, acc


def _norm_mod_fwd(x, gain, sc, sh, name):
    S, D = x.shape
    ts = _pick(S, 256, 16)

    def kern(x_ref, g_ref, sc_ref, sh_ref, h_ref):
        xv = x_ref[...]
        inv = lax.rsqrt(jnp.mean(xv * xv, axis=-1, keepdims=True) + NORM_EPS)
        h = (xv * inv) * g_ref[...] * (1.0 + sc_ref[...]) + sh_ref[...]
        h_ref[...] = h.astype(h_ref.dtype)

    row = pl.BlockSpec((1, D), lambda i: (0, 0))
    blk = pl.BlockSpec((ts, D), lambda i: (i, 0))
    return pl.pallas_call(
        kern, name=name, grid=(S // ts,),
        in_specs=[blk, row, row, row], out_specs=blk,
        out_shape=jax.ShapeDtypeStruct((S, D), BF16),
        compiler_params=pltpu.CompilerParams(dimension_semantics=("parallel",), vmem_limit_bytes=VMEM_LIMIT),
    )(x, gain, sc, sh)


def _norm_mod_bwd(x, gain, sc, dh, d_in, name):
    S, D = x.shape
    ts = _pick(S, 256, 8)

    def kern(x_ref, g_ref, sc_ref, dh_ref, din_ref, dx_ref, dg_ref, dsc_ref, dsh_ref):
        @pl.when(pl.program_id(0) == 0)
        def _():
            dg_ref[...] = jnp.zeros_like(dg_ref)
            dsc_ref[...] = jnp.zeros_like(dsc_ref)
            dsh_ref[...] = jnp.zeros_like(dsh_ref)

        xv = x_ref[...]
        dh = dh_ref[...].astype(F32)
        inv = lax.rsqrt(jnp.mean(xv * xv, axis=-1, keepdims=True) + NORM_EPS)
        xn = xv * inv
        g = g_ref[...]
        dsh_ref[...] += jnp.sum(dh, axis=0, keepdims=True)
        dsc_ref[...] += jnp.sum(dh * (xn * g), axis=0, keepdims=True)
        dn = dh * (1.0 + sc_ref[...])
        dg_ref[...] += jnp.sum(dn * xn, axis=0, keepdims=True)
        dxn = dn * g
        dx_ref[...] = din_ref[...] + inv * (dxn - xn * jnp.mean(dxn * xn, axis=-1, keepdims=True))

    row = pl.BlockSpec((1, D), lambda i: (0, 0))
    blk = pl.BlockSpec((ts, D), lambda i: (i, 0))
    vec = jax.ShapeDtypeStruct((1, D), F32)
    return pl.pallas_call(
        kern, name=name, grid=(S // ts,),
        in_specs=[blk, row, row, blk, blk], out_specs=[blk, row, row, row],
        out_shape=[jax.ShapeDtypeStruct((S, D), F32), vec, vec, vec],
        compiler_params=pltpu.CompilerParams(dimension_semantics=("arbitrary",), vmem_limit_bytes=VMEM_LIMIT),
    )(x, gain, sc, dh, d_in)


def _gate_bwd(d, y, g, name):
    S, D = d.shape
    ts = _pick(S, 256, 16)

    def kern(d_ref, y_ref, g_ref, dy_ref, dg_ref):
        @pl.when(pl.program_id(0) == 0)
        def _():
            dg_ref[...] = jnp.zeros_like(dg_ref)

        dv = d_ref[...]
        dy_ref[...] = (g_ref[...] * dv).astype(dy_ref.dtype)
        dg_ref[...] += jnp.sum(dv * y_ref[...].astype(F32), axis=0, keepdims=True)

    row = pl.BlockSpec((1, D), lambda i: (0, 0))
    blk = pl.BlockSpec((ts, D), lambda i: (i, 0))
    return pl.pallas_call(
        kern, name=name, grid=(S // ts,), in_specs=[blk, blk, row], out_specs=[blk, row],
        out_shape=[jax.ShapeDtypeStruct((S, D), BF16), jax.ShapeDtypeStruct((1, D), F32)],
        compiler_params=pltpu.CompilerParams(dimension_semantics=("arbitrary",), vmem_limit_bytes=VMEM_LIMIT),
    )(d, y, g)


def _final_loss(x, gain, target, name):
    S, D = x.shape
    ts = _pick(S, 256, 8)

    def kern(x_ref, g_ref, t_ref, loss_ref, dx_ref, dg_ref):
        @pl.when(pl.program_id(0) == 0)
        def _():
            loss_ref[...] = jnp.zeros_like(loss_ref)
            dg_ref[...] = jnp.zeros_like(dg_ref)

        xv = x_ref[...]
        g = g_ref[...]
        inv = lax.rsqrt(jnp.mean(xv * xv, axis=-1, keepdims=True) + NORM_EPS)
        xn = xv * inv
        err = xn * g - t_ref[...]
        row_loss = jnp.mean(err * err, axis=-1, keepdims=True)
        loss_ref[...] += 0.5 * jnp.sum(row_loss, axis=0, keepdims=True)
        dy = err * (1.0 / D)
        dg_ref[...] += jnp.sum(dy * xn, axis=0, keepdims=True)
        dxn = dy * g
        dx_ref[...] = inv * (dxn - xn * jnp.mean(dxn * xn, axis=-1, keepdims=True))

    row = pl.BlockSpec((1, D), lambda i: (0, 0))
    blk = pl.BlockSpec((ts, D), lambda i: (i, 0))
    return pl.pallas_call(
        kern, name=name, grid=(S // ts,),
        in_specs=[blk, row, blk],
        out_specs=[pl.BlockSpec((1, 128), lambda i: (0, 0)), blk, row],
        out_shape=[jax.ShapeDtypeStruct((1, 128), F32), jax.ShapeDtypeStruct((S, D), F32),
                   jax.ShapeDtypeStruct((1, D), F32)],
        compiler_params=pltpu.CompilerParams(dimension_semantics=("arbitrary",), vmem_limit_bytes=VMEM_LIMIT),
    )(x, gain, target)


def _regroup_columns(w, outer, inner, name):
    rows = w.shape[0]

    def kern(i_ref, o_ref):
        o_ref[...] = i_ref[...]

    return pl.pallas_call(
        kern, name=name, grid=(outer, inner),
        in_specs=[pl.BlockSpec((rows, HEAD_DIM), lambda a, b: (0, a * inner + b))],
        out_specs=pl.BlockSpec((rows, HEAD_DIM), lambda a, b: (0, b * outer + a)),
        out_shape=jax.ShapeDtypeStruct(w.shape, w.dtype),
        compiler_params=pltpu.CompilerParams(
            dimension_semantics=("parallel", "parallel"), vmem_limit_bytes=VMEM_LIMIT),
    )(w)


def _heads_major(w, name):
    return _regroup_columns(w, 4, w.shape[1] // (4 * HEAD_DIM), name)


def _heads_minor(w, name):
    return _regroup_columns(w, w.shape[1] // (4 * HEAD_DIM), 4, name)


def _part(ref, j, rows=slice(None)):
    return ref[rows, j * HEAD_DIM:(j + 1) * HEAD_DIM]


def _chunk_scan(v, ric, reverse):
    n = v.shape[0]
    d = 1
    while d < A_CHUNK:
        if reverse:
            v = v + jnp.where(ric < A_CHUNK - d, pltpu.roll(v, n - d, 0), 0.0)
        else:
            v = v + jnp.where(ric >= d, pltpu.roll(v, d, 0), 0.0)
        d *= 2
    return v


def _hgrn2_gates(q_raw, f_logit, lbv, ric):
    qs = _silu(q_raw)
    sig = _sigmoid(f_logit)
    f = lbv + (1.0 - lbv) * sig
    log_f = jnp.log(f)
    kk = (1.0 - lbv) * _sigmoid(-f_logit)
    cum = _chunk_scan(log_f, ric, False)
    cl = cum + _chunk_scan(log_f, ric, True) - log_f
    e_cum = jnp.exp(cum)
    e_neg = jnp.exp(-cum)
    e_end = jnp.exp(cl - cum)
    dec = jnp.exp(cl)
    return qs, sig, f, kk, e_cum, e_neg, e_end, dec


def _same_chunk_mask():
    r = lax.broadcasted_iota(jnp.int32, (HGRN2_GROUP, HGRN2_GROUP), 0)
    c = lax.broadcasted_iota(jnp.int32, (HGRN2_GROUP, HGRN2_GROUP), 1)
    return (r // A_CHUNK == c // A_CHUNK) & (r >= c)


def _hgrn2_specs(S, D, TB, reverse):
    H = D // HEAD_DIM
    NB = S // TB
    pos = (lambda nb: NB - 1 - nb) if reverse else (lambda nb: nb)
    proj = pl.BlockSpec((TB, 4 * HEAD_DIM), lambda h, nb: (pos(nb), h))
    head = pl.BlockSpec((TB, HEAD_DIM), lambda h, nb: (pos(nb), h))
    vec = pl.BlockSpec((1, HEAD_DIM), lambda h, nb: (0, h))
    state = pl.BlockSpec((TB // A_CHUNK, None, HEAD_DIM, HEAD_DIM), lambda h, nb: (pos(nb), h, 0, 0))
    return H, NB, proj, head, vec, state


def _hgrn2_fwd(proj, lb, gain, name):
    S, D4 = proj.shape
    D = D4 // 4
    TB = _pick(S, 512, A_CHUNK)
    H, NB, pspec, head, vec, state = _hgrn2_specs(S, D, TB, False)
    NCB = TB // A_CHUNK

    def kern(p_ref, lb_ref, gain_ref, og_ref, oraw_ref, st_ref, a_s, b_s, k_s, v_s, dec_s, o_s, st_s):
        @pl.when(pl.program_id(1) == 0)
        def _():
            st_s[...] = jnp.zeros_like(st_s)

        ric = lax.broadcasted_iota(jnp.int32, (TB, HEAD_DIM), 0) % A_CHUNK
        qs, _, _, kk, e_cum, e_neg, e_end, dec = _hgrn2_gates(_part(p_ref, 0), _part(p_ref, 1), lb_ref[...], ric)
        a_s[...] = (qs * e_cum).astype(BF16)
        b_s[...] = (kk * e_neg).astype(BF16)
        k_s[...] = (kk * e_end).astype(BF16)
        v_s[...] = _part(p_ref, 2).astype(BF16)
        dec_s[...] = dec

        same_chunk = _same_chunk_mask()
        for gi in range(TB // HGRN2_GROUP):
            rows = pl.ds(gi * HGRN2_GROUP, HGRN2_GROUP)
            p = jnp.where(same_chunk, _dot_nt(a_s[rows, :], b_s[rows, :]), 0.0).astype(BF16)
            o_s[rows, :] = _dot_nn(p, v_s[rows, :])

        def chunks(it, st):
            for u in range(HGRN2_UNROLL):
                ci = it * HGRN2_UNROLL + u
                r = pl.multiple_of(ci * A_CHUNK, A_CHUNK)
                rows = pl.ds(r, A_CHUNK)
                st_bf = st.astype(BF16)
                st_ref[ci] = st_bf
                o_s[rows, :] += _dot_nt(a_s[rows, :], st_bf)
                st = dec_s[pl.ds(r, 1), :] * st + _dot_tn(v_s[rows, :], k_s[rows, :])
            return st

        st_s[...] = lax.fori_loop(0, NCB // HGRN2_UNROLL, chunks, st_s[...])
        o = o_s[...]
        oraw_ref[...] = o
        on = o * lax.rsqrt(jnp.mean(o * o, axis=-1, keepdims=True) + NORM_EPS)
        og_ref[...] = ((on * gain_ref[...]) * _silu(_part(p_ref, 3))).astype(og_ref.dtype)

    tb_bf = pltpu.VMEM((TB, HEAD_DIM), BF16)
    tb_f = pltpu.VMEM((TB, HEAD_DIM), F32)
    return pl.pallas_call(
        kern, name=name, grid=(H, NB),
        in_specs=[pspec, vec, vec],
        out_specs=[head, head, state],
        out_shape=[jax.ShapeDtypeStruct((S, D), BF16), jax.ShapeDtypeStruct((S, D), F32),
                   jax.ShapeDtypeStruct((S // A_CHUNK, H, HEAD_DIM, HEAD_DIM), BF16)],
        scratch_shapes=[tb_bf, tb_bf, tb_bf, tb_bf, tb_f, tb_f, pltpu.VMEM((HEAD_DIM, HEAD_DIM), F32)],
        compiler_params=pltpu.CompilerParams(
            dimension_semantics=("parallel", "arbitrary"), vmem_limit_bytes=VMEM_LIMIT),
    )(proj, lb, gain)


def _hgrn2_bwd(proj, lb, gain, oraw, states, dog, name):
    S, D4 = proj.shape
    D = D4 // 4
    TB = _pick(S, 512, A_CHUNK)
    H, NB, pspec, head, vec, state = _hgrn2_specs(S, D, TB, True)
    NCB = TB // A_CHUNK

    def kern(p_ref, lb_ref, gain_ref, oraw_ref, st_ref, dog_ref, dp_ref, dlb_ref, dgain_ref,
             a_s, b_s, k_s, v_s, do_s, dec_s, da_s, db_s, dk_s, dv_s, ddec_s, dst_s):
        @pl.when(pl.program_id(1) == 0)
        def _():
            dst_s[...] = jnp.zeros_like(dst_s)
            dlb_ref[...] = jnp.zeros_like(dlb_ref)
            dgain_ref[...] = jnp.zeros_like(dgain_ref)

        ric = lax.broadcasted_iota(jnp.int32, (TB, HEAD_DIM), 0) % A_CHUNK
        lbv = lb_ref[...]
        q_raw = _part(p_ref, 0)
        qs, sig, f, kk, e_cum, e_neg, e_end, dec = _hgrn2_gates(q_raw, _part(p_ref, 1), lbv, ric)
        a32, b32, k32 = qs * e_cum, kk * e_neg, kk * e_end
        a_s[...] = a32.astype(BF16)
        b_s[...] = b32.astype(BF16)
        k_s[...] = k32.astype(BF16)
        v_s[...] = _part(p_ref, 2).astype(BF16)
        dec_s[...] = dec

        o = oraw_ref[...]
        gain_v = gain_ref[...]
        g_raw = _part(p_ref, 3)
        rinv = lax.rsqrt(jnp.mean(o * o, axis=-1, keepdims=True) + NORM_EPS)
        on = o * rinv
        dog_v = dog_ref[...].astype(F32)
        dp_ref[:, 3 * HEAD_DIM:4 * HEAD_DIM] = (dog_v * (on * gain_v) * _dsilu(g_raw)).astype(dp_ref.dtype)
        dog2 = dog_v * _silu(g_raw)
        dgain_ref[...] += jnp.sum(dog2 * on, axis=0, keepdims=True)
        don = dog2 * gain_v
        do_s[...] = (rinv * (don - on * jnp.mean(don * on, axis=-1, keepdims=True))).astype(BF16)

        same_chunk = _same_chunk_mask()
        for gi in range(TB // HGRN2_GROUP):
            rows = pl.ds(gi * HGRN2_GROUP, HGRN2_GROUP)
            a, b, do = a_s[rows, :], b_s[rows, :], do_s[rows, :]
            p = jnp.where(same_chunk, _dot_nt(a, b), 0.0).astype(BF16)
            dp = jnp.where(same_chunk, _dot_nt(do, v_s[rows, :]), 0.0).astype(BF16)
            dv_s[rows, :] = _dot_tn(p, do)
            da_s[rows, :] = _dot_nn(dp, b)
            db_s[rows, :] = _dot_tn(dp, a)

        def chunks(it, dst):
            for u in range(HGRN2_UNROLL):
                ci = NCB - 1 - (it * HGRN2_UNROLL + u)
                r = pl.multiple_of(ci * A_CHUNK, A_CHUNK)
                rows = pl.ds(r, A_CHUNK)
                do = do_s[rows, :]
                st_prev = st_ref[ci]
                dst_bf = dst.astype(BF16)
                dv_s[rows, :] += _dot_nt(k_s[rows, :], dst_bf)
                da_s[rows, :] += _dot_nn(do, st_prev)
                dk_s[rows, :] = _dot_nn(v_s[rows, :], dst_bf)
                ddec = jnp.sum(dst * st_prev.astype(F32), axis=0, keepdims=True)
                ddec_s[rows, :] = jnp.broadcast_to(ddec, (A_CHUNK, HEAD_DIM))
                dst = dec_s[pl.ds(r, 1), :] * dst + _dot_tn(do, a_s[rows, :])
            return dst

        dst_s[...] = lax.fori_loop(0, NCB // HGRN2_UNROLL, chunks, dst_s[...])

        da, db, dk = da_s[...], db_s[...], dk_s[...]
        dqs = da * e_cum
        dkk = db * e_neg + dk * e_end
        w = dk * k32
        dlog_f = (_chunk_scan(da * a32 - db * b32, ric, True) + (_chunk_scan(w, ric, False) - w)
                  + ddec_s[...] * dec)
        dfg = dlog_f / f - dkk
        dlb_ref[...] += jnp.sum(dfg * (1.0 - sig), axis=0, keepdims=True)
        dp_ref[:, 0:HEAD_DIM] = (dqs * _dsilu(q_raw)).astype(dp_ref.dtype)
        dp_ref[:, HEAD_DIM:2 * HEAD_DIM] = (dfg * (1.0 - lbv) * sig * (1.0 - sig)).astype(dp_ref.dtype)
        dp_ref[:, 2 * HEAD_DIM:3 * HEAD_DIM] = dv_s[...].astype(dp_ref.dtype)

    tb_bf = pltpu.VMEM((TB, HEAD_DIM), BF16)
    tb_f = pltpu.VMEM((TB, HEAD_DIM), F32)
    vec_shape = jax.ShapeDtypeStruct((1, D), F32)
    return pl.pallas_call(
        kern, name=name, grid=(H, NB),
        in_specs=[pspec, vec, vec, head, state, head],
        out_specs=[pspec, vec, vec],
        out_shape=[jax.ShapeDtypeStruct((S, D4), BF16), vec_shape, vec_shape],
        scratch_shapes=[tb_bf, tb_bf, tb_bf, tb_bf, tb_bf, tb_f, tb_f, tb_f, tb_f, tb_f, tb_f,
                        pltpu.VMEM((HEAD_DIM, HEAD_DIM), F32)],
        compiler_params=pltpu.CompilerParams(
            dimension_semantics=("parallel", "arbitrary"), vmem_limit_bytes=VMEM_LIMIT),
    )(proj, lb, gain, oraw, states, dog)


def _split_sum(v, tri):
    hi = v.astype(BF16)
    lo = (v - hi.astype(F32)).astype(BF16)
    return _dot_nn(hi, tri) + _dot_nn(lo, tri)


def _with_ones(tri):
    return jnp.concatenate([tri, jnp.ones_like(tri)], axis=1)


def _stack(ref, G):
    return jnp.concatenate([ref[:, g * HEAD_DIM:(g + 1) * HEAD_DIM] for g in range(G)], axis=0)


def _unstack(v, G):
    return jnp.concatenate([v[g * SB_TILE:(g + 1) * SB_TILE, :] for g in range(G)], axis=1)


def _sb_specs(S, D, KVH):
    G = D // HEAD_DIM // KVH
    qblk = pl.BlockSpec((SB_TILE, G * HEAD_DIM), lambda h, qi: (qi, h))
    kblk = pl.BlockSpec((S, HEAD_DIM), lambda h, qi: (0, h))
    vblk = pl.BlockSpec((S, HEAD_DIM), lambda h, qi: (0, KVH + h))
    return G, qblk, kblk, vblk


def _tile_index(qi, j):
    return qi * (qi + 1) // 2 + j


def _sb_fwd(q, kv, name):
    S, D = q.shape
    KVH = kv.shape[1] // (2 * HEAD_DIM)
    G, qblk, kblk, vblk = _sb_specs(S, D, KVH)
    R = G * SB_TILE
    NQ = S // SB_TILE
    scale = HEAD_DIM ** -0.5

    def kern(q_ref, k_ref, v_ref, o_ref, w_hbm, b_hbm, acc_s, run_s, w_stage, b_stage, sems):
        head, qi = pl.program_id(0), pl.program_id(1)
        qs = _stack(q_ref, G)
        row = lax.broadcasted_iota(jnp.int32, (R, SB_TILE), 0) % SB_TILE
        colm = lax.broadcasted_iota(jnp.int32, (R, SB_TILE), 1)
        mask = colm < row
        ti = lax.broadcasted_iota(jnp.int32, (SB_TILE, SB_TILE), 0)
        tj = lax.broadcasted_iota(jnp.int32, (SB_TILE, SB_TILE), 1)
        after = _with_ones((ti > tj).astype(BF16))
        acc_s[...] = jnp.zeros_like(acc_s)
        run_s[...] = jnp.zeros_like(run_s)

        def save(slot, j):
            idx = _tile_index(qi, j)
            return (pltpu.make_async_copy(w_stage.at[slot], w_hbm.at[head, idx], sems.at[0, slot]),
                    pltpu.make_async_copy(b_stage.at[slot], b_hbm.at[head, idx], sems.at[1, slot]))

        def tile(j, slot, masked):
            ks = pl.ds(pl.multiple_of(j * SB_TILE, SB_TILE), SB_TILE)
            kj, vj = k_ref[ks, :], v_ref[ks, :]
            z = _dot_nt(qs, kj) * scale
            t = jnp.exp(-jnp.abs(z))
            one_t = 1.0 + t
            log_beta = jnp.minimum(z, 0.0) - jnp.log(one_t)
            log_rest = log_beta - z
            if masked:
                log_rest = jnp.where(mask, log_rest, 0.0)
            sums = _split_sum(log_rest, after)
            between = sums[:, :SB_TILE] + run_s[...]
            w = jnp.exp(log_beta + between)
            if masked:
                w = jnp.where(mask, w, 0.0)
            w_bf = w.astype(BF16)
            acc_s[...] += _dot_nn(w_bf, vj)
            run_s[...] += sums[:, SB_TILE:]
            w_stage[slot] = w_bf
            b_stage[slot] = (jnp.where(z >= 0.0, 1.0, t) * pl.reciprocal(one_t, approx=True)).astype(BF16)
            for cp in save(slot, j):
                cp.start()

        tile(qi, 0, True)

        def body(it, carry):
            slot = (it + 1) % 2

            @pl.when(it >= 1)
            def _():
                for cp in save(slot, 0):
                    cp.wait()

            tile(qi - 1 - it, slot, False)
            return carry

        lax.fori_loop(0, qi, body, 0)
        o_ref[...] = _unstack(acc_s[...], G).astype(o_ref.dtype)
        for cp in save(qi % 2, 0):
            cp.wait()

        @pl.when(qi >= 1)
        def _():
            for cp in save((qi + 1) % 2, 0):
                cp.wait()

    tiles = jax.ShapeDtypeStruct((KVH, NQ * (NQ + 1) // 2, R, SB_TILE), BF16)
    stage = pltpu.VMEM((2, R, SB_TILE), BF16)
    return pl.pallas_call(
        kern, name=name, grid=(KVH, NQ),
        in_specs=[qblk, kblk, vblk],
        out_specs=[qblk, pl.BlockSpec(memory_space=pl.ANY), pl.BlockSpec(memory_space=pl.ANY)],
        out_shape=[jax.ShapeDtypeStruct((S, D), BF16), tiles, tiles],
        scratch_shapes=[pltpu.VMEM((R, HEAD_DIM), F32), pltpu.VMEM((R, SB_TILE), F32), stage, stage,
                        pltpu.SemaphoreType.DMA((2, 2))],
        compiler_params=pltpu.CompilerParams(
            dimension_semantics=("parallel", "arbitrary"), vmem_limit_bytes=VMEM_LIMIT),
    )(q, kv, kv)


def _sb_bwd(q, kv, w_tiles, b_tiles, do, after, name):
    S, D = q.shape
    KVH = kv.shape[1] // (2 * HEAD_DIM)
    G, qblk, kblk, vblk = _sb_specs(S, D, KVH)
    R = G * SB_TILE
    scale = HEAD_DIM ** -0.5

    def kern(q_ref, k_ref, v_ref, w_hbm, b_hbm, do_ref, after_ref, dq_ref, dk_ref, dv_ref,
             dq_s, esum_s, w_stage, b_stage, sems):
        head, qi = pl.program_id(0), pl.program_id(1)

        @pl.when(qi == 0)
        def _():
            dk_ref[...] = jnp.zeros_like(dk_ref)
            dv_ref[...] = jnp.zeros_like(dv_ref)

        qs = _stack(q_ref, G)
        dos = _stack(do_ref, G)
        row = lax.broadcasted_iota(jnp.int32, (R, SB_TILE), 0) % SB_TILE
        colm = lax.broadcasted_iota(jnp.int32, (R, SB_TILE), 1)
        mask = colm < row
        ti = lax.broadcasted_iota(jnp.int32, (SB_TILE, SB_TILE), 0)
        tj = lax.broadcasted_iota(jnp.int32, (SB_TILE, SB_TILE), 1)
        before = (ti < tj).astype(BF16)
        dq_s[...] = jnp.zeros_like(dq_s)
        esum_s[...] = jnp.zeros_like(esum_s)

        def fetch(slot, j):
            idx = _tile_index(qi, j)
            return (pltpu.make_async_copy(w_hbm.at[head, idx], w_stage.at[slot], sems.at[0, slot]),
                    pltpu.make_async_copy(b_hbm.at[head, idx], b_stage.at[slot], sems.at[1, slot]))

        def tile(j, slot, masked):
            for cp in fetch(slot, j):
                cp.wait()
            ks = pl.ds(pl.multiple_of(j * SB_TILE, SB_TILE), SB_TILE)
            kj, vj = k_ref[ks, :], v_ref[ks, :]
            w_bf = w_stage[slot]
            beta = b_stage[slot].astype(F32)
            e = _dot_nt(dos, vj) * w_bf.astype(F32)
            e_before = esum_s[...] + _dot_nn(e.astype(BF16), before)
            dz = e - beta * (e + e_before)
            if masked:
                dz = jnp.where(mask, dz, 0.0)
            dzs = (dz * scale).astype(BF16)
            dq_s[...] += _dot_nn(dzs, kj)
            dk_ref[ks, :] += _dot_tn(dzs, qs)
            dv_ref[ks, :] += _dot_tn(w_bf, dos)
            esum_s[...] += jnp.sum(e, axis=1, keepdims=True)

        for cp in fetch(0, 0):
            cp.start()

        def body(j, carry):
            for cp in fetch((j + 1) % 2, j + 1):
                cp.start()
            tile(j, j % 2, False)
            return carry

        lax.fori_loop(0, qi, body, 0)
        tile(qi, qi % 2, True)
        dq_ref[...] = _unstack(dq_s[...], G).astype(dq_ref.dtype)

    kvout = pl.BlockSpec((S, HEAD_DIM), lambda h, qi: (0, h))
    stage = pltpu.VMEM((2, R, SB_TILE), BF16)
    hbm = pl.BlockSpec(memory_space=pl.ANY)
    return pl.pallas_call(
        kern, name=name, grid=(KVH, S // SB_TILE),
        in_specs=[qblk, kblk, vblk, hbm, hbm, qblk, pl.BlockSpec(after.shape, lambda h, qi: (0, 0))],
        out_specs=[qblk, kvout, kvout],
        out_shape=[jax.ShapeDtypeStruct((S, D), BF16), jax.ShapeDtypeStruct((S, KVH * HEAD_DIM), F32),
                   jax.ShapeDtypeStruct((S, KVH * HEAD_DIM), F32)],
        scratch_shapes=[pltpu.VMEM((R, HEAD_DIM), F32), pltpu.VMEM((R, SB_TILE), F32), stage, stage,
                        pltpu.SemaphoreType.DMA((2, 2))],
        compiler_params=pltpu.CompilerParams(
            dimension_semantics=("parallel", "arbitrary"), vmem_limit_bytes=VMEM_LIMIT),
    )(q, kv, kv, w_tiles, b_tiles, do, after)


def _adamw(parts, w, m, v, name, layer=None, filled=None):
    shape = w.shape
    L = 1 if layer is None else shape[0]
    l = 0 if layer is None else layer
    C = shape[-1]
    R = w.size // (C * L)
    P = parts.shape[0]
    parts3 = parts.reshape(P, R, C)
    w3, m3, v3 = w.reshape(L, R, C), m.reshape(L, R, C), v.reshape(L, R, C)
    tr = _pick(R, max(8, (1 << 18) // C), 16)

    def kern(p_ref, w_ref, m_ref, v_ref, *rest):
        g_ref, d_ref, nm_ref, nv_ref = rest[-4:]
        g = p_ref[0].astype(F32)
        for i in range(1, P):
            g = g + p_ref[i].astype(F32)
        nm = ADAM_B1 * m_ref[...] + (1.0 - ADAM_B1) * g
        nv = ADAM_B2 * v_ref[...] + (1.0 - ADAM_B2) * (g * g)
        m_hat = nm / (1.0 - ADAM_B1 ** ADAM_STEP)
        v_hat = nv / (1.0 - ADAM_B2 ** ADAM_STEP)
        g_ref[...] = g
        d_ref[...] = -ADAM_LR * (m_hat / (jnp.sqrt(v_hat) + ADAM_EPS) + ADAM_WD * w_ref[...])
        nm_ref[...] = nm
        nv_ref[...] = nv

    blk = pl.BlockSpec((None, tr, C), lambda i: (l, i, 0))
    out = jax.ShapeDtypeStruct((L, R, C), F32)
    extra = [] if filled is None else [f.reshape(L, R, C) for f in filled]
    res = pl.pallas_call(
        kern, name=name, grid=(R // tr,),
        in_specs=[pl.BlockSpec((P, tr, C), lambda i: (0, i, 0)), blk, blk, blk]
        + [pl.BlockSpec(memory_space=pl.ANY)] * len(extra),
        out_specs=[blk, blk, blk, blk], out_shape=[out, out, out, out],
        input_output_aliases={4 + j: j for j in range(len(extra))},
        compiler_params=pltpu.CompilerParams(dimension_semantics=("parallel",), vmem_limit_bytes=VMEM_LIMIT),
    )(parts3, w3, m3, v3, *extra)
    return tuple(r.reshape(shape) for r in res)


def _lower_bound(logits):
    return jnp.cumsum(jax.nn.softmax(logits.astype(F32), axis=0), axis=0)[0:1]


def _pad_rows(a, rows):
    return jnp.zeros((rows, a.shape[1]), a.dtype).at[:a.shape[0]].set(a)


def kernel(x, c, ada_w, ada_b, norm_mix, norm_mlp, a_w_in, a_lb_logits, a_out_gain, a_w_out, kv_ada_w, kv_ada_b, kv_norm, w_kv, b_w_q, b_w_out, mlp_w1, mlp_w2, final_norm, loss_target, m_ada_w, m_ada_b, m_norm_mix, m_norm_mlp, m_a_w_in, m_a_lb_logits, m_a_out_gain, m_a_w_out, m_kv_ada_w, m_kv_ada_b, m_kv_norm, m_w_kv, m_b_w_q, m_b_w_out, m_mlp_w1, m_mlp_w2, m_final_norm, v_ada_w, v_ada_b, v_norm_mix, v_norm_mlp, v_a_w_in, v_a_lb_logits, v_a_out_gain, v_a_w_out, v_kv_ada_w, v_kv_ada_b, v_kv_norm, v_w_kv, v_b_w_q, v_b_w_out, v_mlp_w1, v_mlp_w2, v_final_norm):
    S, D = x.shape[1], x.shape[2]
    assert x.shape[0] == 1 and ada_w.shape[0] == 2 and a_w_in.shape[0] == 1 and b_w_q.shape[0] == 1
    me = 4 * lax.axis_index("x") + 2 * lax.axis_index("y") + lax.axis_index("c")
    dl = D // N_DEV
    na = ada_w.shape[2]
    nk = kv_ada_w.shape[1]

    small = jnp.concatenate([c.reshape(1, D), a_lb_logits.reshape(1, 2 * dl), a_out_gain.reshape(1, dl)], axis=1)
    small_all = _all_gather(small, "gather_small")[:, 0, :]
    c_all = small_all[:, :D]
    lb_logits = small_all[:, D:D + 2 * dl].reshape(N_DEV, 2, dl).transpose(1, 0, 2).reshape(2, D)
    out_gain = small_all[:, D + 2 * dl:].reshape(1, D)

    c_act = jax.nn.silu(c_all)
    c_act_rows = _pad_rows(c_act.astype(BF16), 128)
    mod_cols = jnp.concatenate([
        _matmul(c_act_rows, ada_w[0].astype(BF16), "nn", (F32,), "ada0")[0][:N_DEV],
        _matmul(c_act_rows, ada_w[1].astype(BF16), "nn", (F32,), "ada1")[0][:N_DEV],
        _matmul(c_act_rows, kv_ada_w.astype(BF16), "nn", (F32,), "ada_kv")[0][:N_DEV]], axis=1)
    mod_all = _all_gather(mod_cols, "gather_mod")
    mod_mine = lax.dynamic_index_in_dim(mod_all, me, axis=1, keepdims=False)
    mod0 = mod_mine[:, :na].reshape(1, 6 * D) + ada_b[0:1]
    mod1 = mod_mine[:, na:2 * na].reshape(1, 6 * D) + ada_b[1:2]
    modk = mod_mine[:, 2 * na:].reshape(1, 2 * D) + kv_ada_b.reshape(1, 2 * D)
    sh1a, sc1a, g1a, sh2a, sc2a, g2a = jnp.split(mod0, 6, axis=1)
    sh1b, sc1b, g1b, sh2b, sc2b, g2b = jnp.split(mod1, 6, axis=1)
    kv_sh, kv_sc = jnp.split(modk, 2, axis=1)
    nmix0, nmix1, nmlp0, nmlp1 = norm_mix[0:1], norm_mix[1:2], norm_mlp[0:1], norm_mlp[1:2]
    kvn = kv_norm.reshape(1, D)
    lb, lb_vjp = jax.vjp(_lower_bound, lb_logits)

    w_in = _all_gather(a_w_in[0].astype(BF16), "gather_a_w_in", axis=1)
    w_names = ["a_w_out", "mlp0_w1", "mlp0_w2", "w_kv", "b_w_q", "b_w_out", "mlp1_w1", "mlp1_w2"]
    w_shards = [a_w_out[0], mlp_w1[0], mlp_w2[0], w_kv, b_w_q[0], b_w_out[0], mlp_w1[1], mlp_w2[1]]
    w_axes = [0, 1, 0, 0, 0, 0, 1, 0]
    w_shards = [s.astype(BF16) for s in w_shards]
    w_routes = [_gather_route(s, ax) for s, ax in zip(w_shards, w_axes)]
    w_handles, w_token = _copies_start(w_shards, w_routes, w_in, "gather_weights_start")

    def weight(key, after):
        i = w_names.index(key)
        return _copies_wait(w_handles[i], w_routes[i], after, "gather_" + key + "_wait")

    x0 = x[0]
    h1 = _norm_mod_fwd(x0, nmix0, sc1a + w_token[0:1, 0:1], sh1a, "a_in_norm")
    w_in = _heads_major(w_in, "a_w_in_by_head")
    (proj,) = _matmul(h1, w_in, "nn", (F32,), "a_in_mm")
    og, oraw, states = _hgrn2_fwd(proj, lb, out_gain, "hgrn2_fwd")
    w_aout = weight("a_w_out", og)
    x1, y1 = _matmul(og, w_aout, "nn", (F32, BF16), "a_out_mm", _residual_epilogue, (x0,), (g1a,))
    h2 = _norm_mod_fwd(x1, nmlp0, sc2a, sh2a, "mlp0_up_norm")
    w1a = weight("mlp0_w1", h2)
    u0, a0 = _matmul(h2, w1a, "nn", (BF16, BF16), "mlp0_up_mm", _relu2_epilogue)
    w2a = weight("mlp0_w2", u0)
    x2, y2 = _matmul(u0, w2a, "nn", (F32, BF16), "mlp0_down_mm", _residual_epilogue, (x1,), (g2a,))

    hk = _norm_mod_fwd(x2, kvn, kv_sc, kv_sh, "kv_norm")
    w_kvf = weight("w_kv", hk)
    (kv,) = _matmul(hk, w_kvf, "nn", (BF16,), "kv_mm")
    h3 = _norm_mod_fwd(x2, nmix1, sc1b, sh1b, "b_q_norm")
    w_q = weight("b_w_q", h3)
    (q,) = _matmul(h3, w_q, "nn", (BF16,), "b_q_mm")
    o, w_tiles, b_tiles = _sb_fwd(q, kv, "attn_fwd")
    w_bout = weight("b_w_out", o)
    x3, y3 = _matmul(o, w_bout, "nn", (F32, BF16), "b_out_mm", _residual_epilogue, (x2,), (g1b,))
    h4 = _norm_mod_fwd(x3, nmlp1, sc2b, sh2b, "mlp1_up_norm")
    w1b = weight("mlp1_w1", h4)
    u1, a1 = _matmul(h4, w1b, "nn", (BF16, BF16), "mlp1_up_mm", _relu2_epilogue)
    w2b = weight("mlp1_w2", u1)
    x4, y4 = _matmul(u1, w2b, "nn", (F32, BF16), "mlp1_down_mm", _residual_epilogue, (x3,), (g2b,))

    loss_vec, d4, d_final = _final_loss(x4, final_norm.reshape(1, D), loss_target[0], "final_loss")
    loss = lax.psum(loss_vec[0, 0], MESH_AXES)

    sent = {}

    def send(key, g, axis):
        route = _scatter_route(g, axis)
        (handle,), token = _copies_start([g], [route], None, "scatter_" + key + "_start")
        sent[key] = (handle, route)
        return token

    def behind(vec, *tokens):
        for token in tokens:
            vec = vec + token[0:1, 0:1]
        return vec

    def mlp_bwd(tag, d, y, g, u, a, w2, w1, h, x_in, gain, sc):
        dy, dg = _gate_bwd(d, y, g, tag + "_down_dgate")
        (dw2,) = _matmul(u, dy, "tn", (BF16,), tag + "_down_dw")
        token2 = send(tag + "_w2", dw2, 0)
        (dz,) = _matmul(dy, w2, "nt", (BF16,), tag + "_down_dz", lambda acc, a_tile: (acc * (2.0 * a_tile),), (a,))
        (dw1,) = _matmul(h, dz, "tn", (BF16,), tag + "_up_dw")
        token1 = send(tag + "_w1", dw1, 1)
        (dh,) = _matmul(dz, w1, "nt", (F32,), tag + "_up_dh")
        d_out, dgain, dsc, dsh = _norm_mod_bwd(x_in, gain, behind(sc, token2, token1), dh, d, tag + "_up_dnorm")
        return d_out, dg, dgain, dsc, dsh

    d3, dg2b, dnmlp1, dsc2b, dsh2b = mlp_bwd("mlp1", d4, y4, g2b, u1, a1, w2b, w1b, h4, x3, nmlp1, sc2b)

    dy3, dg1b = _gate_bwd(d3, y3, g1b, "b_out_dgate")
    (dw_bout,) = _matmul(o, dy3, "tn", (BF16,), "b_out_dw")
    token = send("b_w_out", dw_bout, 0)
    (do,) = _matmul(dy3, w_bout, "nt", (BF16,), "b_out_du")
    dq, dk, dv = _sb_bwd(q, kv, w_tiles, b_tiles, do, token, "attn_bwd")
    dkv = jnp.concatenate([dk, dv], axis=1).astype(BF16)
    (dw_q,) = _matmul(h3, dq, "tn", (BF16,), "b_q_dw")
    token = send("b_w_q", dw_q, 0)
    (dh3,) = _matmul(dq, w_q, "nt", (F32,), "b_q_dh")
    d2, dnmix1, dsc1b, dsh1b = _norm_mod_bwd(x2, nmix1, behind(sc1b, token), dh3, d3, "b_q_dnorm")
    (dw_kv,) = _matmul(hk, dkv, "tn", (BF16,), "kv_dw")
    token = send("w_kv", dw_kv, 0)
    (dhk,) = _matmul(dkv, w_kvf, "nt", (F32,), "kv_dh")
    d2, dkvn, dkv_sc, dkv_sh = _norm_mod_bwd(x2, kvn, behind(kv_sc, token), dhk, d2, "kv_dnorm")

    d1, dg2a, dnmlp0, dsc2a, dsh2a = mlp_bwd("mlp0", d2, y2, g2a, u0, a0, w2a, w1a, h2, x1, nmlp0, sc2a)

    dy1, dg1a = _gate_bwd(d1, y1, g1a, "a_out_dgate")
    (dw_aout,) = _matmul(og, dy1, "tn", (BF16,), "a_out_dw")
    token = send("a_w_out", dw_aout, 0)
    (dog,) = _matmul(dy1, w_aout, "nt", (BF16,), "a_out_du")
    dproj, dlb, d_out_gain = _hgrn2_bwd(proj, behind(lb, token), out_gain, oraw, states, dog, "hgrn2_bwd")
    (dw_in,) = _matmul(h1, dproj, "tn", (BF16,), "a_in_dw")
    token = send("a_w_in", _heads_minor(dw_in, "a_w_in_grad_by_part"), 1)
    (dh1,) = _matmul(dproj, w_in, "nt", (F32,), "a_in_dh")
    d0, dnmix0, dsc1a, dsh1a = _norm_mod_bwd(x0, nmix0, behind(sc1a, token), dh1, d1, "a_in_dnorm")
    (d_lb_logits,) = lb_vjp(dlb)

    dmod0 = jnp.concatenate([dsh1a, dsc1a, dg1a, dsh2a, dsc2a, dg2a], axis=1)
    dmod1 = jnp.concatenate([dsh1b, dsc1b, dg1b, dsh2b, dsc2b, dg2b], axis=1)
    dmodk = jnp.concatenate([dkv_sh, dkv_sc], axis=1)
    pieces = [dmod0, dmod1, dmodk, dnmix0, dnmix1, dnmlp0, dnmlp1, dkvn, d_final,
              d_lb_logits.reshape(1, 2 * D), d_out_gain]
    widths = [p.shape[1] for p in pieces]
    offs = [sum(widths[:i]) for i in range(len(widths))]
    part_all = _all_gather(jnp.concatenate(pieces, axis=1), "gather_dsmall")[:, 0, :]
    take = lambda i: part_all[:, offs[i]:offs[i] + widths[i]]
    dmod0_all, dmod1_all, dmodk_all = take(0), take(1), take(2)

    outs = {}

    def update(key, parts, w, m, v, **kw):
        outs[key] = _adamw(parts, w, m, v, "adamw_" + key + ("_%d" % kw["layer"] if "layer" in kw else ""), **kw)

    c_act_cols = _pad_rows(c_act.astype(BF16), 128).T
    my_cols = lambda a, n: _pad_rows(lax.dynamic_slice_in_dim(a, me * n, n, axis=1).astype(BF16), 128)
    g_ada0 = _matmul(c_act_cols, my_cols(dmod0_all, na), "nn", (F32,), "dada0")[0]
    g_ada1 = _matmul(c_act_cols, my_cols(dmod1_all, na), "nn", (F32,), "dada1")[0]
    g_adak = _matmul(c_act_cols, my_cols(dmodk_all, nk), "nn", (F32,), "dada_kv")[0]
    update("ada_w", g_ada1[None], ada_w, m_ada_w, v_ada_w, layer=1)
    update("ada_w", g_ada0[None], ada_w, m_ada_w, v_ada_w, layer=0, filled=outs["ada_w"])
    update("kv_ada_w", g_adak[None], kv_ada_w, m_kv_ada_w, v_kv_ada_w)
    update("ada_b", jnp.stack([dmod0_all, dmod1_all], axis=1), ada_b, m_ada_b, v_ada_b)
    update("kv_ada_b", dmodk_all, kv_ada_b, m_kv_ada_b, v_kv_ada_b)
    update("norm_mix", jnp.stack([take(3), take(4)], axis=1), norm_mix, m_norm_mix, v_norm_mix)
    update("norm_mlp", jnp.stack([take(5), take(6)], axis=1), norm_mlp, m_norm_mlp, v_norm_mlp)
    update("kv_norm", take(7), kv_norm, m_kv_norm, v_kv_norm)
    update("final_norm", take(8), final_norm, m_final_norm, v_final_norm)
    d_lb_all = take(9).reshape(N_DEV, 2, D)
    update("a_lb_logits", lax.dynamic_slice_in_dim(d_lb_all, me * dl, dl, axis=2), a_lb_logits, m_a_lb_logits,
           v_a_lb_logits)
    update("a_out_gain", lax.dynamic_slice_in_dim(take(10), me * dl, dl, axis=1)[:, None, :], a_out_gain,
           m_a_out_gain, v_a_out_gain)

    def landed(key, after):
        handle, route = sent[key]
        return _copies_wait(handle, route, after, "scatter_" + key + "_wait")

    update("mlp_w2", landed("mlp1_w2", d0), mlp_w2, m_mlp_w2, v_mlp_w2, layer=1)
    update("mlp_w1", landed("mlp1_w1", d0), mlp_w1, m_mlp_w1, v_mlp_w1, layer=1)
    update("b_w_out", landed("b_w_out", d0), b_w_out, m_b_w_out, v_b_w_out)
    update("b_w_q", landed("b_w_q", d0), b_w_q, m_b_w_q, v_b_w_q)
    update("w_kv", landed("w_kv", d0), w_kv, m_w_kv, v_w_kv)
    update("mlp_w2", landed("mlp0_w2", d0), mlp_w2, m_mlp_w2, v_mlp_w2, layer=0, filled=outs["mlp_w2"])
    update("mlp_w1", landed("mlp0_w1", d0), mlp_w1, m_mlp_w1, v_mlp_w1, layer=0, filled=outs["mlp_w1"])
    update("a_w_out", landed("a_w_out", d0), a_w_out, m_a_w_out, v_a_w_out)
    update("a_w_in", landed("a_w_in", outs["mlp_w1"][0]), a_w_in, m_a_w_in, v_a_w_in)

    names = ["ada_w", "ada_b", "norm_mix", "norm_mlp", "a_w_in", "a_lb_logits", "a_out_gain", "a_w_out", "kv_ada_w",
             "kv_ada_b", "kv_norm", "w_kv", "b_w_q", "b_w_out", "mlp_w1", "mlp_w2", "final_norm"]
    result = [loss, d0[None]]
    for field in range(4):
        result += [outs[n][field] for n in names]
    return tuple(result)
```

```python
import jax
import jax.numpy as jnp
from jax import lax
from jax.experimental import pallas as pl
from jax.experimental.pallas import tpu as pltpu

F32 = jnp.float32
BF16 = jnp.bfloat16

N_DEV = 8
MESH_AXES = ("x", "y", "c")
HEAD_DIM = 128
A_CHUNK = 16
HGRN2_UNROLL = 8
HGRN2_GROUP = 128
SB_TILE = 256
NORM_EPS = 1e-6
ADAM_LR = 0.001
ADAM_B1 = 0.9
ADAM_B2 = 0.999
ADAM_EPS = 1e-08
ADAM_WD = 0.01
ADAM_STEP = 10
VMEM_LIMIT = 56 * 1024 * 1024
MATMUL_VMEM = 40 * 1024 * 1024


def _pick(dim, target, align):
    t = (min(dim, target) // align) * align
    while t >= align:
        if dim % t == 0:
            return t
        t -= align
    return dim


def _sigmoid(x):
    return 1.0 / (1.0 + jnp.exp(-x))


def _silu(x):
    return x * _sigmoid(x)


def _dsilu(x):
    s = _sigmoid(x)
    return s * (1.0 + x * (1.0 - s))


def _dot(a, b, dims):
    return lax.dot_general(a, b, (dims, ((), ())), preferred_element_type=F32)


def _dot_nn(a, b):
    return _dot(a, b, ((1,), (0,)))


def _dot_nt(a, b):
    return _dot(a, b, ((1,), (1,)))


def _dot_tn(a, b):
    return _dot(a, b, ((0,), (0,)))


def _mesh_pos():
    return lax.axis_index("x"), lax.axis_index("y"), lax.axis_index("c")


def _flip(v, d):
    return 1 - v if d else v


def _all_gather(x, name, axis=None):
    if axis is None:
        out_shape, place = (N_DEV,) + x.shape, lambda d: (d,)
    else:
        route = _gather_route(x, axis)
        out_shape, place = route.land_shape, route.dst_slice

    def body(x_ref, out_ref, send_sems, recv_sems, local_sem):
        x, y, c = _mesh_pos()
        me, sibling = (x, y, c), (x, y, 1 - c)
        chips = [(1 - x, y), (x, 1 - y), (1 - x, 1 - y)]

        def rows(px, py, pc):
            return out_ref.at[place(4 * px + 2 * py + pc)]

        def copy(k, block, to, src=None):
            return pltpu.make_async_remote_copy(
                src_ref=rows(*block) if src is None else src,
                dst_ref=rows(*block),
                send_sem=send_sems.at[k],
                recv_sem=recv_sems.at[k],
                device_id=to,
                device_id_type=pl.DeviceIdType.MESH,
            )

        mine = pltpu.make_async_copy(x_ref, rows(*me), local_sem)
        mine.start()
        first = [copy(0, me, sibling, src=x_ref)]
        first += [copy(1 + j, me, (*chip, c), src=x_ref) for j, chip in enumerate(chips)]
        for cp in first:
            cp.start()
        passed = [copy(4 + j, (*chip, c), sibling) for j, chip in enumerate(chips)]
        for j, chip in enumerate(chips):
            copy(1 + j, (*chip, c), me).wait_recv()
            passed[j].start()
        copy(0, sibling, me).wait_recv()
        for j, chip in enumerate(chips):
            copy(4 + j, (*chip, 1 - c), me).wait_recv()
        for cp in first + passed:
            cp.wait_send()
        mine.wait()

    return pl.pallas_call(
        body,
        name=name,
        out_shape=jax.ShapeDtypeStruct(out_shape, x.dtype),
        in_specs=[pl.BlockSpec(memory_space=pl.ANY)],
        out_specs=pl.BlockSpec(memory_space=pl.ANY),
        scratch_shapes=[
            pltpu.SemaphoreType.DMA((7,)),
            pltpu.SemaphoreType.DMA((7,)),
            pltpu.SemaphoreType.DMA(()),
        ],
    )(x)


_RELATIONS = [(dx, dy, dc) for dx in (0, 1) for dy in (0, 1) for dc in (0, 1) if (dx, dy, dc) != (0, 0, 0)]
_HBM = pl.BlockSpec(memory_space=pltpu.HBM)
_SEM = pl.BlockSpec(memory_space=pltpu.SEMAPHORE)
_EFFECT = pltpu.SideEffectType.DATAFLOW_SIDE_EFFECTING


def _at(ref, idx):
    return ref.at[idx] if idx else ref


def _block(axis, ndim, n):
    return lambda d: (slice(None),) * axis + (pl.ds(d * n, n),) + (slice(None),) * (ndim - axis - 1)


class _Route:
    def __init__(self, land_shape, src_slice, dst_slice):
        self.land_shape, self.src_slice, self.dst_slice = tuple(land_shape), src_slice, dst_slice


def _gather_route(shard, axis):
    n = shard.shape[axis]
    shape = shard.shape[:axis] + (N_DEV * n,) + shard.shape[axis + 1:]
    return _Route(shape, lambda p: (), _block(axis, shard.ndim, n))


def _scatter_route(g, axis):
    n = g.shape[axis] // N_DEV
    shape = (N_DEV,) + g.shape[:axis] + (n,) + g.shape[axis + 1:]
    return _Route(shape, _block(axis, g.ndim, n), lambda p: (p,))


def _peers():
    x, y, c = _mesh_pos()
    out = []
    for k, (dx, dy, dc) in enumerate(_RELATIONS):
        px, py, pc = _flip(x, dx), _flip(y, dy), _flip(c, dc)
        out.append((k, (px, py, pc), 4 * px + 2 * py + pc))
    return 4 * x + 2 * y + c, out


_LOCAL = len(_RELATIONS)


def _copies_start(srcs, routes, after, name):
    n = len(srcs)
    order = [] if after is None else [after]

    def body(*refs):
        src_refs, land_refs = refs[:n], refs[n:2 * n]
        outs = refs[2 * n + len(order):]
        send, recv, token = outs[:n], outs[n:2 * n], outs[-1]
        me, peers = _peers()
        for i, route in enumerate(routes):
            for k, peer, pid in peers:
                pltpu.make_async_remote_copy(
                    src_ref=_at(src_refs[i], route.src_slice(pid)), dst_ref=_at(land_refs[i], route.dst_slice(me)),
                    send_sem=send[i].at[k], recv_sem=recv[i].at[k],
                    device_id=peer, device_id_type=pl.DeviceIdType.MESH).start()
            pltpu.make_async_copy(_at(src_refs[i], route.src_slice(me)), _at(land_refs[i], route.dst_slice(me)),
                                  send[i].at[_LOCAL]).start()
        token[...] = jnp.zeros_like(token)

    lands = [lax.empty(r.land_shape, s.dtype) for s, r in zip(srcs, routes)]
    hbm = lambda a: pltpu.with_memory_space_constraint(a, pltpu.HBM)
    res = pl.pallas_call(
        body, name=name,
        out_shape=([pltpu.SemaphoreType.DMA((_LOCAL + 1,))] * n + [pltpu.SemaphoreType.DMA((_LOCAL,))] * n
                   + [pltpu.HBM(s.shape, s.dtype) for s in srcs]
                   + [pltpu.HBM(l.shape, l.dtype) for l in lands]
                   + [jax.ShapeDtypeStruct((8, 128), F32)]),
        in_specs=[_HBM] * (2 * n) + [pl.BlockSpec(memory_space=pl.ANY)] * len(order),
        out_specs=[_SEM] * (2 * n) + [_HBM] * (2 * n) + [pl.BlockSpec(memory_space=pltpu.VMEM)],
        input_output_aliases={i: 2 * n + i for i in range(2 * n)},
        compiler_params=pltpu.CompilerParams(has_side_effects=_EFFECT),
    )(*[hbm(s) for s in srcs], *[hbm(l) for l in lands], *order)
    handles = [(res[i], res[n + i], res[2 * n + i], res[3 * n + i]) for i in range(n)]
    return handles, res[-1]


def _copies_wait(handle, route, after, name):
    send_sems, recv_sems, src, land = handle

    def body(src_ref, land_ref, send_ref, recv_ref, after_ref, src_out, got_ref):
        me, peers = _peers()
        for k, peer, pid in peers:
            cp = pltpu.make_async_remote_copy(
                src_ref=_at(src_ref, route.src_slice(pid)), dst_ref=_at(land_ref, route.dst_slice(pid)),
                send_sem=send_ref.at[k], recv_sem=recv_ref.at[k],
                device_id=peer, device_id_type=pl.DeviceIdType.MESH)
            cp.wait_send()
            cp.wait_recv()
        pltpu.make_async_copy(_at(src_ref, route.src_slice(me)), _at(land_ref, route.dst_slice(me)),
                              send_ref.at[_LOCAL]).wait()

    return pl.pallas_call(
        body, name=name,
        out_shape=(pltpu.HBM(src.shape, src.dtype), pltpu.HBM(land.shape, land.dtype)),
        in_specs=[_HBM, _HBM, _SEM, _SEM, pl.BlockSpec(memory_space=pl.ANY)],
        out_specs=(_HBM, _HBM),
        input_output_aliases={0: 0, 1: 1},
        compiler_params=pltpu.CompilerParams(has_side_effects=_EFFECT),
    )(src, land, send_sems, recv_sems, after)[1]


def _matmul_tiles(M, N, K, a_item, b_item, mn_bytes, whole_rows):
    tm, tk = _pick(M, 512, 128), _pick(K, 2048, 128)
    tn = N if whole_rows else _pick(N, 2048, 128)
    while True:
        cast = (tm * tk * 2 if a_item != 2 else 0) + (tk * tn * 2 if b_item != 2 else 0)
        need = 2 * (tm * tk * a_item + tk * tn * b_item + tm * tn * mn_bytes) + 2 * tm * tn * 4 + cast
        if whole_rows:
            smaller = _pick(M, tm - 128, 128) if tm > 128 else tm
            if need <= MATMUL_VMEM or smaller >= tm:
                return tm, tn, tk
            tm = smaller
        else:
            smaller = _pick(N, tn - 128, 128) if tn > 128 else tn
            if need <= MATMUL_VMEM or smaller >= tn:
                return tm, tn, tk
            tn = smaller


def _matmul(a, b, mode, out_dtypes, name, epilogue=None, tiles=(), rows=(), whole_rows=False):
    if mode == "nn":
        (M, K), (K2, N) = a.shape, b.shape
    elif mode == "nt":
        (M, K), (N, K2) = a.shape, b.shape
    else:
        (K, M), (K2, N) = a.shape, b.shape
    assert K == K2, (a.shape, b.shape, mode)
    mn_bytes = sum(t.dtype.itemsize for t in tiles) + sum(jnp.dtype(d).itemsize for d in out_dtypes)
    tm, tn, tk = _matmul_tiles(M, N, K, a.dtype.itemsize, b.dtype.itemsize, mn_bytes, whole_rows)
    nm, nn, nk = M // tm, N // tn, K // tk
    n_extra = len(tiles) + len(rows)
    n_out = len(out_dtypes)
    if epilogue is None:
        epilogue = lambda acc: (acc,)
    dims = {"nn": ((1,), (0,)), "nt": ((1,), (1,)), "tn": ((0,), (0,))}[mode]

    def finish(acc, extra, outs):
        vals = epilogue(acc, *[r[...] for r in extra])
        for o_ref, v in zip(outs, vals):
            o_ref[...] = v.astype(o_ref.dtype)

    def kern(*refs):
        a_ref, b_ref = refs[0], refs[1]
        extra = refs[2:2 + n_extra]
        outs = refs[2 + n_extra:2 + n_extra + n_out]
        part = _dot(a_ref[...].astype(BF16), b_ref[...].astype(BF16), dims)
        if nk == 1:
            finish(part, extra, outs)
            return
        acc_ref = refs[-1]
        k = pl.program_id(2)

        @pl.when(k == 0)
        def _():
            acc_ref[...] = part

        @pl.when(k > 0)
        def _():
            acc_ref[...] += part

        @pl.when(k == nk - 1)
        def _():
            finish(acc_ref[...], extra, outs)

    a_bytes, b_bytes = a.size * a.dtype.itemsize, b.size * b.dtype.itemsize
    m_outer = a_bytes + nm * b_bytes <= nn * a_bytes + b_bytes
    ij = (lambda g0, g1: (g0, g1)) if m_outer else (lambda g0, g1: (g1, g0))

    def spec(shape, fn):
        return pl.BlockSpec(shape, lambda g0, g1, k: fn(*ij(g0, g1), k))

    a_spec = {"nn": spec((tm, tk), lambda i, j, k: (i, k)),
              "nt": spec((tm, tk), lambda i, j, k: (i, k)),
              "tn": spec((tk, tm), lambda i, j, k: (k, i))}[mode]
    b_spec = {"nn": spec((tk, tn), lambda i, j, k: (k, j)),
              "nt": spec((tn, tk), lambda i, j, k: (j, k)),
              "tn": spec((tk, tn), lambda i, j, k: (k, j))}[mode]
    tile_spec = spec((tm, tn), lambda i, j, k: (i, j))
    row_spec = spec((1, tn), lambda i, j, k: (0, j))
    return pl.pallas_call(
        kern,
        name=name,
        grid=(nm, nn, nk) if m_outer else (nn, nm, nk),
        in_specs=[a_spec, b_spec] + [tile_spec] * len(tiles) + [row_spec] * len(rows),
        out_specs=[tile_spec] * n_out,
        out_shape=[jax.ShapeDtypeStruct((M, N), dt) for dt in out_dtypes],
        scratch_shapes=[pltpu.VMEM((tm, tn), F32)] if nk > 1 else [],
        compiler_params=pltpu.CompilerParams(
            dimension_semantics=("parallel", "parallel", "arbitrary"), vmem_limit_bytes=VMEM_LIMIT),
    )(a, b, *tiles, *rows)


def _relu2_epilogue(acc):
    a = jnp.maximum(acc, 0.0)
    return a * a, a


def _modulated_norm(x, gain, sc, sh):
    inv = lax.rsqrt(jnp.mean(x * x, axis=-1, keepdims=True) + NORM_EPS)
    return (x * inv) * gain * (1.0 + sc) + sh


def _residual_epilogue(acc, x_tile, gate_row, *norm_rows):
    x_new = x_tile + gate_row * acc
    normed = [_modulated_norm(x_new, *norm_rows[i:i + 3]) for i in range(0, len(norm_rows), 3)]
    return (x_new, acc, *normed)


def _norm_mod_fwd(x, gain, sc, sh, name):
    S, D = x.shape
    ts = _pick(S, 256, 16)

    def kern(x_ref, g_ref, sc_ref, sh_ref, h_ref):
        h_ref[...] = _modulated_norm(x_ref[...], g_ref[...], sc_ref[...], sh_ref[...]).astype(h_ref.dtype)

    row = pl.BlockSpec((1, D), lambda i: (0, 0))
    blk = pl.BlockSpec((ts, D), lambda i: (i, 0))
    return pl.pallas_call(
        kern, name=name, grid=(S // ts,),
        in_specs=[blk, row, row, row], out_specs=blk,
        out_shape=jax.ShapeDtypeStruct((S, D), BF16),
        compiler_params=pltpu.CompilerParams(dimension_semantics=("parallel",), vmem_limit_bytes=VMEM_LIMIT),
    )(x, gain, sc, sh)


def _norm_mod_bwd(x, gain, sc, dh, d_in, name):
    S, D = x.shape
    ts = _pick(S, 256, 8)

    def kern(x_ref, g_ref, sc_ref, dh_ref, din_ref, dx_ref, dg_ref, dsc_ref, dsh_ref):
        @pl.when(pl.program_id(0) == 0)
        def _():
            dg_ref[...] = jnp.zeros_like(dg_ref)
            dsc_ref[...] = jnp.zeros_like(dsc_ref)
            dsh_ref[...] = jnp.zeros_like(dsh_ref)

        xv = x_ref[...]
        dh = dh_ref[...].astype(F32)
        inv = lax.rsqrt(jnp.mean(xv * xv, axis=-1, keepdims=True) + NORM_EPS)
        xn = xv * inv
        g = g_ref[...]
        dsh_ref[...] += jnp.sum(dh, axis=0, keepdims=True)
        dsc_ref[...] += jnp.sum(dh * (xn * g), axis=0, keepdims=True)
        dn = dh * (1.0 + sc_ref[...])
        dg_ref[...] += jnp.sum(dn * xn, axis=0, keepdims=True)
        dxn = dn * g
        dx_ref[...] = din_ref[...] + inv * (dxn - xn * jnp.mean(dxn * xn, axis=-1, keepdims=True))

    row = pl.BlockSpec((1, D), lambda i: (0, 0))
    blk = pl.BlockSpec((ts, D), lambda i: (i, 0))
    vec = jax.ShapeDtypeStruct((1, D), F32)
    return pl.pallas_call(
        kern, name=name, grid=(S // ts,),
        in_specs=[blk, row, row, blk, blk], out_specs=[blk, row, row, row],
        out_shape=[jax.ShapeDtypeStruct((S, D), F32), vec, vec, vec],
        compiler_params=pltpu.CompilerParams(dimension_semantics=("arbitrary",), vmem_limit_bytes=VMEM_LIMIT),
    )(x, gain, sc, dh, d_in)


def _gate_bwd(d, y, g, name):
    S, D = d.shape
    ts = _pick(S, 256, 16)

    def kern(d_ref, y_ref, g_ref, dy_ref, dg_ref):
        @pl.when(pl.program_id(0) == 0)
        def _():
            dg_ref[...] = jnp.zeros_like(dg_ref)

        dv = d_ref[...]
        dy_ref[...] = (g_ref[...] * dv).astype(dy_ref.dtype)
        dg_ref[...] += jnp.sum(dv * y_ref[...].astype(F32), axis=0, keepdims=True)

    row = pl.BlockSpec((1, D), lambda i: (0, 0))
    blk = pl.BlockSpec((ts, D), lambda i: (i, 0))
    return pl.pallas_call(
        kern, name=name, grid=(S // ts,), in_specs=[blk, blk, row], out_specs=[blk, row],
        out_shape=[jax.ShapeDtypeStruct((S, D), BF16), jax.ShapeDtypeStruct((1, D), F32)],
        compiler_params=pltpu.CompilerParams(dimension_semantics=("arbitrary",), vmem_limit_bytes=VMEM_LIMIT),
    )(d, y, g)


def _final_loss(x, gain, target, name):
    S, D = x.shape
    ts = _pick(S, 256, 8)

    def kern(x_ref, g_ref, t_ref, loss_ref, dx_ref, dg_ref):
        @pl.when(pl.program_id(0) == 0)
        def _():
            loss_ref[...] = jnp.zeros_like(loss_ref)
            dg_ref[...] = jnp.zeros_like(dg_ref)

        xv = x_ref[...]
        g = g_ref[...]
        inv = lax.rsqrt(jnp.mean(xv * xv, axis=-1, keepdims=True) + NORM_EPS)
        xn = xv * inv
        err = xn * g - t_ref[...]
        row_loss = jnp.mean(err * err, axis=-1, keepdims=True)
        loss_ref[...] += 0.5 * jnp.sum(row_loss, axis=0, keepdims=True)
        dy = err * (1.0 / D)
        dg_ref[...] += jnp.sum(dy * xn, axis=0, keepdims=True)
        dxn = dy * g
        dx_ref[...] = inv * (dxn - xn * jnp.mean(dxn * xn, axis=-1, keepdims=True))

    row = pl.BlockSpec((1, D), lambda i: (0, 0))
    blk = pl.BlockSpec((ts, D), lambda i: (i, 0))
    return pl.pallas_call(
        kern, name=name, grid=(S // ts,),
        in_specs=[blk, row, blk],
        out_specs=[pl.BlockSpec((1, 128), lambda i: (0, 0)), blk, row],
        out_shape=[jax.ShapeDtypeStruct((1, 128), F32), jax.ShapeDtypeStruct((S, D), F32),
                   jax.ShapeDtypeStruct((1, D), F32)],
        compiler_params=pltpu.CompilerParams(dimension_semantics=("arbitrary",), vmem_limit_bytes=VMEM_LIMIT),
    )(x, gain, target)


def _regroup_columns(w, outer, inner, name):
    rows = w.shape[0]

    def kern(i_ref, o_ref):
        o_ref[...] = i_ref[...]

    return pl.pallas_call(
        kern, name=name, grid=(outer, inner),
        in_specs=[pl.BlockSpec((rows, HEAD_DIM), lambda a, b: (0, a * inner + b))],
        out_specs=pl.BlockSpec((rows, HEAD_DIM), lambda a, b: (0, b * outer + a)),
        out_shape=jax.ShapeDtypeStruct(w.shape, w.dtype),
        compiler_params=pltpu.CompilerParams(
            dimension_semantics=("parallel", "parallel"), vmem_limit_bytes=VMEM_LIMIT),
    )(w)


def _heads_major(w, name):
    return _regroup_columns(w, 4, w.shape[1] // (4 * HEAD_DIM), name)


def _heads_minor(w, name):
    return _regroup_columns(w, w.shape[1] // (4 * HEAD_DIM), 4, name)


def _part(ref, j, rows=slice(None)):
    return ref[rows, j * HEAD_DIM:(j + 1) * HEAD_DIM]


def _chunk_scan(v, ric, reverse):
    n = v.shape[0]
    d = 1
    while d < A_CHUNK:
        if reverse:
            v = v + jnp.where(ric < A_CHUNK - d, pltpu.roll(v, n - d, 0), 0.0)
        else:
            v = v + jnp.where(ric >= d, pltpu.roll(v, d, 0), 0.0)
        d *= 2
    return v


def _hgrn2_gates(q_raw, f_logit, lbv, ric):
    qs = _silu(q_raw)
    sig = _sigmoid(f_logit)
    f = lbv + (1.0 - lbv) * sig
    log_f = jnp.log(f)
    kk = (1.0 - lbv) * _sigmoid(-f_logit)
    cum = _chunk_scan(log_f, ric, False)
    cl = cum + _chunk_scan(log_f, ric, True) - log_f
    e_cum = jnp.exp(cum)
    e_neg = jnp.exp(-cum)
    e_end = jnp.exp(cl - cum)
    dec = jnp.exp(cl)
    return qs, sig, f, kk, e_cum, e_neg, e_end, dec


def _same_chunk_mask():
    r = lax.broadcasted_iota(jnp.int32, (HGRN2_GROUP, HGRN2_GROUP), 0)
    c = lax.broadcasted_iota(jnp.int32, (HGRN2_GROUP, HGRN2_GROUP), 1)
    return (r // A_CHUNK == c // A_CHUNK) & (r >= c)


def _hgrn2_specs(S, D, TB, reverse):
    H = D // HEAD_DIM
    NB = S // TB
    pos = (lambda nb: NB - 1 - nb) if reverse else (lambda nb: nb)
    proj = pl.BlockSpec((TB, 4 * HEAD_DIM), lambda h, nb: (pos(nb), h))
    head = pl.BlockSpec((TB, HEAD_DIM), lambda h, nb: (pos(nb), h))
    vec = pl.BlockSpec((1, HEAD_DIM), lambda h, nb: (0, h))
    state = pl.BlockSpec((TB // A_CHUNK, None, HEAD_DIM, HEAD_DIM), lambda h, nb: (pos(nb), h, 0, 0))
    return H, NB, proj, head, vec, state


def _hgrn2_fwd(proj, lb, gain, name):
    S, D4 = proj.shape
    D = D4 // 4
    TB = _pick(S, 512, A_CHUNK)
    H, NB, pspec, head, vec, state = _hgrn2_specs(S, D, TB, False)
    NCB = TB // A_CHUNK

    def kern(p_ref, lb_ref, gain_ref, og_ref, oraw_ref, st_ref, a_s, b_s, k_s, v_s, dec_s, o_s, st_s):
        @pl.when(pl.program_id(1) == 0)
        def _():
            st_s[...] = jnp.zeros_like(st_s)

        ric = lax.broadcasted_iota(jnp.int32, (TB, HEAD_DIM), 0) % A_CHUNK
        qs, _, _, kk, e_cum, e_neg, e_end, dec = _hgrn2_gates(_part(p_ref, 0), _part(p_ref, 1), lb_ref[...], ric)
        a_s[...] = (qs * e_cum).astype(BF16)
        b_s[...] = (kk * e_neg).astype(BF16)
        k_s[...] = (kk * e_end).astype(BF16)
        v_s[...] = _part(p_ref, 2).astype(BF16)
        dec_s[...] = dec

        same_chunk = _same_chunk_mask()
        for gi in range(TB // HGRN2_GROUP):
            rows = pl.ds(gi * HGRN2_GROUP, HGRN2_GROUP)
            p = jnp.where(same_chunk, _dot_nt(a_s[rows, :], b_s[rows, :]), 0.0).astype(BF16)
            o_s[rows, :] = _dot_nn(p, v_s[rows, :])

        def chunks(it, st):
            for u in range(HGRN2_UNROLL):
                ci = it * HGRN2_UNROLL + u
                r = pl.multiple_of(ci * A_CHUNK, A_CHUNK)
                rows = pl.ds(r, A_CHUNK)
                st_bf = st.astype(BF16)
                st_ref[ci] = st_bf
                o_s[rows, :] += _dot_nt(a_s[rows, :], st_bf)
                st = dec_s[pl.ds(r, 1), :] * st + _dot_tn(v_s[rows, :], k_s[rows, :])
            return st

        st_s[...] = lax.fori_loop(0, NCB // HGRN2_UNROLL, chunks, st_s[...])
        o = o_s[...]
        oraw_ref[...] = o
        on = o * lax.rsqrt(jnp.mean(o * o, axis=-1, keepdims=True) + NORM_EPS)
        og_ref[...] = ((on * gain_ref[...]) * _silu(_part(p_ref, 3))).astype(og_ref.dtype)

    tb_bf = pltpu.VMEM((TB, HEAD_DIM), BF16)
    tb_f = pltpu.VMEM((TB, HEAD_DIM), F32)
    return pl.pallas_call(
        kern, name=name, grid=(H, NB),
        in_specs=[pspec, vec, vec],
        out_specs=[head, head, state],
        out_shape=[jax.ShapeDtypeStruct((S, D), BF16), jax.ShapeDtypeStruct((S, D), F32),
                   jax.ShapeDtypeStruct((S // A_CHUNK, H, HEAD_DIM, HEAD_DIM), BF16)],
        scratch_shapes=[tb_bf, tb_bf, tb_bf, tb_bf, tb_f, tb_f, pltpu.VMEM((HEAD_DIM, HEAD_DIM), F32)],
        compiler_params=pltpu.CompilerParams(
            dimension_semantics=("parallel", "arbitrary"), vmem_limit_bytes=VMEM_LIMIT),
    )(proj, lb, gain)


def _hgrn2_bwd(proj, lb, gain, oraw, states, dog, name):
    S, D4 = proj.shape
    D = D4 // 4
    TB = _pick(S, 512, A_CHUNK)
    H, NB, pspec, head, vec, state = _hgrn2_specs(S, D, TB, True)
    NCB = TB // A_CHUNK

    def kern(p_ref, lb_ref, gain_ref, oraw_ref, st_ref, dog_ref, dp_ref, dlb_ref, dgain_ref,
             a_s, b_s, k_s, v_s, do_s, dec_s, da_s, db_s, dk_s, dv_s, ddec_s, dst_s):
        @pl.when(pl.program_id(1) == 0)
        def _():
            dst_s[...] = jnp.zeros_like(dst_s)
            dlb_ref[...] = jnp.zeros_like(dlb_ref)
            dgain_ref[...] = jnp.zeros_like(dgain_ref)

        ric = lax.broadcasted_iota(jnp.int32, (TB, HEAD_DIM), 0) % A_CHUNK
        lbv = lb_ref[...]
        q_raw = _part(p_ref, 0)
        qs, sig, f, kk, e_cum, e_neg, e_end, dec = _hgrn2_gates(q_raw, _part(p_ref, 1), lbv, ric)
        a32, b32, k32 = qs * e_cum, kk * e_neg, kk * e_end
        a_s[...] = a32.astype(BF16)
        b_s[...] = b32.astype(BF16)
        k_s[...] = k32.astype(BF16)
        v_s[...] = _part(p_ref, 2).astype(BF16)
        dec_s[...] = dec

        o = oraw_ref[...]
        gain_v = gain_ref[...]
        g_raw = _part(p_ref, 3)
        rinv = lax.rsqrt(jnp.mean(o * o, axis=-1, keepdims=True) + NORM_EPS)
        on = o * rinv
        dog_v = dog_ref[...].astype(F32)
        dp_ref[:, 3 * HEAD_DIM:4 * HEAD_DIM] = (dog_v * (on * gain_v) * _dsilu(g_raw)).astype(dp_ref.dtype)
        dog2 = dog_v * _silu(g_raw)
        dgain_ref[...] += jnp.sum(dog2 * on, axis=0, keepdims=True)
        don = dog2 * gain_v
        do_s[...] = (rinv * (don - on * jnp.mean(don * on, axis=-1, keepdims=True))).astype(BF16)

        same_chunk = _same_chunk_mask()
        for gi in range(TB // HGRN2_GROUP):
            rows = pl.ds(gi * HGRN2_GROUP, HGRN2_GROUP)
            a, b, do = a_s[rows, :], b_s[rows, :], do_s[rows, :]
            p = jnp.where(same_chunk, _dot_nt(a, b), 0.0).astype(BF16)
            dp = jnp.where(same_chunk, _dot_nt(do, v_s[rows, :]), 0.0).astype(BF16)
            dv_s[rows, :] = _dot_tn(p, do)
            da_s[rows, :] = _dot_nn(dp, b)
            db_s[rows, :] = _dot_tn(dp, a)

        def chunks(it, dst):
            for u in range(HGRN2_UNROLL):
                ci = NCB - 1 - (it * HGRN2_UNROLL + u)
                r = pl.multiple_of(ci * A_CHUNK, A_CHUNK)
                rows = pl.ds(r, A_CHUNK)
                do = do_s[rows, :]
                st_prev = st_ref[ci]
                dst_bf = dst.astype(BF16)
                dv_s[rows, :] += _dot_nt(k_s[rows, :], dst_bf)
                da_s[rows, :] += _dot_nn(do, st_prev)
                dk_s[rows, :] = _dot_nn(v_s[rows, :], dst_bf)
                ddec = jnp.sum(dst * st_prev.astype(F32), axis=0, keepdims=True)
                ddec_s[rows, :] = jnp.broadcast_to(ddec, (A_CHUNK, HEAD_DIM))
                dst = dec_s[pl.ds(r, 1), :] * dst + _dot_tn(do, a_s[rows, :])
            return dst

        dst_s[...] = lax.fori_loop(0, NCB // HGRN2_UNROLL, chunks, dst_s[...])

        da, db, dk = da_s[...], db_s[...], dk_s[...]
        dqs = da * e_cum
        dkk = db * e_neg + dk * e_end
        w = dk * k32
        dlog_f = (_chunk_scan(da * a32 - db * b32, ric, True) + (_chunk_scan(w, ric, False) - w)
                  + ddec_s[...] * dec)
        dfg = dlog_f / f - dkk
        dlb_ref[...] += jnp.sum(dfg * (1.0 - sig), axis=0, keepdims=True)
        dp_ref[:, 0:HEAD_DIM] = (dqs * _dsilu(q_raw)).astype(dp_ref.dtype)
        dp_ref[:, HEAD_DIM:2 * HEAD_DIM] = (dfg * (1.0 - lbv) * sig * (1.0 - sig)).astype(dp_ref.dtype)
        dp_ref[:, 2 * HEAD_DIM:3 * HEAD_DIM] = dv_s[...].astype(dp_ref.dtype)

    tb_bf = pltpu.VMEM((TB, HEAD_DIM), BF16)
    tb_f = pltpu.VMEM((TB, HEAD_DIM), F32)
    vec_shape = jax.ShapeDtypeStruct((1, D), F32)
    return pl.pallas_call(
        kern, name=name, grid=(H, NB),
        in_specs=[pspec, vec, vec, head, state, head],
        out_specs=[pspec, vec, vec],
        out_shape=[jax.ShapeDtypeStruct((S, D4), BF16), vec_shape, vec_shape],
        scratch_shapes=[tb_bf, tb_bf, tb_bf, tb_bf, tb_bf, tb_f, tb_f, tb_f, tb_f, tb_f, tb_f,
                        pltpu.VMEM((HEAD_DIM, HEAD_DIM), F32)],
        compiler_params=pltpu.CompilerParams(
            dimension_semantics=("parallel", "arbitrary"), vmem_limit_bytes=VMEM_LIMIT),
    )(proj, lb, gain, oraw, states, dog)


def _with_ones(tri):
    return jnp.concatenate([tri, jnp.ones_like(tri)], axis=1)


def _stack(ref, G):
    return jnp.concatenate([ref[:, g * HEAD_DIM:(g + 1) * HEAD_DIM] for g in range(G)], axis=0)


def _unstack(v, G):
    return jnp.concatenate([v[g * SB_TILE:(g + 1) * SB_TILE, :] for g in range(G)], axis=1)


def _sb_specs(S, D, KVH):
    G = D // HEAD_DIM // KVH
    qblk = pl.BlockSpec((SB_TILE, G * HEAD_DIM), lambda h, qi: (qi, h))
    kblk = pl.BlockSpec((S, HEAD_DIM), lambda h, qi: (0, h))
    vblk = pl.BlockSpec((S, HEAD_DIM), lambda h, qi: (0, KVH + h))
    return G, qblk, kblk, vblk


def _tile_index(qi, j):
    return qi * (qi + 1) // 2 + j


def _sb_fwd(q, kv, name):
    S, D = q.shape
    KVH = kv.shape[1] // (2 * HEAD_DIM)
    G, qblk, kblk, vblk = _sb_specs(S, D, KVH)
    R = G * SB_TILE
    NQ = S // SB_TILE
    scale = HEAD_DIM ** -0.5

    def kern(q_ref, k_ref, v_ref, o_ref, w_hbm, b_hbm, acc_s, run_s, w_stage, b_stage, sems):
        head, qi = pl.program_id(0), pl.program_id(1)
        qs = _stack(q_ref, G)
        row = lax.broadcasted_iota(jnp.int32, (R, SB_TILE), 0) % SB_TILE
        colm = lax.broadcasted_iota(jnp.int32, (R, SB_TILE), 1)
        mask = colm < row
        ti = lax.broadcasted_iota(jnp.int32, (SB_TILE, SB_TILE), 0)
        tj = lax.broadcasted_iota(jnp.int32, (SB_TILE, SB_TILE), 1)
        after = _with_ones((ti > tj).astype(BF16))
        acc_s[...] = jnp.zeros_like(acc_s)
        run_s[...] = jnp.zeros_like(run_s)

        def save(slot, j):
            idx = _tile_index(qi, j)
            return (pltpu.make_async_copy(w_stage.at[slot], w_hbm.at[head, idx], sems.at[0, slot]),
                    pltpu.make_async_copy(b_stage.at[slot], b_hbm.at[head, idx], sems.at[1, slot]))

        def tile(j, slot, masked):
            ks = pl.ds(pl.multiple_of(j * SB_TILE, SB_TILE), SB_TILE)
            kj, vj = k_ref[ks, :], v_ref[ks, :]
            z = _dot_nt(qs, kj) * scale
            t = jnp.exp(-jnp.abs(z))
            one_t = 1.0 + t
            log_beta = jnp.minimum(z, 0.0) - jnp.log(one_t)
            log_rest = log_beta - z
            if masked:
                log_rest = jnp.where(mask, log_rest, 0.0)
            sums = _dot_nn(log_rest.astype(BF16), after)
            between = sums[:, :SB_TILE] + run_s[...]
            w = jnp.exp(log_beta + between)
            if masked:
                w = jnp.where(mask, w, 0.0)
            w_bf = w.astype(BF16)
            acc_s[...] += _dot_nn(w_bf, vj)
            run_s[...] += sums[:, SB_TILE:]
            w_stage[slot] = w_bf
            b_stage[slot] = (jnp.where(z >= 0.0, 1.0, t) * pl.reciprocal(one_t, approx=True)).astype(BF16)
            for cp in save(slot, j):
                cp.start()

        tile(qi, 0, True)

        def body(it, carry):
            slot = (it + 1) % 2

            @pl.when(it >= 1)
            def _():
                for cp in save(slot, 0):
                    cp.wait()

            tile(qi - 1 - it, slot, False)
            return carry

        lax.fori_loop(0, qi, body, 0)
        o_ref[...] = _unstack(acc_s[...], G).astype(o_ref.dtype)
        for cp in save(qi % 2, 0):
            cp.wait()

        @pl.when(qi >= 1)
        def _():
            for cp in save((qi + 1) % 2, 0):
                cp.wait()

    tiles = jax.ShapeDtypeStruct((KVH, NQ * (NQ + 1) // 2, R, SB_TILE), BF16)
    stage = pltpu.VMEM((2, R, SB_TILE), BF16)
    return pl.pallas_call(
        kern, name=name, grid=(KVH, NQ),
        in_specs=[qblk, kblk, vblk],
        out_specs=[qblk, pl.BlockSpec(memory_space=pl.ANY), pl.BlockSpec(memory_space=pl.ANY)],
        out_shape=[jax.ShapeDtypeStruct((S, D), BF16), tiles, tiles],
        scratch_shapes=[pltpu.VMEM((R, HEAD_DIM), F32), pltpu.VMEM((R, SB_TILE), F32), stage, stage,
                        pltpu.SemaphoreType.DMA((2, 2))],
        compiler_params=pltpu.CompilerParams(
            dimension_semantics=("parallel", "arbitrary"), vmem_limit_bytes=VMEM_LIMIT),
    )(q, kv, kv)


def _sb_bwd(q, kv, w_tiles, b_tiles, do, after, name):
    S, D = q.shape
    KVH = kv.shape[1] // (2 * HEAD_DIM)
    G, qblk, kblk, vblk = _sb_specs(S, D, KVH)
    R = G * SB_TILE
    scale = HEAD_DIM ** -0.5

    def kern(q_ref, k_ref, v_ref, w_hbm, b_hbm, do_ref, after_ref, dq_ref, dk_ref, dv_ref,
             dq_s, esum_s, w_stage, b_stage, sems):
        head, qi = pl.program_id(0), pl.program_id(1)

        @pl.when(qi == 0)
        def _():
            dk_ref[...] = jnp.zeros_like(dk_ref)
            dv_ref[...] = jnp.zeros_like(dv_ref)

        qs = _stack(q_ref, G)
        dos = _stack(do_ref, G)
        row = lax.broadcasted_iota(jnp.int32, (R, SB_TILE), 0) % SB_TILE
        colm = lax.broadcasted_iota(jnp.int32, (R, SB_TILE), 1)
        mask = colm < row
        ti = lax.broadcasted_iota(jnp.int32, (SB_TILE, SB_TILE), 0)
        tj = lax.broadcasted_iota(jnp.int32, (SB_TILE, SB_TILE), 1)
        before = (ti < tj).astype(BF16)
        dq_s[...] = jnp.zeros_like(dq_s)
        esum_s[...] = jnp.zeros_like(esum_s)

        def fetch(slot, j):
            idx = _tile_index(qi, j)
            return (pltpu.make_async_copy(w_hbm.at[head, idx], w_stage.at[slot], sems.at[0, slot]),
                    pltpu.make_async_copy(b_hbm.at[head, idx], b_stage.at[slot], sems.at[1, slot]))

        def tile(j, slot, masked):
            for cp in fetch(slot, j):
                cp.wait()
            ks = pl.ds(pl.multiple_of(j * SB_TILE, SB_TILE), SB_TILE)
            kj, vj = k_ref[ks, :], v_ref[ks, :]
            w_bf = w_stage[slot]
            beta = b_stage[slot].astype(F32)
            e = _dot_nt(dos, vj) * w_bf.astype(F32)
            e_before = esum_s[...] + _dot_nn(e.astype(BF16), before)
            dz = e - beta * (e + e_before)
            if masked:
                dz = jnp.where(mask, dz, 0.0)
            dzs = (dz * scale).astype(BF16)
            dq_s[...] += _dot_nn(dzs, kj)
            dk_ref[ks, :] += _dot_tn(dzs, qs)
            dv_ref[ks, :] += _dot_tn(w_bf, dos)
            esum_s[...] += jnp.sum(e, axis=1, keepdims=True)

        for cp in fetch(0, 0):
            cp.start()

        def body(j, carry):
            for cp in fetch((j + 1) % 2, j + 1):
                cp.start()
            tile(j, j % 2, False)
            return carry

        lax.fori_loop(0, qi, body, 0)
        tile(qi, qi % 2, True)
        dq_ref[...] = _unstack(dq_s[...], G).astype(dq_ref.dtype)

    kvout = pl.BlockSpec((S, HEAD_DIM), lambda h, qi: (0, h))
    stage = pltpu.VMEM((2, R, SB_TILE), BF16)
    hbm = pl.BlockSpec(memory_space=pl.ANY)
    return pl.pallas_call(
        kern, name=name, grid=(KVH, S // SB_TILE),
        in_specs=[qblk, kblk, vblk, hbm, hbm, qblk, pl.BlockSpec(after.shape, lambda h, qi: (0, 0))],
        out_specs=[qblk, kvout, kvout],
        out_shape=[jax.ShapeDtypeStruct((S, D), BF16), jax.ShapeDtypeStruct((S, KVH * HEAD_DIM), F32),
                   jax.ShapeDtypeStruct((S, KVH * HEAD_DIM), F32)],
        scratch_shapes=[pltpu.VMEM((R, HEAD_DIM), F32), pltpu.VMEM((R, SB_TILE), F32), stage, stage,
                        pltpu.SemaphoreType.DMA((2, 2))],
        compiler_params=pltpu.CompilerParams(
            dimension_semantics=("parallel", "arbitrary"), vmem_limit_bytes=VMEM_LIMIT),
    )(q, kv, kv, w_tiles, b_tiles, do, after)


def _adamw(parts, w, m, v, name, layer=None, filled=None):
    shape = w.shape
    L = 1 if layer is None else shape[0]
    l = 0 if layer is None else layer
    C = shape[-1]
    R = w.size // (C * L)
    P = parts.shape[0]
    parts3 = parts.reshape(P, R, C)
    w3, m3, v3 = w.reshape(L, R, C), m.reshape(L, R, C), v.reshape(L, R, C)
    tr = _pick(R, max(8, (1 << 18) // C), 16)

    def kern(p_ref, w_ref, m_ref, v_ref, *rest):
        g_ref, d_ref, nm_ref, nv_ref = rest[-4:]
        g = p_ref[0].astype(F32)
        for i in range(1, P):
            g = g + p_ref[i].astype(F32)
        nm = ADAM_B1 * m_ref[...] + (1.0 - ADAM_B1) * g
        nv = ADAM_B2 * v_ref[...] + (1.0 - ADAM_B2) * (g * g)
        m_hat = nm / (1.0 - ADAM_B1 ** ADAM_STEP)
        v_hat = nv / (1.0 - ADAM_B2 ** ADAM_STEP)
        g_ref[...] = g
        d_ref[...] = -ADAM_LR * (m_hat / (jnp.sqrt(v_hat) + ADAM_EPS) + ADAM_WD * w_ref[...])
        nm_ref[...] = nm
        nv_ref[...] = nv

    blk = pl.BlockSpec((None, tr, C), lambda i: (l, i, 0))
    out = jax.ShapeDtypeStruct((L, R, C), F32)
    extra = [] if filled is None else [f.reshape(L, R, C) for f in filled]
    res = pl.pallas_call(
        kern, name=name, grid=(R // tr,),
        in_specs=[pl.BlockSpec((P, tr, C), lambda i: (0, i, 0)), blk, blk, blk]
        + [pl.BlockSpec(memory_space=pl.ANY)] * len(extra),
        out_specs=[blk, blk, blk, blk], out_shape=[out, out, out, out],
        input_output_aliases={4 + j: j for j in range(len(extra))},
        compiler_params=pltpu.CompilerParams(dimension_semantics=("parallel",), vmem_limit_bytes=VMEM_LIMIT),
    )(parts3, w3, m3, v3, *extra)
    return tuple(r.reshape(shape) for r in res)


def _lower_bound(logits):
    return jnp.cumsum(jax.nn.softmax(logits.astype(F32), axis=0), axis=0)[0:1]


def _pad_rows(a, rows):
    return jnp.zeros((rows, a.shape[1]), a.dtype).at[:a.shape[0]].set(a)


def kernel(x, c, ada_w, ada_b, norm_mix, norm_mlp, a_w_in, a_lb_logits, a_out_gain, a_w_out, kv_ada_w, kv_ada_b, kv_norm, w_kv, b_w_q, b_w_out, mlp_w1, mlp_w2, final_norm, loss_target, m_ada_w, m_ada_b, m_norm_mix, m_norm_mlp, m_a_w_in, m_a_lb_logits, m_a_out_gain, m_a_w_out, m_kv_ada_w, m_kv_ada_b, m_kv_norm, m_w_kv, m_b_w_q, m_b_w_out, m_mlp_w1, m_mlp_w2, m_final_norm, v_ada_w, v_ada_b, v_norm_mix, v_norm_mlp, v_a_w_in, v_a_lb_logits, v_a_out_gain, v_a_w_out, v_kv_ada_w, v_kv_ada_b, v_kv_norm, v_w_kv, v_b_w_q, v_b_w_out, v_mlp_w1, v_mlp_w2, v_final_norm):
    S, D = x.shape[1], x.shape[2]
    assert x.shape[0] == 1 and ada_w.shape[0] == 2 and a_w_in.shape[0] == 1 and b_w_q.shape[0] == 1
    me = 4 * lax.axis_index("x") + 2 * lax.axis_index("y") + lax.axis_index("c")
    dl = D // N_DEV
    na = ada_w.shape[2]
    nk = kv_ada_w.shape[1]

    small = jnp.concatenate([c.reshape(1, D), a_lb_logits.reshape(1, 2 * dl), a_out_gain.reshape(1, dl)], axis=1)
    small_all = _all_gather(small, "gather_small")[:, 0, :]
    c_all = small_all[:, :D]
    lb_logits = small_all[:, D:D + 2 * dl].reshape(N_DEV, 2, dl).transpose(1, 0, 2).reshape(2, D)
    out_gain = small_all[:, D + 2 * dl:].reshape(1, D)

    c_act = jax.nn.silu(c_all)
    c_act_rows = _pad_rows(c_act.astype(BF16), 128)
    mod_cols = jnp.concatenate([
        _matmul(c_act_rows, ada_w[0].astype(BF16), "nn", (F32,), "ada0")[0][:N_DEV],
        _matmul(c_act_rows, ada_w[1].astype(BF16), "nn", (F32,), "ada1")[0][:N_DEV],
        _matmul(c_act_rows, kv_ada_w.astype(BF16), "nn", (F32,), "ada_kv")[0][:N_DEV]], axis=1)
    mod_all = _all_gather(mod_cols, "gather_mod")
    mod_mine = lax.dynamic_index_in_dim(mod_all, me, axis=1, keepdims=False)
    mod0 = mod_mine[:, :na].reshape(1, 6 * D) + ada_b[0:1]
    mod1 = mod_mine[:, na:2 * na].reshape(1, 6 * D) + ada_b[1:2]
    modk = mod_mine[:, 2 * na:].reshape(1, 2 * D) + kv_ada_b.reshape(1, 2 * D)
    sh1a, sc1a, g1a, sh2a, sc2a, g2a = jnp.split(mod0, 6, axis=1)
    sh1b, sc1b, g1b, sh2b, sc2b, g2b = jnp.split(mod1, 6, axis=1)
    kv_sh, kv_sc = jnp.split(modk, 2, axis=1)
    nmix0, nmix1, nmlp0, nmlp1 = norm_mix[0:1], norm_mix[1:2], norm_mlp[0:1], norm_mlp[1:2]
    kvn = kv_norm.reshape(1, D)
    lb, lb_vjp = jax.vjp(_lower_bound, lb_logits)

    w_in = _all_gather(a_w_in[0].astype(BF16), "gather_a_w_in", axis=1)
    w_names = ["a_w_out", "mlp0_w1", "mlp0_w2", "w_kv", "b_w_q", "b_w_out", "mlp1_w1", "mlp1_w2"]
    w_shards = [a_w_out[0], mlp_w1[0], mlp_w2[0], w_kv, b_w_q[0], b_w_out[0], mlp_w1[1], mlp_w2[1]]
    w_axes = [0, 1, 0, 0, 0, 0, 1, 0]
    w_shards = [s.astype(BF16) for s in w_shards]
    w_routes = [_gather_route(s, ax) for s, ax in zip(w_shards, w_axes)]
    w_handles, w_token = _copies_start(w_shards, w_routes, w_in, "gather_weights_start")

    def weight(key, after):
        i = w_names.index(key)
        return _copies_wait(w_handles[i], w_routes[i], after, "gather_" + key + "_wait")

    x0 = x[0]
    h1 = _norm_mod_fwd(x0, nmix0, sc1a + w_token[0:1, 0:1], sh1a, "a_in_norm")
    w_in = _heads_major(w_in, "a_w_in_by_head")
    (proj,) = _matmul(h1, w_in, "nn", (F32,), "a_in_mm")
    og, oraw, states = _hgrn2_fwd(proj, lb, out_gain, "hgrn2_fwd")
    w_aout = weight("a_w_out", og)
    x1, y1, h2 = _matmul(og, w_aout, "nn", (F32, BF16, BF16), "a_out_mm", _residual_epilogue, (x0,),
                         (g1a, nmlp0, sc2a, sh2a), whole_rows=True)
    w1a = weight("mlp0_w1", h2)
    u0, a0 = _matmul(h2, w1a, "nn", (BF16, BF16), "mlp0_up_mm", _relu2_epilogue)
    w2a = weight("mlp0_w2", u0)
    x2, y2, hk, h3 = _matmul(u0, w2a, "nn", (F32, BF16, BF16, BF16), "mlp0_down_mm", _residual_epilogue, (x1,),
                             (g2a, kvn, kv_sc, kv_sh, nmix1, sc1b, sh1b), whole_rows=True)

    w_kvf = weight("w_kv", hk)
    (kv,) = _matmul(hk, w_kvf, "nn", (BF16,), "kv_mm")
    w_q = weight("b_w_q", h3)
    (q,) = _matmul(h3, w_q, "nn", (BF16,), "b_q_mm")
    o, w_tiles, b_tiles = _sb_fwd(q, kv, "attn_fwd")
    w_bout = weight("b_w_out", o)
    x3, y3, h4 = _matmul(o, w_bout, "nn", (F32, BF16, BF16), "b_out_mm", _residual_epilogue, (x2,),
                         (g1b, nmlp1, sc2b, sh2b), whole_rows=True)
    w1b = weight("mlp1_w1", h4)
    u1, a1 = _matmul(h4, w1b, "nn", (BF16, BF16), "mlp1_up_mm", _relu2_epilogue)
    w2b = weight("mlp1_w2", u1)
    x4, y4 = _matmul(u1, w2b, "nn", (F32, BF16), "mlp1_down_mm", _residual_epilogue, (x3,), (g2b,))

    loss_vec, d4, d_final = _final_loss(x4, final_norm.reshape(1, D), loss_target[0], "final_loss")
    loss = lax.psum(loss_vec[0, 0], MESH_AXES)

    sent = {}

    def send(key, g, axis):
        route = _scatter_route(g, axis)
        (handle,), token = _copies_start([g], [route], None, "scatter_" + key + "_start")
        sent[key] = (handle, route)
        return token

    def behind(vec, *tokens):
        for token in tokens:
            vec = vec + token[0:1, 0:1]
        return vec

    def mlp_bwd(tag, d, y, g, u, a, w2, w1, h, x_in, gain, sc):
        dy, dg = _gate_bwd(d, y, g, tag + "_down_dgate")
        (dw2,) = _matmul(u, dy, "tn", (BF16,), tag + "_down_dw")
        token2 = send(tag + "_w2", dw2, 0)
        (dz,) = _matmul(dy, w2, "nt", (BF16,), tag + "_down_dz", lambda acc, a_tile: (acc * (2.0 * a_tile),), (a,))
        (dw1,) = _matmul(h, dz, "tn", (BF16,), tag + "_up_dw")
        token1 = send(tag + "_w1", dw1, 1)
        (dh,) = _matmul(dz, w1, "nt", (F32,), tag + "_up_dh")
        d_out, dgain, dsc, dsh = _norm_mod_bwd(x_in, gain, behind(sc, token2, token1), dh, d, tag + "_up_dnorm")
        return d_out, dg, dgain, dsc, dsh

    d3, dg2b, dnmlp1, dsc2b, dsh2b = mlp_bwd("mlp1", d4, y4, g2b, u1, a1, w2b, w1b, h4, x3, nmlp1, sc2b)

    dy3, dg1b = _gate_bwd(d3, y3, g1b, "b_out_dgate")
    (dw_bout,) = _matmul(o, dy3, "tn", (BF16,), "b_out_dw")
    token = send("b_w_out", dw_bout, 0)
    (do,) = _matmul(dy3, w_bout, "nt", (BF16,), "b_out_du")
    dq, dk, dv = _sb_bwd(q, kv, w_tiles, b_tiles, do, token, "attn_bwd")
    dkv = jnp.concatenate([dk, dv], axis=1).astype(BF16)
    (dw_q,) = _matmul(h3, dq, "tn", (BF16,), "b_q_dw")
    token = send("b_w_q", dw_q, 0)
    (dh3,) = _matmul(dq, w_q, "nt", (F32,), "b_q_dh")
    d2, dnmix1, dsc1b, dsh1b = _norm_mod_bwd(x2, nmix1, behind(sc1b, token), dh3, d3, "b_q_dnorm")
    (dw_kv,) = _matmul(hk, dkv, "tn", (BF16,), "kv_dw")
    token = send("w_kv", dw_kv, 0)
    (dhk,) = _matmul(dkv, w_kvf, "nt", (F32,), "kv_dh")
    d2, dkvn, dkv_sc, dkv_sh = _norm_mod_bwd(x2, kvn, behind(kv_sc, token), dhk, d2, "kv_dnorm")

    d1, dg2a, dnmlp0, dsc2a, dsh2a = mlp_bwd("mlp0", d2, y2, g2a, u0, a0, w2a, w1a, h2, x1, nmlp0, sc2a)

    dy1, dg1a = _gate_bwd(d1, y1, g1a, "a_out_dgate")
    (dw_aout,) = _matmul(og, dy1, "tn", (BF16,), "a_out_dw")
    token = send("a_w_out", dw_aout, 0)
    (dog,) = _matmul(dy1, w_aout, "nt", (BF16,), "a_out_du")
    dproj, dlb, d_out_gain = _hgrn2_bwd(proj, behind(lb, token), out_gain, oraw, states, dog, "hgrn2_bwd")
    (dw_in,) = _matmul(h1, dproj, "tn", (BF16,), "a_in_dw")
    token = send("a_w_in", _heads_minor(dw_in, "a_w_in_grad_by_part"), 1)
    (dh1,) = _matmul(dproj, w_in, "nt", (F32,), "a_in_dh")
    d0, dnmix0, dsc1a, dsh1a = _norm_mod_bwd(x0, nmix0, behind(sc1a, token), dh1, d1, "a_in_dnorm")
    (d_lb_logits,) = lb_vjp(dlb)

    dmod0 = jnp.concatenate([dsh1a, dsc1a, dg1a, dsh2a, dsc2a, dg2a], axis=1)
    dmod1 = jnp.concatenate([dsh1b, dsc1b, dg1b, dsh2b, dsc2b, dg2b], axis=1)
    dmodk = jnp.concatenate([dkv_sh, dkv_sc], axis=1)
    pieces = [dmod0, dmod1, dmodk, dnmix0, dnmix1, dnmlp0, dnmlp1, dkvn, d_final,
              d_lb_logits.reshape(1, 2 * D), d_out_gain]
    widths = [p.shape[1] for p in pieces]
    offs = [sum(widths[:i]) for i in range(len(widths))]
    part_all = _all_gather(jnp.concatenate(pieces, axis=1), "gather_dsmall")[:, 0, :]
    take = lambda i: part_all[:, offs[i]:offs[i] + widths[i]]
    dmod0_all, dmod1_all, dmodk_all = take(0), take(1), take(2)

    outs = {}

    def update(key, parts, w, m, v, **kw):
        outs[key] = _adamw(parts, w, m, v, "adamw_" + key + ("_%d" % kw["layer"] if "layer" in kw else ""), **kw)

    c_act_cols = _pad_rows(c_act.astype(BF16), 128).T
    my_cols = lambda a, n: _pad_rows(lax.dynamic_slice_in_dim(a, me * n, n, axis=1).astype(BF16), 128)
    g_ada0 = _matmul(c_act_cols, my_cols(dmod0_all, na), "nn", (F32,), "dada0")[0]
    g_ada1 = _matmul(c_act_cols, my_cols(dmod1_all, na), "nn", (F32,), "dada1")[0]
    g_adak = _matmul(c_act_cols, my_cols(dmodk_all, nk), "nn", (F32,), "dada_kv")[0]
    update("ada_w", g_ada1[None], ada_w, m_ada_w, v_ada_w, layer=1)
    update("ada_w", g_ada0[None], ada_w, m_ada_w, v_ada_w, layer=0, filled=outs["ada_w"])
    update("kv_ada_w", g_adak[None], kv_ada_w, m_kv_ada_w, v_kv_ada_w)
    update("ada_b", jnp.stack([dmod0_all, dmod1_all], axis=1), ada_b, m_ada_b, v_ada_b)
    update("kv_ada_b", dmodk_all, kv_ada_b, m_kv_ada_b, v_kv_ada_b)
    update("norm_mix", jnp.stack([take(3), take(4)], axis=1), norm_mix, m_norm_mix, v_norm_mix)
    update("norm_mlp", jnp.stack([take(5), take(6)], axis=1), norm_mlp, m_norm_mlp, v_norm_mlp)
    update("kv_norm", take(7), kv_norm, m_kv_norm, v_kv_norm)
    update("final_norm", take(8), final_norm, m_final_norm, v_final_norm)
    d_lb_all = take(9).reshape(N_DEV, 2, D)
    update("a_lb_logits", lax.dynamic_slice_in_dim(d_lb_all, me * dl, dl, axis=2), a_lb_logits, m_a_lb_logits,
           v_a_lb_logits)
    update("a_out_gain", lax.dynamic_slice_in_dim(take(10), me * dl, dl, axis=1)[:, None, :], a_out_gain,
           m_a_out_gain, v_a_out_gain)

    def landed(key, after):
        handle, route = sent[key]
        return _copies_wait(handle, route, after, "scatter_" + key + "_wait")

    update("mlp_w2", landed("mlp1_w2", d0), mlp_w2, m_mlp_w2, v_mlp_w2, layer=1)
    update("mlp_w1", landed("mlp1_w1", d0), mlp_w1, m_mlp_w1, v_mlp_w1, layer=1)
    update("b_w_out", landed("b_w_out", d0), b_w_out, m_b_w_out, v_b_w_out)
    update("b_w_q", landed("b_w_q", d0), b_w_q, m_b_w_q, v_b_w_q)
    update("w_kv", landed("w_kv", d0), w_kv, m_w_kv, v_w_kv)
    update("mlp_w2", landed("mlp0_w2", d0), mlp_w2, m_mlp_w2, v_mlp_w2, layer=0, filled=outs["mlp_w2"])
    update("mlp_w1", landed("mlp0_w1", d0), mlp_w1, m_mlp_w1, v_mlp_w1, layer=0, filled=outs["mlp_w1"])
    update("a_w_out", landed("a_w_out", d0), a_w_out, m_a_w_out, v_a_w_out)
    update("a_w_in", landed("a_w_in", outs["mlp_w1"][0]), a_w_in, m_a_w_in, v_a_w_in)

    names = ["ada_w", "ada_b", "norm_mix", "norm_mlp", "a_w_in", "a_lb_logits", "a_out_gain", "a_w_out", "kv_ada_w",
             "kv_ada_b", "kv_norm", "w_kv", "b_w_q", "b_w_out", "mlp_w1", "mlp_w2", "final_norm"]
    result = [loss, d0[None]]
    for field in range(4):
        result += [outs[n][field] for n in names]
    return tuple(result)
```

```python
import jax
import jax.numpy as jnp
from jax import lax
from jax.experimental import pallas as pl
from jax.experimental.pallas import tpu as pltpu

F32 = jnp.float32
BF16 = jnp.bfloat16

N_DEV = 8
MESH_AXES = ("x", "y", "c")
HEAD_DIM = 128
A_CHUNK = 16
HGRN2_UNROLL = 8
HGRN2_GROUP = 128
SB_TILE = 256
NORM_EPS = 1e-6
ADAM_LR = 0.001
ADAM_B1 = 0.9
ADAM_B2 = 0.999
ADAM_EPS = 1e-08
ADAM_WD = 0.01
ADAM_STEP = 10
VMEM_LIMIT = 56 * 1024 * 1024
MATMUL_VMEM = 40 * 1024 * 1024


def _pick(dim, target, align):
    t = (min(dim, target) // align) * align
    while t >= align:
        if dim % t == 0:
            return t
        t -= align
    return dim


def _sigmoid(x):
    return 1.0 / (1.0 + jnp.exp(-x))


def _silu(x):
    return x * _sigmoid(x)


def _dsilu(x):
    s = _sigmoid(x)
    return s * (1.0 + x * (1.0 - s))


def _dot(a, b, dims):
    return lax.dot_general(a, b, (dims, ((), ())), preferred_element_type=F32)


def _dot_nn(a, b):
    return _dot(a, b, ((1,), (0,)))


def _dot_nt(a, b):
    return _dot(a, b, ((1,), (1,)))


def _dot_tn(a, b):
    return _dot(a, b, ((0,), (0,)))


def _mesh_pos():
    return lax.axis_index("x"), lax.axis_index("y"), lax.axis_index("c")


def _flip(v, d):
    return 1 - v if d else v


def _all_gather(x, name, axis=None):
    if axis is None:
        out_shape, place = (N_DEV,) + x.shape, lambda d: (d,)
    else:
        route = _gather_route(x, axis)
        out_shape, place = route.land_shape, route.dst_slice

    def body(x_ref, out_ref, send_sems, recv_sems, local_sem):
        x, y, c = _mesh_pos()
        me, sibling = (x, y, c), (x, y, 1 - c)
        chips = [(1 - x, y), (x, 1 - y), (1 - x, 1 - y)]

        def rows(px, py, pc):
            return out_ref.at[place(4 * px + 2 * py + pc)]

        def copy(k, block, to, src=None):
            return pltpu.make_async_remote_copy(
                src_ref=rows(*block) if src is None else src,
                dst_ref=rows(*block),
                send_sem=send_sems.at[k],
                recv_sem=recv_sems.at[k],
                device_id=to,
                device_id_type=pl.DeviceIdType.MESH,
            )

        mine = pltpu.make_async_copy(x_ref, rows(*me), local_sem)
        mine.start()
        first = [copy(0, me, sibling, src=x_ref)]
        first += [copy(1 + j, me, (*chip, c), src=x_ref) for j, chip in enumerate(chips)]
        for cp in first:
            cp.start()
        passed = [copy(4 + j, (*chip, c), sibling) for j, chip in enumerate(chips)]
        for j, chip in enumerate(chips):
            copy(1 + j, (*chip, c), me).wait_recv()
            passed[j].start()
        copy(0, sibling, me).wait_recv()
        for j, chip in enumerate(chips):
            copy(4 + j, (*chip, 1 - c), me).wait_recv()
        for cp in first + passed:
            cp.wait_send()
        mine.wait()

    return pl.pallas_call(
        body,
        name=name,
        out_shape=jax.ShapeDtypeStruct(out_shape, x.dtype),
        in_specs=[pl.BlockSpec(memory_space=pl.ANY)],
        out_specs=pl.BlockSpec(memory_space=pl.ANY),
        scratch_shapes=[
            pltpu.SemaphoreType.DMA((7,)),
            pltpu.SemaphoreType.DMA((7,)),
            pltpu.SemaphoreType.DMA(()),
        ],
    )(x)


_RELATIONS = [(dx, dy, dc) for dx in (0, 1) for dy in (0, 1) for dc in (0, 1) if (dx, dy, dc) != (0, 0, 0)]
_HBM = pl.BlockSpec(memory_space=pltpu.HBM)
_SEM = pl.BlockSpec(memory_space=pltpu.SEMAPHORE)
_EFFECT = pltpu.SideEffectType.DATAFLOW_SIDE_EFFECTING


def _at(ref, idx):
    return ref.at[idx] if idx else ref


def _block(axis, ndim, n):
    return lambda d: (slice(None),) * axis + (pl.ds(d * n, n),) + (slice(None),) * (ndim - axis - 1)


class _Route:
    def __init__(self, land_shape, src_slice, dst_slice):
        self.land_shape, self.src_slice, self.dst_slice = tuple(land_shape), src_slice, dst_slice


def _gather_route(shard, axis):
    n = shard.shape[axis]
    shape = shard.shape[:axis] + (N_DEV * n,) + shard.shape[axis + 1:]
    return _Route(shape, lambda p: (), _block(axis, shard.ndim, n))


def _scatter_route(g, axis):
    n = g.shape[axis] // N_DEV
    shape = (N_DEV,) + g.shape[:axis] + (n,) + g.shape[axis + 1:]
    return _Route(shape, _block(axis, g.ndim, n), lambda p: (p,))


def _peers():
    x, y, c = _mesh_pos()
    out = []
    for k, (dx, dy, dc) in enumerate(_RELATIONS):
        px, py, pc = _flip(x, dx), _flip(y, dy), _flip(c, dc)
        out.append((k, (px, py, pc), 4 * px + 2 * py + pc))
    return 4 * x + 2 * y + c, out


_LOCAL = len(_RELATIONS)


def _copies_start(srcs, routes, after, name):
    n = len(srcs)
    order = [] if after is None else [after]

    def body(*refs):
        src_refs, land_refs = refs[:n], refs[n:2 * n]
        outs = refs[2 * n + len(order):]
        send, recv, token = outs[:n], outs[n:2 * n], outs[-1]
        me, peers = _peers()
        for i, route in enumerate(routes):
            for k, peer, pid in peers:
                pltpu.make_async_remote_copy(
                    src_ref=_at(src_refs[i], route.src_slice(pid)), dst_ref=_at(land_refs[i], route.dst_slice(me)),
                    send_sem=send[i].at[k], recv_sem=recv[i].at[k],
                    device_id=peer, device_id_type=pl.DeviceIdType.MESH).start()
            pltpu.make_async_copy(_at(src_refs[i], route.src_slice(me)), _at(land_refs[i], route.dst_slice(me)),
                                  send[i].at[_LOCAL]).start()
        token[...] = jnp.zeros_like(token)

    lands = [lax.empty(r.land_shape, s.dtype) for s, r in zip(srcs, routes)]
    hbm = lambda a: pltpu.with_memory_space_constraint(a, pltpu.HBM)
    res = pl.pallas_call(
        body, name=name,
        out_shape=([pltpu.SemaphoreType.DMA((_LOCAL + 1,))] * n + [pltpu.SemaphoreType.DMA((_LOCAL,))] * n
                   + [pltpu.HBM(s.shape, s.dtype) for s in srcs]
                   + [pltpu.HBM(l.shape, l.dtype) for l in lands]
                   + [jax.ShapeDtypeStruct((8, 128), F32)]),
        in_specs=[_HBM] * (2 * n) + [pl.BlockSpec(memory_space=pl.ANY)] * len(order),
        out_specs=[_SEM] * (2 * n) + [_HBM] * (2 * n) + [pl.BlockSpec(memory_space=pltpu.VMEM)],
        input_output_aliases={i: 2 * n + i for i in range(2 * n)},
        compiler_params=pltpu.CompilerParams(has_side_effects=_EFFECT),
    )(*[hbm(s) for s in srcs], *[hbm(l) for l in lands], *order)
    handles = [(res[i], res[n + i], res[2 * n + i], res[3 * n + i]) for i in range(n)]
    return handles, res[-1]


def _copies_wait(handle, route, after, name):
    send_sems, recv_sems, src, land = handle

    def body(src_ref, land_ref, send_ref, recv_ref, after_ref, src_out, got_ref):
        me, peers = _peers()
        for k, peer, pid in peers:
            cp = pltpu.make_async_remote_copy(
                src_ref=_at(src_ref, route.src_slice(pid)), dst_ref=_at(land_ref, route.dst_slice(pid)),
                send_sem=send_ref.at[k], recv_sem=recv_ref.at[k],
                device_id=peer, device_id_type=pl.DeviceIdType.MESH)
            cp.wait_send()
            cp.wait_recv()
        pltpu.make_async_copy(_at(src_ref, route.src_slice(me)), _at(land_ref, route.dst_slice(me)),
                              send_ref.at[_LOCAL]).wait()

    return pl.pallas_call(
        body, name=name,
        out_shape=(pltpu.HBM(src.shape, src.dtype), pltpu.HBM(land.shape, land.dtype)),
        in_specs=[_HBM, _HBM, _SEM, _SEM, pl.BlockSpec(memory_space=pl.ANY)],
        out_specs=(_HBM, _HBM),
        input_output_aliases={0: 0, 1: 1},
        compiler_params=pltpu.CompilerParams(has_side_effects=_EFFECT),
    )(src, land, send_sems, recv_sems, after)[1]


def _matmul_tiles(M, N, K, a_item, b_item, mn_bytes, whole_rows):
    tm, tk = _pick(M, 512, 128), _pick(K, 2048, 128)
    tn = N if whole_rows else _pick(N, 2048, 128)
    while True:
        cast = (tm * tk * 2 if a_item != 2 else 0) + (tk * tn * 2 if b_item != 2 else 0)
        need = 2 * (tm * tk * a_item + tk * tn * b_item + tm * tn * mn_bytes) + 2 * tm * tn * 4 + cast
        if whole_rows:
            smaller = _pick(M, tm - 128, 128) if tm > 128 else tm
            if need <= MATMUL_VMEM or smaller >= tm:
                return tm, tn, tk
            tm = smaller
        else:
            smaller = _pick(N, tn - 128, 128) if tn > 128 else tn
            if need <= MATMUL_VMEM or smaller >= tn:
                return tm, tn, tk
            tn = smaller


def _matmul(a, b, mode, out_dtypes, name, epilogue=None, tiles=(), rows=(), whole_rows=False):
    if mode == "nn":
        (M, K), (K2, N) = a.shape, b.shape
    elif mode == "nt":
        (M, K), (N, K2) = a.shape, b.shape
    else:
        (K, M), (K2, N) = a.shape, b.shape
    assert K == K2, (a.shape, b.shape, mode)
    mn_bytes = sum(t.dtype.itemsize for t in tiles) + sum(jnp.dtype(d).itemsize for d in out_dtypes)
    tm, tn, tk = _matmul_tiles(M, N, K, a.dtype.itemsize, b.dtype.itemsize, mn_bytes, whole_rows)
    nm, nn, nk = M // tm, N // tn, K // tk
    n_extra = len(tiles) + len(rows)
    n_out = len(out_dtypes)
    if epilogue is None:
        epilogue = lambda acc: (acc,)
    dims = {"nn": ((1,), (0,)), "nt": ((1,), (1,)), "tn": ((0,), (0,))}[mode]

    def finish(acc, extra, outs):
        vals = epilogue(acc, *[r[...] for r in extra])
        for o_ref, v in zip(outs, vals):
            o_ref[...] = v.astype(o_ref.dtype)

    def kern(*refs):
        a_ref, b_ref = refs[0], refs[1]
        extra = refs[2:2 + n_extra]
        outs = refs[2 + n_extra:2 + n_extra + n_out]
        part = _dot(a_ref[...].astype(BF16), b_ref[...].astype(BF16), dims)
        if nk == 1:
            finish(part, extra, outs)
            return
        acc_ref = refs[-1]
        k = pl.program_id(2)

        @pl.when(k == 0)
        def _():
            acc_ref[...] = part

        @pl.when(k > 0)
        def _():
            acc_ref[...] += part

        @pl.when(k == nk - 1)
        def _():
            finish(acc_ref[...], extra, outs)

    a_bytes, b_bytes = a.size * a.dtype.itemsize, b.size * b.dtype.itemsize
    m_outer = a_bytes + nm * b_bytes <= nn * a_bytes + b_bytes
    ij = (lambda g0, g1: (g0, g1)) if m_outer else (lambda g0, g1: (g1, g0))

    def spec(shape, fn):
        return pl.BlockSpec(shape, lambda g0, g1, k: fn(*ij(g0, g1), k))

    a_spec = {"nn": spec((tm, tk), lambda i, j, k: (i, k)),
              "nt": spec((tm, tk), lambda i, j, k: (i, k)),
              "tn": spec((tk, tm), lambda i, j, k: (k, i))}[mode]
    b_spec = {"nn": spec((tk, tn), lambda i, j, k: (k, j)),
              "nt": spec((tn, tk), lambda i, j, k: (j, k)),
              "tn": spec((tk, tn), lambda i, j, k: (k, j))}[mode]
    tile_spec = spec((tm, tn), lambda i, j, k: (i, j))
    row_spec = spec((1, tn), lambda i, j, k: (0, j))
    return pl.pallas_call(
        kern,
        name=name,
        grid=(nm, nn, nk) if m_outer else (nn, nm, nk),
        in_specs=[a_spec, b_spec] + [tile_spec] * len(tiles) + [row_spec] * len(rows),
        out_specs=[tile_spec] * n_out,
        out_shape=[jax.ShapeDtypeStruct((M, N), dt) for dt in out_dtypes],
        scratch_shapes=[pltpu.VMEM((tm, tn), F32)] if nk > 1 else [],
        compiler_params=pltpu.CompilerParams(
            dimension_semantics=("parallel", "parallel", "arbitrary"), vmem_limit_bytes=VMEM_LIMIT),
    )(a, b, *tiles, *rows)


def _relu2_epilogue(acc):
    a = jnp.maximum(acc, 0.0)
    return a * a, a


def _modulated_norm(x, gain, sc, sh):
    inv = lax.rsqrt(jnp.mean(x * x, axis=-1, keepdims=True) + NORM_EPS)
    return (x * inv) * gain * (1.0 + sc) + sh


def _residual_epilogue(acc, x_tile, gate_row, *norm_rows):
    x_new = x_tile + gate_row * acc
    normed = [_modulated_norm(x_new, *norm_rows[i:i + 3]) for i in range(0, len(norm_rows), 3)]
    return (x_new, acc, *normed)


def _norm_mod_fwd(x, gain, sc, sh, name):
    S, D = x.shape
    ts = _pick(S, 256, 16)

    def kern(x_ref, g_ref, sc_ref, sh_ref, h_ref):
        h_ref[...] = _modulated_norm(x_ref[...], g_ref[...], sc_ref[...], sh_ref[...]).astype(h_ref.dtype)

    row = pl.BlockSpec((1, D), lambda i: (0, 0))
    blk = pl.BlockSpec((ts, D), lambda i: (i, 0))
    return pl.pallas_call(
        kern, name=name, grid=(S // ts,),
        in_specs=[blk, row, row, row], out_specs=blk,
        out_shape=jax.ShapeDtypeStruct((S, D), BF16),
        compiler_params=pltpu.CompilerParams(dimension_semantics=("parallel",), vmem_limit_bytes=VMEM_LIMIT),
    )(x, gain, sc, sh)


def _through_gate(d, y_ref, gate_ref, dy_ref, dgate_ref):
    dy_ref[...] = (gate_ref[...] * d).astype(dy_ref.dtype)
    dgate_ref[...] += jnp.sum(d * y_ref[...].astype(F32), axis=0, keepdims=True)


def _norm_mod_bwd(x, gain, sc, dh, d_in, name, y=None, gate=None):
    S, D = x.shape
    ts = _pick(S, 256, 16)
    gated = y is not None

    def kern(x_ref, g_ref, sc_ref, dh_ref, din_ref, *rest):
        dx_ref, dg_ref, dsc_ref, dsh_ref = rest[-6:-2] if gated else rest

        @pl.when(pl.program_id(0) == 0)
        def _():
            dg_ref[...] = jnp.zeros_like(dg_ref)
            dsc_ref[...] = jnp.zeros_like(dsc_ref)
            dsh_ref[...] = jnp.zeros_like(dsh_ref)
            if gated:
                rest[-1][...] = jnp.zeros_like(rest[-1])

        xv = x_ref[...]
        dh = dh_ref[...].astype(F32)
        inv = lax.rsqrt(jnp.mean(xv * xv, axis=-1, keepdims=True) + NORM_EPS)
        xn = xv * inv
        g = g_ref[...]
        dsh_ref[...] += jnp.sum(dh, axis=0, keepdims=True)
        dsc_ref[...] += jnp.sum(dh * (xn * g), axis=0, keepdims=True)
        dn = dh * (1.0 + sc_ref[...])
        dg_ref[...] += jnp.sum(dn * xn, axis=0, keepdims=True)
        dxn = dn * g
        d = din_ref[...] + inv * (dxn - xn * jnp.mean(dxn * xn, axis=-1, keepdims=True))
        dx_ref[...] = d
        if gated:
            _through_gate(d, rest[0], rest[1], rest[-2], rest[-1])

    row = pl.BlockSpec((1, D), lambda i: (0, 0))
    blk = pl.BlockSpec((ts, D), lambda i: (i, 0))
    vec = jax.ShapeDtypeStruct((1, D), F32)
    full = lambda dt: jax.ShapeDtypeStruct((S, D), dt)
    return pl.pallas_call(
        kern, name=name, grid=(S // ts,),
        in_specs=[blk, row, row, blk, blk] + ([blk, row] if gated else []),
        out_specs=[blk, row, row, row] + ([blk, row] if gated else []),
        out_shape=[full(F32), vec, vec, vec] + ([full(BF16), vec] if gated else []),
        compiler_params=pltpu.CompilerParams(dimension_semantics=("arbitrary",), vmem_limit_bytes=VMEM_LIMIT),
    )(x, gain, sc, dh, d_in, *((y, gate) if gated else ()))


def _final_loss(x, gain, target, y, gate, name):
    S, D = x.shape
    ts = _pick(S, 256, 16)

    def kern(x_ref, g_ref, t_ref, y_ref, gate_ref, loss_ref, dx_ref, dg_ref, dy_ref, dgate_ref):
        @pl.when(pl.program_id(0) == 0)
        def _():
            loss_ref[...] = jnp.zeros_like(loss_ref)
            dg_ref[...] = jnp.zeros_like(dg_ref)
            dgate_ref[...] = jnp.zeros_like(dgate_ref)

        xv = x_ref[...]
        g = g_ref[...]
        inv = lax.rsqrt(jnp.mean(xv * xv, axis=-1, keepdims=True) + NORM_EPS)
        xn = xv * inv
        err = xn * g - t_ref[...]
        row_loss = jnp.mean(err * err, axis=-1, keepdims=True)
        loss_ref[...] += 0.5 * jnp.sum(row_loss, axis=0, keepdims=True)
        dout = err * (1.0 / D)
        dg_ref[...] += jnp.sum(dout * xn, axis=0, keepdims=True)
        dxn = dout * g
        d = inv * (dxn - xn * jnp.mean(dxn * xn, axis=-1, keepdims=True))
        dx_ref[...] = d
        _through_gate(d, y_ref, gate_ref, dy_ref, dgate_ref)

    row = pl.BlockSpec((1, D), lambda i: (0, 0))
    blk = pl.BlockSpec((ts, D), lambda i: (i, 0))
    vec = jax.ShapeDtypeStruct((1, D), F32)
    return pl.pallas_call(
        kern, name=name, grid=(S // ts,),
        in_specs=[blk, row, blk, blk, row],
        out_specs=[pl.BlockSpec((1, 128), lambda i: (0, 0)), blk, row, blk, row],
        out_shape=[jax.ShapeDtypeStruct((1, 128), F32), jax.ShapeDtypeStruct((S, D), F32), vec,
                   jax.ShapeDtypeStruct((S, D), BF16), vec],
        compiler_params=pltpu.CompilerParams(dimension_semantics=("arbitrary",), vmem_limit_bytes=VMEM_LIMIT),
    )(x, gain, target, y, gate)


def _regroup_columns(w, outer, inner, name):
    rows = w.shape[0]

    def kern(i_ref, o_ref):
        o_ref[...] = i_ref[...]

    return pl.pallas_call(
        kern, name=name, grid=(outer, inner),
        in_specs=[pl.BlockSpec((rows, HEAD_DIM), lambda a, b: (0, a * inner + b))],
        out_specs=pl.BlockSpec((rows, HEAD_DIM), lambda a, b: (0, b * outer + a)),
        out_shape=jax.ShapeDtypeStruct(w.shape, w.dtype),
        compiler_params=pltpu.CompilerParams(
            dimension_semantics=("parallel", "parallel"), vmem_limit_bytes=VMEM_LIMIT),
    )(w)


def _heads_major(w, name):
    return _regroup_columns(w, 4, w.shape[1] // (4 * HEAD_DIM), name)


def _heads_minor(w, name):
    return _regroup_columns(w, w.shape[1] // (4 * HEAD_DIM), 4, name)


def _part(ref, j, rows=slice(None)):
    return ref[rows, j * HEAD_DIM:(j + 1) * HEAD_DIM]


def _chunk_scan(v, ric, reverse):
    n = v.shape[0]
    d = 1
    while d < A_CHUNK:
        if reverse:
            v = v + jnp.where(ric < A_CHUNK - d, pltpu.roll(v, n - d, 0), 0.0)
        else:
            v = v + jnp.where(ric >= d, pltpu.roll(v, d, 0), 0.0)
        d *= 2
    return v


def _hgrn2_gates(q_raw, f_logit, lbv, ric):
    qs = _silu(q_raw)
    sig = _sigmoid(f_logit)
    f = lbv + (1.0 - lbv) * sig
    log_f = jnp.log(f)
    kk = (1.0 - lbv) * _sigmoid(-f_logit)
    cum = _chunk_scan(log_f, ric, False)
    cl = cum + _chunk_scan(log_f, ric, True) - log_f
    e_cum = jnp.exp(cum)
    e_neg = jnp.exp(-cum)
    e_end = jnp.exp(cl - cum)
    dec = jnp.exp(cl)
    return qs, sig, f, kk, e_cum, e_neg, e_end, dec


def _same_chunk_mask():
    r = lax.broadcasted_iota(jnp.int32, (HGRN2_GROUP, HGRN2_GROUP), 0)
    c = lax.broadcasted_iota(jnp.int32, (HGRN2_GROUP, HGRN2_GROUP), 1)
    return (r // A_CHUNK == c // A_CHUNK) & (r >= c)


def _hgrn2_specs(S, D, TB, reverse):
    H = D // HEAD_DIM
    NB = S // TB
    pos = (lambda nb: NB - 1 - nb) if reverse else (lambda nb: nb)
    proj = pl.BlockSpec((TB, 4 * HEAD_DIM), lambda h, nb: (pos(nb), h))
    head = pl.BlockSpec((TB, HEAD_DIM), lambda h, nb: (pos(nb), h))
    vec = pl.BlockSpec((1, HEAD_DIM), lambda h, nb: (0, h))
    state = pl.BlockSpec((TB // A_CHUNK, None, HEAD_DIM, HEAD_DIM), lambda h, nb: (pos(nb), h, 0, 0))
    return H, NB, proj, head, vec, state


def _hgrn2_fwd(proj, lb, gain, name):
    S, D4 = proj.shape
    D = D4 // 4
    TB = _pick(S, 512, A_CHUNK)
    H, NB, pspec, head, vec, state = _hgrn2_specs(S, D, TB, False)
    NCB = TB // A_CHUNK

    def kern(p_ref, lb_ref, gain_ref, og_ref, oraw_ref, st_ref, a_s, b_s, k_s, v_s, dec_s, o_s, st_s):
        @pl.when(pl.program_id(1) == 0)
        def _():
            st_s[...] = jnp.zeros_like(st_s)

        ric = lax.broadcasted_iota(jnp.int32, (TB, HEAD_DIM), 0) % A_CHUNK
        qs, _, _, kk, e_cum, e_neg, e_end, dec = _hgrn2_gates(_part(p_ref, 0), _part(p_ref, 1), lb_ref[...], ric)
        a_s[...] = (qs * e_cum).astype(BF16)
        b_s[...] = (kk * e_neg).astype(BF16)
        k_s[...] = (kk * e_end).astype(BF16)
        v_s[...] = _part(p_ref, 2).astype(BF16)
        dec_s[...] = dec

        same_chunk = _same_chunk_mask()
        for gi in range(TB // HGRN2_GROUP):
            rows = pl.ds(gi * HGRN2_GROUP, HGRN2_GROUP)
            p = jnp.where(same_chunk, _dot_nt(a_s[rows, :], b_s[rows, :]), 0.0).astype(BF16)
            o_s[rows, :] = _dot_nn(p, v_s[rows, :])

        def chunks(it, st):
            for u in range(HGRN2_UNROLL):
                ci = it * HGRN2_UNROLL + u
                r = pl.multiple_of(ci * A_CHUNK, A_CHUNK)
                rows = pl.ds(r, A_CHUNK)
                st_bf = st.astype(BF16)
                st_ref[ci] = st_bf
                o_s[rows, :] += _dot_nt(a_s[rows, :], st_bf)
                st = dec_s[pl.ds(r, 1), :] * st + _dot_tn(v_s[rows, :], k_s[rows, :])
            return st

        st_s[...] = lax.fori_loop(0, NCB // HGRN2_UNROLL, chunks, st_s[...])
        o = o_s[...]
        oraw_ref[...] = o
        on = o * lax.rsqrt(jnp.mean(o * o, axis=-1, keepdims=True) + NORM_EPS)
        og_ref[...] = ((on * gain_ref[...]) * _silu(_part(p_ref, 3))).astype(og_ref.dtype)

    tb_bf = pltpu.VMEM((TB, HEAD_DIM), BF16)
    tb_f = pltpu.VMEM((TB, HEAD_DIM), F32)
    return pl.pallas_call(
        kern, name=name, grid=(H, NB),
        in_specs=[pspec, vec, vec],
        out_specs=[head, head, state],
        out_shape=[jax.ShapeDtypeStruct((S, D), BF16), jax.ShapeDtypeStruct((S, D), F32),
                   jax.ShapeDtypeStruct((S // A_CHUNK, H, HEAD_DIM, HEAD_DIM), BF16)],
        scratch_shapes=[tb_bf, tb_bf, tb_bf, tb_bf, tb_f, tb_f, pltpu.VMEM((HEAD_DIM, HEAD_DIM), F32)],
        compiler_params=pltpu.CompilerParams(
            dimension_semantics=("parallel", "arbitrary"), vmem_limit_bytes=VMEM_LIMIT),
    )(proj, lb, gain)


def _hgrn2_bwd(proj, lb, gain, oraw, states, dog, name):
    S, D4 = proj.shape
    D = D4 // 4
    TB = _pick(S, 512, A_CHUNK)
    H, NB, pspec, head, vec, state = _hgrn2_specs(S, D, TB, True)
    NCB = TB // A_CHUNK

    def kern(p_ref, lb_ref, gain_ref, oraw_ref, st_ref, dog_ref, dp_ref, dlb_ref, dgain_ref,
             a_s, b_s, k_s, v_s, do_s, dec_s, da_s, db_s, dk_s, dv_s, ddec_s, dst_s):
        @pl.when(pl.program_id(1) == 0)
        def _():
            dst_s[...] = jnp.zeros_like(dst_s)
            dlb_ref[...] = jnp.zeros_like(dlb_ref)
            dgain_ref[...] = jnp.zeros_like(dgain_ref)

        ric = lax.broadcasted_iota(jnp.int32, (TB, HEAD_DIM), 0) % A_CHUNK
        lbv = lb_ref[...]
        q_raw = _part(p_ref, 0)
        qs, sig, f, kk, e_cum, e_neg, e_end, dec = _hgrn2_gates(q_raw, _part(p_ref, 1), lbv, ric)
        a32, b32, k32 = qs * e_cum, kk * e_neg, kk * e_end
        a_s[...] = a32.astype(BF16)
        b_s[...] = b32.astype(BF16)
        k_s[...] = k32.astype(BF16)
        v_s[...] = _part(p_ref, 2).astype(BF16)
        dec_s[...] = dec

        o = oraw_ref[...]
        gain_v = gain_ref[...]
        g_raw = _part(p_ref, 3)
        rinv = lax.rsqrt(jnp.mean(o * o, axis=-1, keepdims=True) + NORM_EPS)
        on = o * rinv
        dog_v = dog_ref[...].astype(F32)
        dp_ref[:, 3 * HEAD_DIM:4 * HEAD_DIM] = (dog_v * (on * gain_v) * _dsilu(g_raw)).astype(dp_ref.dtype)
        dog2 = dog_v * _silu(g_raw)
        dgain_ref[...] += jnp.sum(dog2 * on, axis=0, keepdims=True)
        don = dog2 * gain_v
        do_s[...] = (rinv * (don - on * jnp.mean(don * on, axis=-1, keepdims=True))).astype(BF16)

        same_chunk = _same_chunk_mask()
        for gi in range(TB // HGRN2_GROUP):
            rows = pl.ds(gi * HGRN2_GROUP, HGRN2_GROUP)
            a, b, do = a_s[rows, :], b_s[rows, :], do_s[rows, :]
            p = jnp.where(same_chunk, _dot_nt(a, b), 0.0).astype(BF16)
            dp = jnp.where(same_chunk, _dot_nt(do, v_s[rows, :]), 0.0).astype(BF16)
            dv_s[rows, :] = _dot_tn(p, do)
            da_s[rows, :] = _dot_nn(dp, b)
            db_s[rows, :] = _dot_tn(dp, a)

        def chunks(it, dst):
            for u in range(HGRN2_UNROLL):
                ci = NCB - 1 - (it * HGRN2_UNROLL + u)
                r = pl.multiple_of(ci * A_CHUNK, A_CHUNK)
                rows = pl.ds(r, A_CHUNK)
                do = do_s[rows, :]
                st_prev = st_ref[ci]
                dst_bf = dst.astype(BF16)
                dv_s[rows, :] += _dot_nt(k_s[rows, :], dst_bf)
                da_s[rows, :] += _dot_nn(do, st_prev)
                dk_s[rows, :] = _dot_nn(v_s[rows, :], dst_bf)
                ddec = jnp.sum(dst * st_prev.astype(F32), axis=0, keepdims=True)
                ddec_s[rows, :] = jnp.broadcast_to(ddec, (A_CHUNK, HEAD_DIM))
                dst = dec_s[pl.ds(r, 1), :] * dst + _dot_tn(do, a_s[rows, :])
            return dst

        dst_s[...] = lax.fori_loop(0, NCB // HGRN2_UNROLL, chunks, dst_s[...])

        da, db, dk = da_s[...], db_s[...], dk_s[...]
        dqs = da * e_cum
        dkk = db * e_neg + dk * e_end
        w = dk * k32
        dlog_f = (_chunk_scan(da * a32 - db * b32, ric, True) + (_chunk_scan(w, ric, False) - w)
                  + ddec_s[...] * dec)
        dfg = dlog_f / f - dkk
        dlb_ref[...] += jnp.sum(dfg * (1.0 - sig), axis=0, keepdims=True)
        dp_ref[:, 0:HEAD_DIM] = (dqs * _dsilu(q_raw)).astype(dp_ref.dtype)
        dp_ref[:, HEAD_DIM:2 * HEAD_DIM] = (dfg * (1.0 - lbv) * sig * (1.0 - sig)).astype(dp_ref.dtype)
        dp_ref[:, 2 * HEAD_DIM:3 * HEAD_DIM] = dv_s[...].astype(dp_ref.dtype)

    tb_bf = pltpu.VMEM((TB, HEAD_DIM), BF16)
    tb_f = pltpu.VMEM((TB, HEAD_DIM), F32)
    vec_shape = jax.ShapeDtypeStruct((1, D), F32)
    return pl.pallas_call(
        kern, name=name, grid=(H, NB),
        in_specs=[pspec, vec, vec, head, state, head],
        out_specs=[pspec, vec, vec],
        out_shape=[jax.ShapeDtypeStruct((S, D4), BF16), vec_shape, vec_shape],
        scratch_shapes=[tb_bf, tb_bf, tb_bf, tb_bf, tb_bf, tb_f, tb_f, tb_f, tb_f, tb_f, tb_f,
                        pltpu.VMEM((HEAD_DIM, HEAD_DIM), F32)],
        compiler_params=pltpu.CompilerParams(
            dimension_semantics=("parallel", "arbitrary"), vmem_limit_bytes=VMEM_LIMIT),
    )(proj, lb, gain, oraw, states, dog)


def _with_ones(tri):
    return jnp.concatenate([tri, jnp.ones_like(tri)], axis=1)


def _stack(ref, G):
    return jnp.concatenate([ref[:, g * HEAD_DIM:(g + 1) * HEAD_DIM] for g in range(G)], axis=0)


def _unstack(v, G):
    return jnp.concatenate([v[g * SB_TILE:(g + 1) * SB_TILE, :] for g in range(G)], axis=1)


def _sb_specs(S, D, KVH):
    G = D // HEAD_DIM // KVH
    qblk = pl.BlockSpec((SB_TILE, G * HEAD_DIM), lambda h, qi: (qi, h))
    kblk = pl.BlockSpec((S, HEAD_DIM), lambda h, qi: (0, h))
    vblk = pl.BlockSpec((S, HEAD_DIM), lambda h, qi: (0, KVH + h))
    return G, qblk, kblk, vblk


def _tile_index(qi, j):
    return qi * (qi + 1) // 2 + j


def _sb_fwd(q, kv, name):
    S, D = q.shape
    KVH = kv.shape[1] // (2 * HEAD_DIM)
    G, qblk, kblk, vblk = _sb_specs(S, D, KVH)
    R = G * SB_TILE
    NQ = S // SB_TILE
    scale = HEAD_DIM ** -0.5

    def kern(q_ref, k_ref, v_ref, o_ref, w_hbm, b_hbm, acc_s, run_s, w_stage, b_stage, sems):
        head, qi = pl.program_id(0), pl.program_id(1)
        qs = _stack(q_ref, G)
        row = lax.broadcasted_iota(jnp.int32, (R, SB_TILE), 0) % SB_TILE
        colm = lax.broadcasted_iota(jnp.int32, (R, SB_TILE), 1)
        mask = colm < row
        ti = lax.broadcasted_iota(jnp.int32, (SB_TILE, SB_TILE), 0)
        tj = lax.broadcasted_iota(jnp.int32, (SB_TILE, SB_TILE), 1)
        after = _with_ones((ti > tj).astype(BF16))
        acc_s[...] = jnp.zeros_like(acc_s)
        run_s[...] = jnp.zeros_like(run_s)

        def save(slot, j):
            idx = _tile_index(qi, j)
            return (pltpu.make_async_copy(w_stage.at[slot], w_hbm.at[head, idx], sems.at[0, slot]),
                    pltpu.make_async_copy(b_stage.at[slot], b_hbm.at[head, idx], sems.at[1, slot]))

        def tile(j, slot, masked):
            ks = pl.ds(pl.multiple_of(j * SB_TILE, SB_TILE), SB_TILE)
            kj, vj = k_ref[ks, :], v_ref[ks, :]
            z = _dot_nt(qs, kj) * scale
            t = jnp.exp(-jnp.abs(z))
            one_t = 1.0 + t
            log_beta = jnp.minimum(z, 0.0) - jnp.log(one_t)
            log_rest = log_beta - z
            if masked:
                log_rest = jnp.where(mask, log_rest, 0.0)
            sums = _dot_nn(log_rest.astype(BF16), after)
            between = sums[:, :SB_TILE] + run_s[...]
            w = jnp.exp(log_beta + between)
            if masked:
                w = jnp.where(mask, w, 0.0)
            w_bf = w.astype(BF16)
            acc_s[...] += _dot_nn(w_bf, vj)
            run_s[...] += sums[:, SB_TILE:]
            w_stage[slot] = w_bf
            b_stage[slot] = (jnp.where(z >= 0.0, 1.0, t) * pl.reciprocal(one_t, approx=True)).astype(BF16)
            for cp in save(slot, j):
                cp.start()

        tile(qi, 0, True)

        def body(it, carry):
            slot = (it + 1) % 2

            @pl.when(it >= 1)
            def _():
                for cp in save(slot, 0):
                    cp.wait()

            tile(qi - 1 - it, slot, False)
            return carry

        lax.fori_loop(0, qi, body, 0)
        o_ref[...] = _unstack(acc_s[...], G).astype(o_ref.dtype)
        for cp in save(qi % 2, 0):
            cp.wait()

        @pl.when(qi >= 1)
        def _():
            for cp in save((qi + 1) % 2, 0):
                cp.wait()

    tiles = jax.ShapeDtypeStruct((KVH, NQ * (NQ + 1) // 2, R, SB_TILE), BF16)
    stage = pltpu.VMEM((2, R, SB_TILE), BF16)
    return pl.pallas_call(
        kern, name=name, grid=(KVH, NQ),
        in_specs=[qblk, kblk, vblk],
        out_specs=[qblk, pl.BlockSpec(memory_space=pl.ANY), pl.BlockSpec(memory_space=pl.ANY)],
        out_shape=[jax.ShapeDtypeStruct((S, D), BF16), tiles, tiles],
        scratch_shapes=[pltpu.VMEM((R, HEAD_DIM), F32), pltpu.VMEM((R, SB_TILE), F32), stage, stage,
                        pltpu.SemaphoreType.DMA((2, 2))],
        compiler_params=pltpu.CompilerParams(
            dimension_semantics=("parallel", "arbitrary"), vmem_limit_bytes=VMEM_LIMIT),
    )(q, kv, kv)


def _sb_bwd(q, kv, w_tiles, b_tiles, do, after, name):
    S, D = q.shape
    KVH = kv.shape[1] // (2 * HEAD_DIM)
    G, qblk, kblk, vblk = _sb_specs(S, D, KVH)
    R = G * SB_TILE
    scale = HEAD_DIM ** -0.5

    def kern(q_ref, k_ref, v_ref, w_hbm, b_hbm, do_ref, after_ref, dq_ref, dk_ref, dv_ref,
             dq_s, esum_s, w_stage, b_stage, sems):
        head, qi = pl.program_id(0), pl.program_id(1)

        @pl.when(qi == 0)
        def _():
            dk_ref[...] = jnp.zeros_like(dk_ref)
            dv_ref[...] = jnp.zeros_like(dv_ref)

        qs = _stack(q_ref, G)
        dos = _stack(do_ref, G)
        row = lax.broadcasted_iota(jnp.int32, (R, SB_TILE), 0) % SB_TILE
        colm = lax.broadcasted_iota(jnp.int32, (R, SB_TILE), 1)
        mask = colm < row
        ti = lax.broadcasted_iota(jnp.int32, (SB_TILE, SB_TILE), 0)
        tj = lax.broadcasted_iota(jnp.int32, (SB_TILE, SB_TILE), 1)
        before = (ti < tj).astype(BF16)
        dq_s[...] = jnp.zeros_like(dq_s)
        esum_s[...] = jnp.zeros_like(esum_s)

        def fetch(slot, j):
            idx = _tile_index(qi, j)
            return (pltpu.make_async_copy(w_hbm.at[head, idx], w_stage.at[slot], sems.at[0, slot]),
                    pltpu.make_async_copy(b_hbm.at[head, idx], b_stage.at[slot], sems.at[1, slot]))

        def tile(j, slot, masked):
            for cp in fetch(slot, j):
                cp.wait()
            ks = pl.ds(pl.multiple_of(j * SB_TILE, SB_TILE), SB_TILE)
            kj, vj = k_ref[ks, :], v_ref[ks, :]
            w_bf = w_stage[slot]
            beta = b_stage[slot].astype(F32)
            e = _dot_nt(dos, vj) * w_bf.astype(F32)
            e_before = esum_s[...] + _dot_nn(e.astype(BF16), before)
            dz = e - beta * (e + e_before)
            if masked:
                dz = jnp.where(mask, dz, 0.0)
            dzs = (dz * scale).astype(BF16)
            dq_s[...] += _dot_nn(dzs, kj)
            dk_ref[ks, :] += _dot_tn(dzs, qs)
            dv_ref[ks, :] += _dot_tn(w_bf, dos)
            esum_s[...] += jnp.sum(e, axis=1, keepdims=True)

        for cp in fetch(0, 0):
            cp.start()

        def body(j, carry):
            for cp in fetch((j + 1) % 2, j + 1):
                cp.start()
            tile(j, j % 2, False)
            return carry

        lax.fori_loop(0, qi, body, 0)
        tile(qi, qi % 2, True)
        dq_ref[...] = _unstack(dq_s[...], G).astype(dq_ref.dtype)

    kvout = pl.BlockSpec((S, HEAD_DIM), lambda h, qi: (0, h))
    stage = pltpu.VMEM((2, R, SB_TILE), BF16)
    hbm = pl.BlockSpec(memory_space=pl.ANY)
    return pl.pallas_call(
        kern, name=name, grid=(KVH, S // SB_TILE),
        in_specs=[qblk, kblk, vblk, hbm, hbm, qblk, pl.BlockSpec(after.shape, lambda h, qi: (0, 0))],
        out_specs=[qblk, kvout, kvout],
        out_shape=[jax.ShapeDtypeStruct((S, D), BF16), jax.ShapeDtypeStruct((S, KVH * HEAD_DIM), F32),
                   jax.ShapeDtypeStruct((S, KVH * HEAD_DIM), F32)],
        scratch_shapes=[pltpu.VMEM((R, HEAD_DIM), F32), pltpu.VMEM((R, SB_TILE), F32), stage, stage,
                        pltpu.SemaphoreType.DMA((2, 2))],
        compiler_params=pltpu.CompilerParams(
            dimension_semantics=("parallel", "arbitrary"), vmem_limit_bytes=VMEM_LIMIT),
    )(q, kv, kv, w_tiles, b_tiles, do, after)


def _adamw(parts, w, m, v, name, layer=None, filled=None):
    shape = w.shape
    L = 1 if layer is None else shape[0]
    l = 0 if layer is None else layer
    C = shape[-1]
    R = w.size // (C * L)
    P = parts.shape[0]
    parts3 = parts.reshape(P, R, C)
    w3, m3, v3 = w.reshape(L, R, C), m.reshape(L, R, C), v.reshape(L, R, C)
    tr = _pick(R, max(8, (1 << 18) // C), 16)

    def kern(p_ref, w_ref, m_ref, v_ref, *rest):
        g_ref, d_ref, nm_ref, nv_ref = rest[-4:]
        g = p_ref[0].astype(F32)
        for i in range(1, P):
            g = g + p_ref[i].astype(F32)
        nm = ADAM_B1 * m_ref[...] + (1.0 - ADAM_B1) * g
        nv = ADAM_B2 * v_ref[...] + (1.0 - ADAM_B2) * (g * g)
        m_hat = nm / (1.0 - ADAM_B1 ** ADAM_STEP)
        v_hat = nv / (1.0 - ADAM_B2 ** ADAM_STEP)
        g_ref[...] = g
        d_ref[...] = -ADAM_LR * (m_hat / (jnp.sqrt(v_hat) + ADAM_EPS) + ADAM_WD * w_ref[...])
        nm_ref[...] = nm
        nv_ref[...] = nv

    blk = pl.BlockSpec((None, tr, C), lambda i: (l, i, 0))
    out = jax.ShapeDtypeStruct((L, R, C), F32)
    extra = [] if filled is None else [f.reshape(L, R, C) for f in filled]
    res = pl.pallas_call(
        kern, name=name, grid=(R // tr,),
        in_specs=[pl.BlockSpec((P, tr, C), lambda i: (0, i, 0)), blk, blk, blk]
        + [pl.BlockSpec(memory_space=pl.ANY)] * len(extra),
        out_specs=[blk, blk, blk, blk], out_shape=[out, out, out, out],
        input_output_aliases={4 + j: j for j in range(len(extra))},
        compiler_params=pltpu.CompilerParams(dimension_semantics=("parallel",), vmem_limit_bytes=VMEM_LIMIT),
    )(parts3, w3, m3, v3, *extra)
    return tuple(r.reshape(shape) for r in res)


def _lower_bound(logits):
    return jnp.cumsum(jax.nn.softmax(logits.astype(F32), axis=0), axis=0)[0:1]


def _pad_rows(a, rows):
    return jnp.zeros((rows, a.shape[1]), a.dtype).at[:a.shape[0]].set(a)


def kernel(x, c, ada_w, ada_b, norm_mix, norm_mlp, a_w_in, a_lb_logits, a_out_gain, a_w_out, kv_ada_w, kv_ada_b, kv_norm, w_kv, b_w_q, b_w_out, mlp_w1, mlp_w2, final_norm, loss_target, m_ada_w, m_ada_b, m_norm_mix, m_norm_mlp, m_a_w_in, m_a_lb_logits, m_a_out_gain, m_a_w_out, m_kv_ada_w, m_kv_ada_b, m_kv_norm, m_w_kv, m_b_w_q, m_b_w_out, m_mlp_w1, m_mlp_w2, m_final_norm, v_ada_w, v_ada_b, v_norm_mix, v_norm_mlp, v_a_w_in, v_a_lb_logits, v_a_out_gain, v_a_w_out, v_kv_ada_w, v_kv_ada_b, v_kv_norm, v_w_kv, v_b_w_q, v_b_w_out, v_mlp_w1, v_mlp_w2, v_final_norm):
    S, D = x.shape[1], x.shape[2]
    assert x.shape[0] == 1 and ada_w.shape[0] == 2 and a_w_in.shape[0] == 1 and b_w_q.shape[0] == 1
    me = 4 * lax.axis_index("x") + 2 * lax.axis_index("y") + lax.axis_index("c")
    dl = D // N_DEV
    na = ada_w.shape[2]
    nk = kv_ada_w.shape[1]

    small = jnp.concatenate([c.reshape(1, D), a_lb_logits.reshape(1, 2 * dl), a_out_gain.reshape(1, dl)], axis=1)
    small_all = _all_gather(small, "gather_small")[:, 0, :]
    c_all = small_all[:, :D]
    lb_logits = small_all[:, D:D + 2 * dl].reshape(N_DEV, 2, dl).transpose(1, 0, 2).reshape(2, D)
    out_gain = small_all[:, D + 2 * dl:].reshape(1, D)

    c_act = jax.nn.silu(c_all)
    c_act_rows = _pad_rows(c_act.astype(BF16), 128)
    mod_cols = jnp.concatenate([
        _matmul(c_act_rows, ada_w[0].astype(BF16), "nn", (F32,), "ada0")[0][:N_DEV],
        _matmul(c_act_rows, ada_w[1].astype(BF16), "nn", (F32,), "ada1")[0][:N_DEV],
        _matmul(c_act_rows, kv_ada_w.astype(BF16), "nn", (F32,), "ada_kv")[0][:N_DEV]], axis=1)
    mod_all = _all_gather(mod_cols, "gather_mod")
    mod_mine = lax.dynamic_index_in_dim(mod_all, me, axis=1, keepdims=False)
    mod0 = mod_mine[:, :na].reshape(1, 6 * D) + ada_b[0:1]
    mod1 = mod_mine[:, na:2 * na].reshape(1, 6 * D) + ada_b[1:2]
    modk = mod_mine[:, 2 * na:].reshape(1, 2 * D) + kv_ada_b.reshape(1, 2 * D)
    sh1a, sc1a, g1a, sh2a, sc2a, g2a = jnp.split(mod0, 6, axis=1)
    sh1b, sc1b, g1b, sh2b, sc2b, g2b = jnp.split(mod1, 6, axis=1)
    kv_sh, kv_sc = jnp.split(modk, 2, axis=1)
    nmix0, nmix1, nmlp0, nmlp1 = norm_mix[0:1], norm_mix[1:2], norm_mlp[0:1], norm_mlp[1:2]
    kvn = kv_norm.reshape(1, D)
    lb, lb_vjp = jax.vjp(_lower_bound, lb_logits)

    w_in = _all_gather(a_w_in[0].astype(BF16), "gather_a_w_in", axis=1)
    w_names = ["a_w_out", "mlp0_w1", "mlp0_w2", "w_kv", "b_w_q", "b_w_out", "mlp1_w1", "mlp1_w2"]
    w_shards = [a_w_out[0], mlp_w1[0], mlp_w2[0], w_kv, b_w_q[0], b_w_out[0], mlp_w1[1], mlp_w2[1]]
    w_axes = [0, 1, 0, 0, 0, 0, 1, 0]
    w_shards = [s.astype(BF16) for s in w_shards]
    w_routes = [_gather_route(s, ax) for s, ax in zip(w_shards, w_axes)]
    w_handles, w_token = _copies_start(w_shards, w_routes, w_in, "gather_weights_start")

    def weight(key, after):
        i = w_names.index(key)
        return _copies_wait(w_handles[i], w_routes[i], after, "gather_" + key + "_wait")

    x0 = x[0]
    h1 = _norm_mod_fwd(x0, nmix0, sc1a + w_token[0:1, 0:1], sh1a, "a_in_norm")
    w_in = _heads_major(w_in, "a_w_in_by_head")
    (proj,) = _matmul(h1, w_in, "nn", (F32,), "a_in_mm")
    og, oraw, states = _hgrn2_fwd(proj, lb, out_gain, "hgrn2_fwd")
    w_aout = weight("a_w_out", og)
    x1, y1, h2 = _matmul(og, w_aout, "nn", (F32, BF16, BF16), "a_out_mm", _residual_epilogue, (x0,),
                         (g1a, nmlp0, sc2a, sh2a), whole_rows=True)
    w1a = weight("mlp0_w1", h2)
    u0, a0 = _matmul(h2, w1a, "nn", (BF16, BF16), "mlp0_up_mm", _relu2_epilogue)
    w2a = weight("mlp0_w2", u0)
    x2, y2 = _matmul(u0, w2a, "nn", (F32, BF16), "mlp0_down_mm", _residual_epilogue, (x1,), (g2a,))

    hk = _norm_mod_fwd(x2, kvn, kv_sc, kv_sh, "kv_norm")
    w_kvf = weight("w_kv", hk)
    (kv,) = _matmul(hk, w_kvf, "nn", (BF16,), "kv_mm")
    h3 = _norm_mod_fwd(x2, nmix1, sc1b, sh1b, "b_q_norm")
    w_q = weight("b_w_q", h3)
    (q,) = _matmul(h3, w_q, "nn", (BF16,), "b_q_mm")
    o, w_tiles, b_tiles = _sb_fwd(q, kv, "attn_fwd")
    w_bout = weight("b_w_out", o)
    x3, y3, h4 = _matmul(o, w_bout, "nn", (F32, BF16, BF16), "b_out_mm", _residual_epilogue, (x2,),
                         (g1b, nmlp1, sc2b, sh2b), whole_rows=True)
    w1b = weight("mlp1_w1", h4)
    u1, a1 = _matmul(h4, w1b, "nn", (BF16, BF16), "mlp1_up_mm", _relu2_epilogue)
    w2b = weight("mlp1_w2", u1)
    x4, y4 = _matmul(u1, w2b, "nn", (F32, BF16), "mlp1_down_mm", _residual_epilogue, (x3,), (g2b,))

    loss_vec, d4, d_final, dy4, dg2b = _final_loss(x4, final_norm.reshape(1, D), loss_target[0], y4, g2b, "final_loss")
    loss = lax.psum(loss_vec[0, 0], MESH_AXES)

    sent = {}

    def send(key, g, axis):
        route = _scatter_route(g, axis)
        (handle,), token = _copies_start([g], [route], None, "scatter_" + key + "_start")
        sent[key] = (handle, route)
        return token

    def behind(vec, *tokens):
        for token in tokens:
            vec = vec + token[0:1, 0:1]
        return vec

    def mlp_bwd(tag, d, dy, u, a, w2, w1, h, x_in, gain, sc, y_in, gate_in):
        (dw2,) = _matmul(u, dy, "tn", (BF16,), tag + "_down_dw")
        token2 = send(tag + "_w2", dw2, 0)
        (dz,) = _matmul(dy, w2, "nt", (BF16,), tag + "_down_dz", lambda acc, a_tile: (acc * (2.0 * a_tile),), (a,))
        (dw1,) = _matmul(h, dz, "tn", (BF16,), tag + "_up_dw")
        token1 = send(tag + "_w1", dw1, 1)
        (dh,) = _matmul(dz, w1, "nt", (F32,), tag + "_up_dh")
        return _norm_mod_bwd(x_in, gain, behind(sc, token2, token1), dh, d, tag + "_up_dnorm", y_in, gate_in)

    d3, dnmlp1, dsc2b, dsh2b, dy3, dg1b = mlp_bwd("mlp1", d4, dy4, u1, a1, w2b, w1b, h4, x3, nmlp1, sc2b, y3, g1b)

    (dw_bout,) = _matmul(o, dy3, "tn", (BF16,), "b_out_dw")
    token = send("b_w_out", dw_bout, 0)
    (do,) = _matmul(dy3, w_bout, "nt", (BF16,), "b_out_du")
    dq, dk, dv = _sb_bwd(q, kv, w_tiles, b_tiles, do, token, "attn_bwd")
    dkv = jnp.concatenate([dk, dv], axis=1).astype(BF16)
    (dw_q,) = _matmul(h3, dq, "tn", (BF16,), "b_q_dw")
    token = send("b_w_q", dw_q, 0)
    (dh3,) = _matmul(dq, w_q, "nt", (F32,), "b_q_dh")
    d2, dnmix1, dsc1b, dsh1b = _norm_mod_bwd(x2, nmix1, behind(sc1b, token), dh3, d3, "b_q_dnorm")
    (dw_kv,) = _matmul(hk, dkv, "tn", (BF16,), "kv_dw")
    token = send("w_kv", dw_kv, 0)
    (dhk,) = _matmul(dkv, w_kvf, "nt", (F32,), "kv_dh")
    d2, dkvn, dkv_sc, dkv_sh, dy2, dg2a = _norm_mod_bwd(x2, kvn, behind(kv_sc, token), dhk, d2, "kv_dnorm", y2, g2a)

    d1, dnmlp0, dsc2a, dsh2a, dy1, dg1a = mlp_bwd("mlp0", d2, dy2, u0, a0, w2a, w1a, h2, x1, nmlp0, sc2a, y1, g1a)

    (dw_aout,) = _matmul(og, dy1, "tn", (BF16,), "a_out_dw")
    token = send("a_w_out", dw_aout, 0)
    (dog,) = _matmul(dy1, w_aout, "nt", (BF16,), "a_out_du")
    dproj, dlb, d_out_gain = _hgrn2_bwd(proj, behind(lb, token), out_gain, oraw, states, dog, "hgrn2_bwd")
    (dw_in,) = _matmul(h1, dproj, "tn", (BF16,), "a_in_dw")
    token = send("a_w_in", _heads_minor(dw_in, "a_w_in_grad_by_part"), 1)
    (dh1,) = _matmul(dproj, w_in, "nt", (F32,), "a_in_dh")
    d0, dnmix0, dsc1a, dsh1a = _norm_mod_bwd(x0, nmix0, behind(sc1a, token), dh1, d1, "a_in_dnorm")
    (d_lb_logits,) = lb_vjp(dlb)

    dmod0 = jnp.concatenate([dsh1a, dsc1a, dg1a, dsh2a, dsc2a, dg2a], axis=1)
    dmod1 = jnp.concatenate([dsh1b, dsc1b, dg1b, dsh2b, dsc2b, dg2b], axis=1)
    dmodk = jnp.concatenate([dkv_sh, dkv_sc], axis=1)
    pieces = [dmod0, dmod1, dmodk, dnmix0, dnmix1, dnmlp0, dnmlp1, dkvn, d_final,
              d_lb_logits.reshape(1, 2 * D), d_out_gain]
    widths = [p.shape[1] for p in pieces]
    offs = [sum(widths[:i]) for i in range(len(widths))]
    part_all = _all_gather(jnp.concatenate(pieces, axis=1), "gather_dsmall")[:, 0, :]
    take = lambda i: part_all[:, offs[i]:offs[i] + widths[i]]
    dmod0_all, dmod1_all, dmodk_all = take(0), take(1), take(2)

    outs = {}

    def update(key, parts, w, m, v, **kw):
        outs[key] = _adamw(parts, w, m, v, "adamw_" + key + ("_%d" % kw["layer"] if "layer" in kw else ""), **kw)

    c_act_cols = _pad_rows(c_act.astype(BF16), 128).T
    my_cols = lambda a, n: _pad_rows(lax.dynamic_slice_in_dim(a, me * n, n, axis=1).astype(BF16), 128)
    g_ada0 = _matmul(c_act_cols, my_cols(dmod0_all, na), "nn", (F32,), "dada0")[0]
    g_ada1 = _matmul(c_act_cols, my_cols(dmod1_all, na), "nn", (F32,), "dada1")[0]
    g_adak = _matmul(c_act_cols, my_cols(dmodk_all, nk), "nn", (F32,), "dada_kv")[0]
    update("ada_w", g_ada1[None], ada_w, m_ada_w, v_ada_w, layer=1)
    update("ada_w", g_ada0[None], ada_w, m_ada_w, v_ada_w, layer=0, filled=outs["ada_w"])
    update("kv_ada_w", g_adak[None], kv_ada_w, m_kv_ada_w, v_kv_ada_w)
    update("ada_b", jnp.stack([dmod0_all, dmod1_all], axis=1), ada_b, m_ada_b, v_ada_b)
    update("kv_ada_b", dmodk_all, kv_ada_b, m_kv_ada_b, v_kv_ada_b)
    update("norm_mix", jnp.stack([take(3), take(4)], axis=1), norm_mix, m_norm_mix, v_norm_mix)
    update("norm_mlp", jnp.stack([take(5), take(6)], axis=1), norm_mlp, m_norm_mlp, v_norm_mlp)
    update("kv_norm", take(7), kv_norm, m_kv_norm, v_kv_norm)
    update("final_norm", take(8), final_norm, m_final_norm, v_final_norm)
    d_lb_all = take(9).reshape(N_DEV, 2, D)
    update("a_lb_logits", lax.dynamic_slice_in_dim(d_lb_all, me * dl, dl, axis=2), a_lb_logits, m_a_lb_logits,
           v_a_lb_logits)
    update("a_out_gain", lax.dynamic_slice_in_dim(take(10), me * dl, dl, axis=1)[:, None, :], a_out_gain,
           m_a_out_gain, v_a_out_gain)

    def landed(key, after):
        handle, route = sent[key]
        return _copies_wait(handle, route, after, "scatter_" + key + "_wait")

    update("mlp_w2", landed("mlp1_w2", d0), mlp_w2, m_mlp_w2, v_mlp_w2, layer=1)
    update("mlp_w1", landed("mlp1_w1", d0), mlp_w1, m_mlp_w1, v_mlp_w1, layer=1)
    update("b_w_out", landed("b_w_out", d0), b_w_out, m_b_w_out, v_b_w_out)
    update("b_w_q", landed("b_w_q", d0), b_w_q, m_b_w_q, v_b_w_q)
    update("w_kv", landed("w_kv", d0), w_kv, m_w_kv, v_w_kv)
    update("mlp_w2", landed("mlp0_w2", d0), mlp_w2, m_mlp_w2, v_mlp_w2, layer=0, filled=outs["mlp_w2"])
    update("mlp_w1", landed("mlp0_w1", d0), mlp_w1, m_mlp_w1, v_mlp_w1, layer=0, filled=outs["mlp_w1"])
    update("a_w_out", landed("a_w_out", d0), a_w_out, m_a_w_out, v_a_w_out)
    update("a_w_in", landed("a_w_in", outs["mlp_w1"][0]), a_w_in, m_a_w_in, v_a_w_in)

    names = ["ada_w", "ada_b", "norm_mix", "norm_mlp", "a_w_in", "a_lb_logits", "a_out_gain", "a_w_out", "kv_ada_w",
             "kv_ada_b", "kv_norm", "w_kv", "b_w_q", "b_w_out", "mlp_w1", "mlp_w2", "final_norm"]
    result = [loss, d0[None]]
    for field in range(4):
        result += [outs[n][field] for n in names]
    return tuple(result)
```

```python
import jax
import jax.numpy as jnp
from jax import lax
from jax.experimental import pallas as pl
from jax.experimental.pallas import tpu as pltpu

F32 = jnp.float32
BF16 = jnp.bfloat16

N_DEV = 8
MESH_AXES = ("x", "y", "c")
HEAD_DIM = 128
A_CHUNK = 16
HGRN2_UNROLL = 8
HGRN2_GROUP = 128
HGRN2_BLOCK = 1024
SB_TILE = 256
NORM_EPS = 1e-6
ADAM_LR = 0.001
ADAM_B1 = 0.9
ADAM_B2 = 0.999
ADAM_EPS = 1e-08
ADAM_WD = 0.01
ADAM_STEP = 10
VMEM_LIMIT = 56 * 1024 * 1024
MATMUL_VMEM = 40 * 1024 * 1024


def _pick(dim, target, align):
    t = (min(dim, target) // align) * align
    while t >= align:
        if dim % t == 0:
            return t
        t -= align
    return dim


def _sigmoid(x):
    return 1.0 / (1.0 + jnp.exp(-x))


def _silu(x):
    return x * _sigmoid(x)


def _dsilu(x):
    s = _sigmoid(x)
    return s * (1.0 + x * (1.0 - s))


def _dot(a, b, dims):
    return lax.dot_general(a, b, (dims, ((), ())), preferred_element_type=F32)


def _dot_nn(a, b):
    return _dot(a, b, ((1,), (0,)))


def _dot_nt(a, b):
    return _dot(a, b, ((1,), (1,)))


def _dot_tn(a, b):
    return _dot(a, b, ((0,), (0,)))


def _mesh_pos():
    return lax.axis_index("x"), lax.axis_index("y"), lax.axis_index("c")


def _flip(v, d):
    return 1 - v if d else v


def _all_gather(x, name, axis=None):
    if axis is None:
        out_shape, place = (N_DEV,) + x.shape, lambda d: (d,)
    else:
        route = _gather_route(x, axis)
        out_shape, place = route.land_shape, route.dst_slice

    def body(x_ref, out_ref, send_sems, recv_sems, local_sem):
        x, y, c = _mesh_pos()
        me, sibling = (x, y, c), (x, y, 1 - c)
        chips = [(1 - x, y), (x, 1 - y), (1 - x, 1 - y)]

        def rows(px, py, pc):
            return out_ref.at[place(4 * px + 2 * py + pc)]

        def copy(k, block, to, src=None):
            return pltpu.make_async_remote_copy(
                src_ref=rows(*block) if src is None else src,
                dst_ref=rows(*block),
                send_sem=send_sems.at[k],
                recv_sem=recv_sems.at[k],
                device_id=to,
                device_id_type=pl.DeviceIdType.MESH,
            )

        mine = pltpu.make_async_copy(x_ref, rows(*me), local_sem)
        mine.start()
        first = [copy(0, me, sibling, src=x_ref)]
        first += [copy(1 + j, me, (*chip, c), src=x_ref) for j, chip in enumerate(chips)]
        for cp in first:
            cp.start()
        passed = [copy(4 + j, (*chip, c), sibling) for j, chip in enumerate(chips)]
        for j, chip in enumerate(chips):
            copy(1 + j, (*chip, c), me).wait_recv()
            passed[j].start()
        copy(0, sibling, me).wait_recv()
        for j, chip in enumerate(chips):
            copy(4 + j, (*chip, 1 - c), me).wait_recv()
        for cp in first + passed:
            cp.wait_send()
        mine.wait()

    return pl.pallas_call(
        body,
        name=name,
        out_shape=jax.ShapeDtypeStruct(out_shape, x.dtype),
        in_specs=[pl.BlockSpec(memory_space=pl.ANY)],
        out_specs=pl.BlockSpec(memory_space=pl.ANY),
        scratch_shapes=[
            pltpu.SemaphoreType.DMA((7,)),
            pltpu.SemaphoreType.DMA((7,)),
            pltpu.SemaphoreType.DMA(()),
        ],
    )(x)


_RELATIONS = [(dx, dy, dc) for dx in (0, 1) for dy in (0, 1) for dc in (0, 1) if (dx, dy, dc) != (0, 0, 0)]
_HBM = pl.BlockSpec(memory_space=pltpu.HBM)
_SEM = pl.BlockSpec(memory_space=pltpu.SEMAPHORE)
_EFFECT = pltpu.SideEffectType.DATAFLOW_SIDE_EFFECTING


def _at(ref, idx):
    return ref.at[idx] if idx else ref


def _block(axis, ndim, n):
    return lambda d: (slice(None),) * axis + (pl.ds(d * n, n),) + (slice(None),) * (ndim - axis - 1)


class _Route:
    def __init__(self, land_shape, src_slice, dst_slice):
        self.land_shape, self.src_slice, self.dst_slice = tuple(land_shape), src_slice, dst_slice


def _gather_route(shard, axis):
    n = shard.shape[axis]
    shape = shard.shape[:axis] + (N_DEV * n,) + shard.shape[axis + 1:]
    return _Route(shape, lambda p: (), _block(axis, shard.ndim, n))


def _scatter_route(g, axis):
    n = g.shape[axis] // N_DEV
    shape = (N_DEV,) + g.shape[:axis] + (n,) + g.shape[axis + 1:]
    return _Route(shape, _block(axis, g.ndim, n), lambda p: (p,))


def _peers():
    x, y, c = _mesh_pos()
    out = []
    for k, (dx, dy, dc) in enumerate(_RELATIONS):
        px, py, pc = _flip(x, dx), _flip(y, dy), _flip(c, dc)
        out.append((k, (px, py, pc), 4 * px + 2 * py + pc))
    return 4 * x + 2 * y + c, out


_LOCAL = len(_RELATIONS)


def _copies_start(srcs, routes, after, name):
    n = len(srcs)
    order = [] if after is None else [after]

    def body(*refs):
        src_refs, land_refs = refs[:n], refs[n:2 * n]
        outs = refs[2 * n + len(order):]
        send, recv, token = outs[:n], outs[n:2 * n], outs[-1]
        me, peers = _peers()
        for i, route in enumerate(routes):
            for k, peer, pid in peers:
                pltpu.make_async_remote_copy(
                    src_ref=_at(src_refs[i], route.src_slice(pid)), dst_ref=_at(land_refs[i], route.dst_slice(me)),
                    send_sem=send[i].at[k], recv_sem=recv[i].at[k],
                    device_id=peer, device_id_type=pl.DeviceIdType.MESH).start()
            pltpu.make_async_copy(_at(src_refs[i], route.src_slice(me)), _at(land_refs[i], route.dst_slice(me)),
                                  send[i].at[_LOCAL]).start()
        token[...] = jnp.zeros_like(token)

    lands = [lax.empty(r.land_shape, s.dtype) for s, r in zip(srcs, routes)]
    hbm = lambda a: pltpu.with_memory_space_constraint(a, pltpu.HBM)
    res = pl.pallas_call(
        body, name=name,
        out_shape=([pltpu.SemaphoreType.DMA((_LOCAL + 1,))] * n + [pltpu.SemaphoreType.DMA((_LOCAL,))] * n
                   + [pltpu.HBM(s.shape, s.dtype) for s in srcs]
                   + [pltpu.HBM(l.shape, l.dtype) for l in lands]
                   + [jax.ShapeDtypeStruct((8, 128), F32)]),
        in_specs=[_HBM] * (2 * n) + [pl.BlockSpec(memory_space=pl.ANY)] * len(order),
        out_specs=[_SEM] * (2 * n) + [_HBM] * (2 * n) + [pl.BlockSpec(memory_space=pltpu.VMEM)],
        input_output_aliases={i: 2 * n + i for i in range(2 * n)},
        compiler_params=pltpu.CompilerParams(has_side_effects=_EFFECT),
    )(*[hbm(s) for s in srcs], *[hbm(l) for l in lands], *order)
    handles = [(res[i], res[n + i], res[2 * n + i], res[3 * n + i]) for i in range(n)]
    return handles, res[-1]


def _copies_wait(handle, route, after, name):
    send_sems, recv_sems, src, land = handle

    def body(src_ref, land_ref, send_ref, recv_ref, after_ref, src_out, got_ref):
        me, peers = _peers()
        for k, peer, pid in peers:
            cp = pltpu.make_async_remote_copy(
                src_ref=_at(src_ref, route.src_slice(pid)), dst_ref=_at(land_ref, route.dst_slice(pid)),
                send_sem=send_ref.at[k], recv_sem=recv_ref.at[k],
                device_id=peer, device_id_type=pl.DeviceIdType.MESH)
            cp.wait_send()
            cp.wait_recv()
        pltpu.make_async_copy(_at(src_ref, route.src_slice(me)), _at(land_ref, route.dst_slice(me)),
                              send_ref.at[_LOCAL]).wait()

    return pl.pallas_call(
        body, name=name,
        out_shape=(pltpu.HBM(src.shape, src.dtype), pltpu.HBM(land.shape, land.dtype)),
        in_specs=[_HBM, _HBM, _SEM, _SEM, pl.BlockSpec(memory_space=pl.ANY)],
        out_specs=(_HBM, _HBM),
        input_output_aliases={0: 0, 1: 1},
        compiler_params=pltpu.CompilerParams(has_side_effects=_EFFECT),
    )(src, land, send_sems, recv_sems, after)[1]


def _matmul_tiles(M, N, K, a_item, b_item, mn_bytes, whole_rows):
    def fit(tk):
        tm = _pick(M, 512, 128)
        tn = N if whole_rows else _pick(N, 2048, 128)
        while True:
            cast = (tm * tk * 2 if a_item != 2 else 0) + (tk * tn * 2 if b_item != 2 else 0)
            need = 2 * (tm * tk * a_item + tk * tn * b_item + tm * tn * mn_bytes) + 2 * tm * tn * 4 + cast
            if whole_rows:
                smaller = _pick(M, tm - 128, 128) if tm > 128 else tm
                if need <= MATMUL_VMEM or smaller >= tm:
                    return tm, tn, tk
                tm = smaller
            else:
                smaller = _pick(N, tn - 128, 128) if tn > 128 else tn
                if need <= MATMUL_VMEM or smaller >= tn:
                    return tm, tn, tk
                tn = smaller

    shallow, deep = fit(_pick(K, 2048, 128)), fit(_pick(K, 4096, 128))
    return deep if deep[:2] == shallow[:2] else shallow


def _matmul(a, b, mode, out_dtypes, name, epilogue=None, tiles=(), rows=(), whole_rows=False):
    if mode == "nn":
        (M, K), (K2, N) = a.shape, b.shape
    elif mode == "nt":
        (M, K), (N, K2) = a.shape, b.shape
    else:
        (K, M), (K2, N) = a.shape, b.shape
    assert K == K2, (a.shape, b.shape, mode)
    mn_bytes = sum(t.dtype.itemsize for t in tiles) + sum(jnp.dtype(d).itemsize for d in out_dtypes)
    tm, tn, tk = _matmul_tiles(M, N, K, a.dtype.itemsize, b.dtype.itemsize, mn_bytes, whole_rows)
    nm, nn, nk = M // tm, N // tn, K // tk
    n_extra = len(tiles) + len(rows)
    n_out = len(out_dtypes)
    if epilogue is None:
        epilogue = lambda acc: (acc,)
    dims = {"nn": ((1,), (0,)), "nt": ((1,), (1,)), "tn": ((0,), (0,))}[mode]

    def finish(acc, extra, outs):
        vals = epilogue(acc, *[r[...] for r in extra])
        for o_ref, v in zip(outs, vals):
            o_ref[...] = v.astype(o_ref.dtype)

    def kern(*refs):
        a_ref, b_ref = refs[0], refs[1]
        extra = refs[2:2 + n_extra]
        outs = refs[2 + n_extra:2 + n_extra + n_out]
        part = _dot(a_ref[...].astype(BF16), b_ref[...].astype(BF16), dims)
        if nk == 1:
            finish(part, extra, outs)
            return
        acc_ref = refs[-1]
        k = pl.program_id(2)

        @pl.when(k == 0)
        def _():
            acc_ref[...] = part

        @pl.when(k > 0)
        def _():
            acc_ref[...] += part

        @pl.when(k == nk - 1)
        def _():
            finish(acc_ref[...], extra, outs)

    a_bytes, b_bytes = a.size * a.dtype.itemsize, b.size * b.dtype.itemsize
    m_outer = a_bytes + nm * b_bytes <= nn * a_bytes + b_bytes
    ij = (lambda g0, g1: (g0, g1)) if m_outer else (lambda g0, g1: (g1, g0))

    def spec(shape, fn):
        return pl.BlockSpec(shape, lambda g0, g1, k: fn(*ij(g0, g1), k))

    a_spec = {"nn": spec((tm, tk), lambda i, j, k: (i, k)),
              "nt": spec((tm, tk), lambda i, j, k: (i, k)),
              "tn": spec((tk, tm), lambda i, j, k: (k, i))}[mode]
    b_spec = {"nn": spec((tk, tn), lambda i, j, k: (k, j)),
              "nt": spec((tn, tk), lambda i, j, k: (j, k)),
              "tn": spec((tk, tn), lambda i, j, k: (k, j))}[mode]
    tile_spec = spec((tm, tn), lambda i, j, k: (i, j))
    row_spec = spec((1, tn), lambda i, j, k: (0, j))
    return pl.pallas_call(
        kern,
        name=name,
        grid=(nm, nn, nk) if m_outer else (nn, nm, nk),
        in_specs=[a_spec, b_spec] + [tile_spec] * len(tiles) + [row_spec] * len(rows),
        out_specs=[tile_spec] * n_out,
        out_shape=[jax.ShapeDtypeStruct((M, N), dt) for dt in out_dtypes],
        scratch_shapes=[pltpu.VMEM((tm, tn), F32)] if nk > 1 else [],
        compiler_params=pltpu.CompilerParams(
            dimension_semantics=("parallel", "parallel", "arbitrary"), vmem_limit_bytes=VMEM_LIMIT),
    )(a, b, *tiles, *rows)


def _relu2_epilogue(acc):
    a = jnp.maximum(acc, 0.0)
    return a * a, a


def _modulated_norm(x, gain, sc, sh):
    inv = lax.rsqrt(jnp.mean(x * x, axis=-1, keepdims=True) + NORM_EPS)
    return (x * inv) * gain * (1.0 + sc) + sh


def _residual_epilogue(acc, x_tile, gate_row, *norm_rows):
    x_new = x_tile + gate_row * acc
    normed = [_modulated_norm(x_new, *norm_rows[i:i + 3]) for i in range(0, len(norm_rows), 3)]
    return (x_new, acc, *normed)


def _norm_mod_fwd(x, gain, sc, sh, name):
    S, D = x.shape
    ts = _pick(S, 256, 16)

    def kern(x_ref, g_ref, sc_ref, sh_ref, h_ref):
        h_ref[...] = _modulated_norm(x_ref[...], g_ref[...], sc_ref[...], sh_ref[...]).astype(h_ref.dtype)

    row = pl.BlockSpec((1, D), lambda i: (0, 0))
    blk = pl.BlockSpec((ts, D), lambda i: (i, 0))
    return pl.pallas_call(
        kern, name=name, grid=(S // ts,),
        in_specs=[blk, row, row, row], out_specs=blk,
        out_shape=jax.ShapeDtypeStruct((S, D), BF16),
        compiler_params=pltpu.CompilerParams(dimension_semantics=("parallel",), vmem_limit_bytes=VMEM_LIMIT),
    )(x, gain, sc, sh)


def _through_gate(d, y_ref, gate_ref, dy_ref, dgate_ref):
    dy_ref[...] = (gate_ref[...] * d).astype(dy_ref.dtype)
    dgate_ref[...] += jnp.sum(d * y_ref[...].astype(F32), axis=0, keepdims=True)


def _norm_mod_bwd(x, gain, sc, dh, d_in, name, y=None, gate=None):
    S, D = x.shape
    ts = _pick(S, 256, 16)
    gated = y is not None

    def kern(x_ref, g_ref, sc_ref, dh_ref, din_ref, *rest):
        dx_ref, dg_ref, dsc_ref, dsh_ref = rest[-6:-2] if gated else rest

        @pl.when(pl.program_id(0) == 0)
        def _():
            dg_ref[...] = jnp.zeros_like(dg_ref)
            dsc_ref[...] = jnp.zeros_like(dsc_ref)
            dsh_ref[...] = jnp.zeros_like(dsh_ref)
            if gated:
                rest[-1][...] = jnp.zeros_like(rest[-1])

        xv = x_ref[...]
        dh = dh_ref[...].astype(F32)
        inv = lax.rsqrt(jnp.mean(xv * xv, axis=-1, keepdims=True) + NORM_EPS)
        xn = xv * inv
        g = g_ref[...]
        dsh_ref[...] += jnp.sum(dh, axis=0, keepdims=True)
        dsc_ref[...] += jnp.sum(dh * (xn * g), axis=0, keepdims=True)
        dn = dh * (1.0 + sc_ref[...])
        dg_ref[...] += jnp.sum(dn * xn, axis=0, keepdims=True)
        dxn = dn * g
        d = din_ref[...] + inv * (dxn - xn * jnp.mean(dxn * xn, axis=-1, keepdims=True))
        dx_ref[...] = d
        if gated:
            _through_gate(d, rest[0], rest[1], rest[-2], rest[-1])

    row = pl.BlockSpec((1, D), lambda i: (0, 0))
    blk = pl.BlockSpec((ts, D), lambda i: (i, 0))
    vec = jax.ShapeDtypeStruct((1, D), F32)
    full = lambda dt: jax.ShapeDtypeStruct((S, D), dt)
    return pl.pallas_call(
        kern, name=name, grid=(S // ts,),
        in_specs=[blk, row, row, blk, blk] + ([blk, row] if gated else []),
        out_specs=[blk, row, row, row] + ([blk, row] if gated else []),
        out_shape=[full(F32), vec, vec, vec] + ([full(BF16), vec] if gated else []),
        compiler_params=pltpu.CompilerParams(dimension_semantics=("arbitrary",), vmem_limit_bytes=VMEM_LIMIT),
    )(x, gain, sc, dh, d_in, *((y, gate) if gated else ()))


def _final_loss(x, gain, target, y, gate, name):
    S, D = x.shape
    ts = _pick(S, 256, 16)

    def kern(x_ref, g_ref, t_ref, y_ref, gate_ref, loss_ref, dx_ref, dg_ref, dy_ref, dgate_ref):
        @pl.when(pl.program_id(0) == 0)
        def _():
            loss_ref[...] = jnp.zeros_like(loss_ref)
            dg_ref[...] = jnp.zeros_like(dg_ref)
            dgate_ref[...] = jnp.zeros_like(dgate_ref)

        xv = x_ref[...]
        g = g_ref[...]
        inv = lax.rsqrt(jnp.mean(xv * xv, axis=-1, keepdims=True) + NORM_EPS)
        xn = xv * inv
        err = xn * g - t_ref[...]
        row_loss = jnp.mean(err * err, axis=-1, keepdims=True)
        loss_ref[...] += 0.5 * jnp.sum(row_loss, axis=0, keepdims=True)
        dout = err * (1.0 / D)
        dg_ref[...] += jnp.sum(dout * xn, axis=0, keepdims=True)
        dxn = dout * g
        d = inv * (dxn - xn * jnp.mean(dxn * xn, axis=-1, keepdims=True))
        dx_ref[...] = d
        _through_gate(d, y_ref, gate_ref, dy_ref, dgate_ref)

    row = pl.BlockSpec((1, D), lambda i: (0, 0))
    blk = pl.BlockSpec((ts, D), lambda i: (i, 0))
    vec = jax.ShapeDtypeStruct((1, D), F32)
    return pl.pallas_call(
        kern, name=name, grid=(S // ts,),
        in_specs=[blk, row, blk, blk, row],
        out_specs=[pl.BlockSpec((1, 128), lambda i: (0, 0)), blk, row, blk, row],
        out_shape=[jax.ShapeDtypeStruct((1, 128), F32), jax.ShapeDtypeStruct((S, D), F32), vec,
                   jax.ShapeDtypeStruct((S, D), BF16), vec],
        compiler_params=pltpu.CompilerParams(dimension_semantics=("arbitrary",), vmem_limit_bytes=VMEM_LIMIT),
    )(x, gain, target, y, gate)


def _regroup_columns(w, outer, inner, name):
    rows = w.shape[0]

    def kern(i_ref, o_ref):
        o_ref[...] = i_ref[...]

    return pl.pallas_call(
        kern, name=name, grid=(outer, inner),
        in_specs=[pl.BlockSpec((rows, HEAD_DIM), lambda a, b: (0, a * inner + b))],
        out_specs=pl.BlockSpec((rows, HEAD_DIM), lambda a, b: (0, b * outer + a)),
        out_shape=jax.ShapeDtypeStruct(w.shape, w.dtype),
        compiler_params=pltpu.CompilerParams(
            dimension_semantics=("parallel", "parallel"), vmem_limit_bytes=VMEM_LIMIT),
    )(w)


def _heads_major(w, name):
    return _regroup_columns(w, 4, w.shape[1] // (4 * HEAD_DIM), name)


def _heads_minor(w, name):
    return _regroup_columns(w, w.shape[1] // (4 * HEAD_DIM), 4, name)


def _part(ref, j, rows=slice(None)):
    return ref[rows, j * HEAD_DIM:(j + 1) * HEAD_DIM]


def _chunk_scan(v, ric, reverse):
    n = v.shape[0]
    d = 1
    while d < A_CHUNK:
        if reverse:
            v = v + jnp.where(ric < A_CHUNK - d, pltpu.roll(v, n - d, 0), 0.0)
        else:
            v = v + jnp.where(ric >= d, pltpu.roll(v, d, 0), 0.0)
        d *= 2
    return v


def _hgrn2_gates(q_raw, f_logit, lbv, ric):
    qs = _silu(q_raw)
    sig = _sigmoid(f_logit)
    f = lbv + (1.0 - lbv) * sig
    log_f = jnp.log(f)
    kk = (1.0 - lbv) * _sigmoid(-f_logit)
    cum = _chunk_scan(log_f, ric, False)
    cl = cum + _chunk_scan(log_f, ric, True) - log_f
    e_cum = jnp.exp(cum)
    e_neg = jnp.exp(-cum)
    e_end = jnp.exp(cl - cum)
    dec = jnp.exp(cl)
    return qs, sig, f, kk, e_cum, e_neg, e_end, dec


def _same_chunk_mask():
    r = lax.broadcasted_iota(jnp.int32, (HGRN2_GROUP, HGRN2_GROUP), 0)
    c = lax.broadcasted_iota(jnp.int32, (HGRN2_GROUP, HGRN2_GROUP), 1)
    return (r // A_CHUNK == c // A_CHUNK) & (r >= c)


def _hgrn2_specs(S, D, TB, reverse):
    H = D // HEAD_DIM
    NB = S // TB
    pos = (lambda nb: NB - 1 - nb) if reverse else (lambda nb: nb)
    proj = pl.BlockSpec((TB, 4 * HEAD_DIM), lambda h, nb: (pos(nb), h))
    head = pl.BlockSpec((TB, HEAD_DIM), lambda h, nb: (pos(nb), h))
    vec = pl.BlockSpec((1, HEAD_DIM), lambda h, nb: (0, h))
    state = pl.BlockSpec((TB // A_CHUNK, None, HEAD_DIM, HEAD_DIM), lambda h, nb: (pos(nb), h, 0, 0))
    return H, NB, proj, head, vec, state


def _hgrn2_fwd(proj, lb, gain, name):
    S, D4 = proj.shape
    D = D4 // 4
    TB = _pick(S, HGRN2_BLOCK, HGRN2_GROUP)
    H, NB, pspec, head, vec, state = _hgrn2_specs(S, D, TB, False)
    NCB = TB // A_CHUNK

    def kern(p_ref, lb_ref, gain_ref, og_ref, oraw_ref, st_ref, a_s, b_s, k_s, v_s, dec_s, o_s, st_s):
        @pl.when(pl.program_id(1) == 0)
        def _():
            st_s[...] = jnp.zeros_like(st_s)

        ric = lax.broadcasted_iota(jnp.int32, (TB, HEAD_DIM), 0) % A_CHUNK
        qs, _, _, kk, e_cum, e_neg, e_end, dec = _hgrn2_gates(_part(p_ref, 0), _part(p_ref, 1), lb_ref[...], ric)
        a_s[...] = (qs * e_cum).astype(BF16)
        b_s[...] = (kk * e_neg).astype(BF16)
        k_s[...] = (kk * e_end).astype(BF16)
        v_s[...] = _part(p_ref, 2).astype(BF16)
        dec_s[...] = dec

        same_chunk = _same_chunk_mask()
        for gi in range(TB // HGRN2_GROUP):
            rows = pl.ds(gi * HGRN2_GROUP, HGRN2_GROUP)
            p = jnp.where(same_chunk, _dot_nt(a_s[rows, :], b_s[rows, :]), 0.0).astype(BF16)
            o_s[rows, :] = _dot_nn(p, v_s[rows, :])

        def chunks(it, st):
            for u in range(HGRN2_UNROLL):
                ci = it * HGRN2_UNROLL + u
                r = pl.multiple_of(ci * A_CHUNK, A_CHUNK)
                rows = pl.ds(r, A_CHUNK)
                st_bf = st.astype(BF16)
                st_ref[ci] = st_bf
                o_s[rows, :] += _dot_nt(a_s[rows, :], st_bf)
                st = dec_s[pl.ds(r, 1), :] * st + _dot_tn(v_s[rows, :], k_s[rows, :])
            return st

        st_s[...] = lax.fori_loop(0, NCB // HGRN2_UNROLL, chunks, st_s[...])
        o = o_s[...]
        oraw_ref[...] = o
        on = o * lax.rsqrt(jnp.mean(o * o, axis=-1, keepdims=True) + NORM_EPS)
        og_ref[...] = ((on * gain_ref[...]) * _silu(_part(p_ref, 3))).astype(og_ref.dtype)

    tb_bf = pltpu.VMEM((TB, HEAD_DIM), BF16)
    tb_f = pltpu.VMEM((TB, HEAD_DIM), F32)
    return pl.pallas_call(
        kern, name=name, grid=(H, NB),
        in_specs=[pspec, vec, vec],
        out_specs=[head, head, state],
        out_shape=[jax.ShapeDtypeStruct((S, D), BF16), jax.ShapeDtypeStruct((S, D), F32),
                   jax.ShapeDtypeStruct((S // A_CHUNK, H, HEAD_DIM, HEAD_DIM), BF16)],
        scratch_shapes=[tb_bf, tb_bf, tb_bf, tb_bf, tb_f, tb_f, pltpu.VMEM((HEAD_DIM, HEAD_DIM), F32)],
        compiler_params=pltpu.CompilerParams(
            dimension_semantics=("parallel", "arbitrary"), vmem_limit_bytes=VMEM_LIMIT),
    )(proj, lb, gain)


def _hgrn2_bwd(proj, lb, gain, oraw, states, dog, name):
    S, D4 = proj.shape
    D = D4 // 4
    TB = _pick(S, HGRN2_BLOCK, HGRN2_GROUP)
    H, NB, pspec, head, vec, state = _hgrn2_specs(S, D, TB, True)
    NCB = TB // A_CHUNK

    def kern(p_ref, lb_ref, gain_ref, oraw_ref, st_ref, dog_ref, dp_ref, dlb_ref, dgain_ref,
             a_s, b_s, k_s, v_s, do_s, dec_s, da_s, db_s, dk_s, dv_s, ddec_s, dst_s):
        @pl.when(pl.program_id(1) == 0)
        def _():
            dst_s[...] = jnp.zeros_like(dst_s)
            dlb_ref[...] = jnp.zeros_like(dlb_ref)
            dgain_ref[...] = jnp.zeros_like(dgain_ref)

        ric = lax.broadcasted_iota(jnp.int32, (TB, HEAD_DIM), 0) % A_CHUNK
        lbv = lb_ref[...]
        q_raw = _part(p_ref, 0)
        qs, sig, f, kk, e_cum, e_neg, e_end, dec = _hgrn2_gates(q_raw, _part(p_ref, 1), lbv, ric)
        a32, b32, k32 = qs * e_cum, kk * e_neg, kk * e_end
        a_s[...] = a32.astype(BF16)
        b_s[...] = b32.astype(BF16)
        k_s[...] = k32.astype(BF16)
        v_s[...] = _part(p_ref, 2).astype(BF16)
        dec_s[...] = dec

        o = oraw_ref[...]
        gain_v = gain_ref[...]
        g_raw = _part(p_ref, 3)
        rinv = lax.rsqrt(jnp.mean(o * o, axis=-1, keepdims=True) + NORM_EPS)
        on = o * rinv
        dog_v = dog_ref[...].astype(F32)
        dp_ref[:, 3 * HEAD_DIM:4 * HEAD_DIM] = (dog_v * (on * gain_v) * _dsilu(g_raw)).astype(dp_ref.dtype)
        dog2 = dog_v * _silu(g_raw)
        dgain_ref[...] += jnp.sum(dog2 * on, axis=0, keepdims=True)
        don = dog2 * gain_v
        do_s[...] = (rinv * (don - on * jnp.mean(don * on, axis=-1, keepdims=True))).astype(BF16)

        same_chunk = _same_chunk_mask()
        for gi in range(TB // HGRN2_GROUP):
            rows = pl.ds(gi * HGRN2_GROUP, HGRN2_GROUP)
            a, b, do = a_s[rows, :], b_s[rows, :], do_s[rows, :]
            p = jnp.where(same_chunk, _dot_nt(a, b), 0.0).astype(BF16)
            dp = jnp.where(same_chunk, _dot_nt(do, v_s[rows, :]), 0.0).astype(BF16)
            dv_s[rows, :] = _dot_tn(p, do)
            da_s[rows, :] = _dot_nn(dp, b)
            db_s[rows, :] = _dot_tn(dp, a)

        def chunks(it, dst):
            for u in range(HGRN2_UNROLL):
                ci = NCB - 1 - (it * HGRN2_UNROLL + u)
                r = pl.multiple_of(ci * A_CHUNK, A_CHUNK)
                rows = pl.ds(r, A_CHUNK)
                do = do_s[rows, :]
                st_prev = st_ref[ci]
                dst_bf = dst.astype(BF16)
                dv_s[rows, :] += _dot_nt(k_s[rows, :], dst_bf)
                da_s[rows, :] += _dot_nn(do, st_prev)
                dk_s[rows, :] = _dot_nn(v_s[rows, :], dst_bf)
                ddec = jnp.sum(dst * st_prev.astype(F32), axis=0, keepdims=True)
                ddec_s[rows, :] = jnp.broadcast_to(ddec, (A_CHUNK, HEAD_DIM))
                dst = dec_s[pl.ds(r, 1), :] * dst + _dot_tn(do, a_s[rows, :])
            return dst

        dst_s[...] = lax.fori_loop(0, NCB // HGRN2_UNROLL, chunks, dst_s[...])

        da, db, dk = da_s[...], db_s[...], dk_s[...]
        dqs = da * e_cum
        dkk = db * e_neg + dk * e_end
        w = dk * k32
        dlog_f = (_chunk_scan(da * a32 - db * b32, ric, True) + (_chunk_scan(w, ric, False) - w)
                  + ddec_s[...] * dec)
        dfg = dlog_f / f - dkk
        dlb_ref[...] += jnp.sum(dfg * (1.0 - sig), axis=0, keepdims=True)
        dp_ref[:, 0:HEAD_DIM] = (dqs * _dsilu(q_raw)).astype(dp_ref.dtype)
        dp_ref[:, HEAD_DIM:2 * HEAD_DIM] = (dfg * (1.0 - lbv) * sig * (1.0 - sig)).astype(dp_ref.dtype)
        dp_ref[:, 2 * HEAD_DIM:3 * HEAD_DIM] = dv_s[...].astype(dp_ref.dtype)

    tb_bf = pltpu.VMEM((TB, HEAD_DIM), BF16)
    tb_f = pltpu.VMEM((TB, HEAD_DIM), F32)
    vec_shape = jax.ShapeDtypeStruct((1, D), F32)
    return pl.pallas_call(
        kern, name=name, grid=(H, NB),
        in_specs=[pspec, vec, vec, head, state, head],
        out_specs=[pspec, vec, vec],
        out_shape=[jax.ShapeDtypeStruct((S, D4), BF16), vec_shape, vec_shape],
        scratch_shapes=[tb_bf, tb_bf, tb_bf, tb_bf, tb_bf, tb_f, tb_f, tb_f, tb_f, tb_f, tb_f,
                        pltpu.VMEM((HEAD_DIM, HEAD_DIM), F32)],
        compiler_params=pltpu.CompilerParams(
            dimension_semantics=("parallel", "arbitrary"), vmem_limit_bytes=VMEM_LIMIT),
    )(proj, lb, gain, oraw, states, dog)


def _with_ones(tri):
    return jnp.concatenate([tri, jnp.ones_like(tri)], axis=1)


def _stack(ref, G):
    return jnp.concatenate([ref[:, g * HEAD_DIM:(g + 1) * HEAD_DIM] for g in range(G)], axis=0)


def _unstack(v, G):
    return jnp.concatenate([v[g * SB_TILE:(g + 1) * SB_TILE, :] for g in range(G)], axis=1)


def _sb_specs(S, D, KVH):
    G = D // HEAD_DIM // KVH
    qblk = pl.BlockSpec((SB_TILE, G * HEAD_DIM), lambda h, qi: (qi, h))
    kblk = pl.BlockSpec((S, HEAD_DIM), lambda h, qi: (0, h))
    vblk = pl.BlockSpec((S, HEAD_DIM), lambda h, qi: (0, KVH + h))
    return G, qblk, kblk, vblk


def _tile_index(qi, j):
    return qi * (qi + 1) // 2 + j


def _sb_fwd(q, kv, name):
    S, D = q.shape
    KVH = kv.shape[1] // (2 * HEAD_DIM)
    G, qblk, kblk, vblk = _sb_specs(S, D, KVH)
    R = G * SB_TILE
    NQ = S // SB_TILE
    scale = HEAD_DIM ** -0.5

    def kern(q_ref, k_ref, v_ref, o_ref, w_hbm, b_hbm, acc_s, run_s, w_stage, b_stage, sems):
        head, qi = pl.program_id(0), pl.program_id(1)
        qs = _stack(q_ref, G)
        row = lax.broadcasted_iota(jnp.int32, (R, SB_TILE), 0) % SB_TILE
        colm = lax.broadcasted_iota(jnp.int32, (R, SB_TILE), 1)
        mask = colm < row
        ti = lax.broadcasted_iota(jnp.int32, (SB_TILE, SB_TILE), 0)
        tj = lax.broadcasted_iota(jnp.int32, (SB_TILE, SB_TILE), 1)
        after = _with_ones((ti > tj).astype(BF16))
        acc_s[...] = jnp.zeros_like(acc_s)
        run_s[...] = jnp.zeros_like(run_s)

        def save(slot, j):
            idx = _tile_index(qi, j)
            return (pltpu.make_async_copy(w_stage.at[slot], w_hbm.at[head, idx], sems.at[0, slot]),
                    pltpu.make_async_copy(b_stage.at[slot], b_hbm.at[head, idx], sems.at[1, slot]))

        def tile(j, slot, masked):
            ks = pl.ds(pl.multiple_of(j * SB_TILE, SB_TILE), SB_TILE)
            kj, vj = k_ref[ks, :], v_ref[ks, :]
            z = _dot_nt(qs, kj) * scale
            t = jnp.exp(-jnp.abs(z))
            one_t = 1.0 + t
            log_beta = jnp.minimum(z, 0.0) - jnp.log(one_t)
            log_rest = log_beta - z
            if masked:
                log_rest = jnp.where(mask, log_rest, 0.0)
            sums = _dot_nn(log_rest.astype(BF16), after)
            between = sums[:, :SB_TILE] + run_s[...]
            w = jnp.exp(log_beta + between)
            if masked:
                w = jnp.where(mask, w, 0.0)
            w_bf = w.astype(BF16)
            acc_s[...] += _dot_nn(w_bf, vj)
            run_s[...] += sums[:, SB_TILE:]
            w_stage[slot] = w_bf
            b_stage[slot] = (jnp.where(z >= 0.0, 1.0, t) * pl.reciprocal(one_t, approx=True)).astype(BF16)
            for cp in save(slot, j):
                cp.start()

        tile(qi, 0, True)

        def body(it, carry):
            slot = (it + 1) % 2

            @pl.when(it >= 1)
            def _():
                for cp in save(slot, 0):
                    cp.wait()

            tile(qi - 1 - it, slot, False)
            return carry

        lax.fori_loop(0, qi, body, 0)
        o_ref[...] = _unstack(acc_s[...], G).astype(o_ref.dtype)
        for cp in save(qi % 2, 0):
            cp.wait()

        @pl.when(qi >= 1)
        def _():
            for cp in save((qi + 1) % 2, 0):
                cp.wait()

    tiles = jax.ShapeDtypeStruct((KVH, NQ * (NQ + 1) // 2, R, SB_TILE), BF16)
    stage = pltpu.VMEM((2, R, SB_TILE), BF16)
    return pl.pallas_call(
        kern, name=name, grid=(KVH, NQ),
        in_specs=[qblk, kblk, vblk],
        out_specs=[qblk, pl.BlockSpec(memory_space=pl.ANY), pl.BlockSpec(memory_space=pl.ANY)],
        out_shape=[jax.ShapeDtypeStruct((S, D), BF16), tiles, tiles],
        scratch_shapes=[pltpu.VMEM((R, HEAD_DIM), F32), pltpu.VMEM((R, SB_TILE), F32), stage, stage,
                        pltpu.SemaphoreType.DMA((2, 2))],
        compiler_params=pltpu.CompilerParams(
            dimension_semantics=("parallel", "arbitrary"), vmem_limit_bytes=VMEM_LIMIT),
    )(q, kv, kv)


def _sb_bwd(q, kv, w_tiles, b_tiles, do, after, name):
    S, D = q.shape
    KVH = kv.shape[1] // (2 * HEAD_DIM)
    G, qblk, kblk, vblk = _sb_specs(S, D, KVH)
    R = G * SB_TILE
    scale = HEAD_DIM ** -0.5

    def kern(q_ref, k_ref, v_ref, w_hbm, b_hbm, do_ref, after_ref, dq_ref, dk_ref, dv_ref,
             dq_s, esum_s, w_stage, b_stage, sems):
        head, qi = pl.program_id(0), pl.program_id(1)

        @pl.when(qi == 0)
        def _():
            dk_ref[...] = jnp.zeros_like(dk_ref)
            dv_ref[...] = jnp.zeros_like(dv_ref)

        qs = _stack(q_ref, G)
        dos = _stack(do_ref, G)
        row = lax.broadcasted_iota(jnp.int32, (R, SB_TILE), 0) % SB_TILE
        colm = lax.broadcasted_iota(jnp.int32, (R, SB_TILE), 1)
        mask = colm < row
        ti = lax.broadcasted_iota(jnp.int32, (SB_TILE, SB_TILE), 0)
        tj = lax.broadcasted_iota(jnp.int32, (SB_TILE, SB_TILE), 1)
        before = (ti < tj).astype(BF16)
        dq_s[...] = jnp.zeros_like(dq_s)
        esum_s[...] = jnp.zeros_like(esum_s)

        def fetch(slot, j):
            idx = _tile_index(qi, j)
            return (pltpu.make_async_copy(w_hbm.at[head, idx], w_stage.at[slot], sems.at[0, slot]),
                    pltpu.make_async_copy(b_hbm.at[head, idx], b_stage.at[slot], sems.at[1, slot]))

        def tile(j, slot, masked):
            for cp in fetch(slot, j):
                cp.wait()
            ks = pl.ds(pl.multiple_of(j * SB_TILE, SB_TILE), SB_TILE)
            kj, vj = k_ref[ks, :], v_ref[ks, :]
            w_bf = w_stage[slot]
            beta = b_stage[slot].astype(F32)
            e = _dot_nt(dos, vj) * w_bf.astype(F32)
            e_before = esum_s[...] + _dot_nn(e.astype(BF16), before)
            dz = e - beta * (e + e_before)
            if masked:
                dz = jnp.where(mask, dz, 0.0)
            dzs = (dz * scale).astype(BF16)
            dq_s[...] += _dot_nn(dzs, kj)
            dk_ref[ks, :] += _dot_tn(dzs, qs)
            dv_ref[ks, :] += _dot_tn(w_bf, dos)
            esum_s[...] += jnp.sum(e, axis=1, keepdims=True)

        for cp in fetch(0, 0):
            cp.start()

        def body(j, carry):
            for cp in fetch((j + 1) % 2, j + 1):
                cp.start()
            tile(j, j % 2, False)
            return carry

        lax.fori_loop(0, qi, body, 0)
        tile(qi, qi % 2, True)
        dq_ref[...] = _unstack(dq_s[...], G).astype(dq_ref.dtype)

    kvout = pl.BlockSpec((S, HEAD_DIM), lambda h, qi: (0, h))
    stage = pltpu.VMEM((2, R, SB_TILE), BF16)
    hbm = pl.BlockSpec(memory_space=pl.ANY)
    return pl.pallas_call(
        kern, name=name, grid=(KVH, S // SB_TILE),
        in_specs=[qblk, kblk, vblk, hbm, hbm, qblk, pl.BlockSpec(after.shape, lambda h, qi: (0, 0))],
        out_specs=[qblk, kvout, kvout],
        out_shape=[jax.ShapeDtypeStruct((S, D), BF16), jax.ShapeDtypeStruct((S, KVH * HEAD_DIM), F32),
                   jax.ShapeDtypeStruct((S, KVH * HEAD_DIM), F32)],
        scratch_shapes=[pltpu.VMEM((R, HEAD_DIM), F32), pltpu.VMEM((R, SB_TILE), F32), stage, stage,
                        pltpu.SemaphoreType.DMA((2, 2))],
        compiler_params=pltpu.CompilerParams(
            dimension_semantics=("parallel", "arbitrary"), vmem_limit_bytes=VMEM_LIMIT),
    )(q, kv, kv, w_tiles, b_tiles, do, after)


def _adamw(parts, w, m, v, name, layer=None, filled=None):
    shape = w.shape
    L = 1 if layer is None else shape[0]
    l = 0 if layer is None else layer
    C = shape[-1]
    R = w.size // (C * L)
    P = parts.shape[0]
    parts3 = parts.reshape(P, R, C)
    w3, m3, v3 = w.reshape(L, R, C), m.reshape(L, R, C), v.reshape(L, R, C)
    tr = _pick(R, max(8, (1 << 18) // C), 16)

    def kern(p_ref, w_ref, m_ref, v_ref, *rest):
        g_ref, d_ref, nm_ref, nv_ref = rest[-4:]
        g = p_ref[0].astype(F32)
        for i in range(1, P):
            g = g + p_ref[i].astype(F32)
        nm = ADAM_B1 * m_ref[...] + (1.0 - ADAM_B1) * g
        nv = ADAM_B2 * v_ref[...] + (1.0 - ADAM_B2) * (g * g)
        m_hat = nm / (1.0 - ADAM_B1 ** ADAM_STEP)
        v_hat = nv / (1.0 - ADAM_B2 ** ADAM_STEP)
        g_ref[...] = g
        d_ref[...] = -ADAM_LR * (m_hat / (jnp.sqrt(v_hat) + ADAM_EPS) + ADAM_WD * w_ref[...])
        nm_ref[...] = nm
        nv_ref[...] = nv

    blk = pl.BlockSpec((None, tr, C), lambda i: (l, i, 0))
    out = jax.ShapeDtypeStruct((L, R, C), F32)
    extra = [] if filled is None else [f.reshape(L, R, C) for f in filled]
    res = pl.pallas_call(
        kern, name=name, grid=(R // tr,),
        in_specs=[pl.BlockSpec((P, tr, C), lambda i: (0, i, 0)), blk, blk, blk]
        + [pl.BlockSpec(memory_space=pl.ANY)] * len(extra),
        out_specs=[blk, blk, blk, blk], out_shape=[out, out, out, out],
        input_output_aliases={4 + j: j for j in range(len(extra))},
        compiler_params=pltpu.CompilerParams(dimension_semantics=("parallel",), vmem_limit_bytes=VMEM_LIMIT),
    )(parts3, w3, m3, v3, *extra)
    return tuple(r.reshape(shape) for r in res)


def _lower_bound(logits):
    return jnp.cumsum(jax.nn.softmax(logits.astype(F32), axis=0), axis=0)[0:1]


def _pad_rows(a, rows):
    return jnp.zeros((rows, a.shape[1]), a.dtype).at[:a.shape[0]].set(a)


def kernel(x, c, ada_w, ada_b, norm_mix, norm_mlp, a_w_in, a_lb_logits, a_out_gain, a_w_out, kv_ada_w, kv_ada_b, kv_norm, w_kv, b_w_q, b_w_out, mlp_w1, mlp_w2, final_norm, loss_target, m_ada_w, m_ada_b, m_norm_mix, m_norm_mlp, m_a_w_in, m_a_lb_logits, m_a_out_gain, m_a_w_out, m_kv_ada_w, m_kv_ada_b, m_kv_norm, m_w_kv, m_b_w_q, m_b_w_out, m_mlp_w1, m_mlp_w2, m_final_norm, v_ada_w, v_ada_b, v_norm_mix, v_norm_mlp, v_a_w_in, v_a_lb_logits, v_a_out_gain, v_a_w_out, v_kv_ada_w, v_kv_ada_b, v_kv_norm, v_w_kv, v_b_w_q, v_b_w_out, v_mlp_w1, v_mlp_w2, v_final_norm):
    S, D = x.shape[1], x.shape[2]
    assert x.shape[0] == 1 and ada_w.shape[0] == 2 and a_w_in.shape[0] == 1 and b_w_q.shape[0] == 1
    me = 4 * lax.axis_index("x") + 2 * lax.axis_index("y") + lax.axis_index("c")
    dl = D // N_DEV
    na = ada_w.shape[2]
    nk = kv_ada_w.shape[1]

    small = jnp.concatenate([c.reshape(1, D), a_lb_logits.reshape(1, 2 * dl), a_out_gain.reshape(1, dl)], axis=1)
    small_all = _all_gather(small, "gather_small")[:, 0, :]
    c_all = small_all[:, :D]
    lb_logits = small_all[:, D:D + 2 * dl].reshape(N_DEV, 2, dl).transpose(1, 0, 2).reshape(2, D)
    out_gain = small_all[:, D + 2 * dl:].reshape(1, D)

    c_act = jax.nn.silu(c_all)
    c_act_rows = _pad_rows(c_act.astype(BF16), 128)
    mod_cols = jnp.concatenate([
        _matmul(c_act_rows, ada_w[0].astype(BF16), "nn", (F32,), "ada0")[0][:N_DEV],
        _matmul(c_act_rows, ada_w[1].astype(BF16), "nn", (F32,), "ada1")[0][:N_DEV],
        _matmul(c_act_rows, kv_ada_w.astype(BF16), "nn", (F32,), "ada_kv")[0][:N_DEV]], axis=1)
    mod_all = _all_gather(mod_cols, "gather_mod")
    mod_mine = lax.dynamic_index_in_dim(mod_all, me, axis=1, keepdims=False)
    mod0 = mod_mine[:, :na].reshape(1, 6 * D) + ada_b[0:1]
    mod1 = mod_mine[:, na:2 * na].reshape(1, 6 * D) + ada_b[1:2]
    modk = mod_mine[:, 2 * na:].reshape(1, 2 * D) + kv_ada_b.reshape(1, 2 * D)
    sh1a, sc1a, g1a, sh2a, sc2a, g2a = jnp.split(mod0, 6, axis=1)
    sh1b, sc1b, g1b, sh2b, sc2b, g2b = jnp.split(mod1, 6, axis=1)
    kv_sh, kv_sc = jnp.split(modk, 2, axis=1)
    nmix0, nmix1, nmlp0, nmlp1 = norm_mix[0:1], norm_mix[1:2], norm_mlp[0:1], norm_mlp[1:2]
    kvn = kv_norm.reshape(1, D)
    lb, lb_vjp = jax.vjp(_lower_bound, lb_logits)

    w_in = _all_gather(a_w_in[0].astype(BF16), "gather_a_w_in", axis=1)
    w_names = ["a_w_out", "mlp0_w1", "mlp0_w2", "w_kv", "b_w_q", "b_w_out", "mlp1_w1", "mlp1_w2"]
    w_shards = [a_w_out[0], mlp_w1[0], mlp_w2[0], w_kv, b_w_q[0], b_w_out[0], mlp_w1[1], mlp_w2[1]]
    w_axes = [0, 1, 0, 0, 0, 0, 1, 0]
    w_shards = [s.astype(BF16) for s in w_shards]
    w_routes = [_gather_route(s, ax) for s, ax in zip(w_shards, w_axes)]
    w_handles, w_token = _copies_start(w_shards, w_routes, w_in, "gather_weights_start")

    def weight(key, after):
        i = w_names.index(key)
        return _copies_wait(w_handles[i], w_routes[i], after, "gather_" + key + "_wait")

    x0 = x[0]
    h1 = _norm_mod_fwd(x0, nmix0, sc1a + w_token[0:1, 0:1], sh1a, "a_in_norm")
    w_in = _heads_major(w_in, "a_w_in_by_head")
    (proj,) = _matmul(h1, w_in, "nn", (F32,), "a_in_mm")
    og, oraw, states = _hgrn2_fwd(proj, lb, out_gain, "hgrn2_fwd")
    w_aout = weight("a_w_out", og)
    x1, y1, h2 = _matmul(og, w_aout, "nn", (F32, BF16, BF16), "a_out_mm", _residual_epilogue, (x0,),
                         (g1a, nmlp0, sc2a, sh2a), whole_rows=True)
    w1a = weight("mlp0_w1", h2)
    u0, a0 = _matmul(h2, w1a, "nn", (BF16, BF16), "mlp0_up_mm", _relu2_epilogue)
    w2a = weight("mlp0_w2", u0)
    x2, y2 = _matmul(u0, w2a, "nn", (F32, BF16), "mlp0_down_mm", _residual_epilogue, (x1,), (g2a,))

    hk = _norm_mod_fwd(x2, kvn, kv_sc, kv_sh, "kv_norm")
    w_kvf = weight("w_kv", hk)
    (kv,) = _matmul(hk, w_kvf, "nn", (BF16,), "kv_mm")
    h3 = _norm_mod_fwd(x2, nmix1, sc1b, sh1b, "b_q_norm")
    w_q = weight("b_w_q", h3)
    (q,) = _matmul(h3, w_q, "nn", (BF16,), "b_q_mm")
    o, w_tiles, b_tiles = _sb_fwd(q, kv, "attn_fwd")
    w_bout = weight("b_w_out", o)
    x3, y3, h4 = _matmul(o, w_bout, "nn", (F32, BF16, BF16), "b_out_mm", _residual_epilogue, (x2,),
                         (g1b, nmlp1, sc2b, sh2b), whole_rows=True)
    w1b = weight("mlp1_w1", h4)
    u1, a1 = _matmul(h4, w1b, "nn", (BF16, BF16), "mlp1_up_mm", _relu2_epilogue)
    w2b = weight("mlp1_w2", u1)
    x4, y4 = _matmul(u1, w2b, "nn", (F32, BF16), "mlp1_down_mm", _residual_epilogue, (x3,), (g2b,))

    loss_vec, d4, d_final, dy4, dg2b = _final_loss(x4, final_norm.reshape(1, D), loss_target[0], y4, g2b, "final_loss")
    loss = lax.psum(loss_vec[0, 0], MESH_AXES)

    sent = {}

    def send(key, g, axis):
        route = _scatter_route(g, axis)
        (handle,), token = _copies_start([g], [route], None, "scatter_" + key + "_start")
        sent[key] = (handle, route)
        return token

    def behind(vec, *tokens):
        for token in tokens:
            vec = vec + token[0:1, 0:1]
        return vec

    def mlp_bwd(tag, d, dy, u, a, w2, w1, h, x_in, gain, sc, y_in, gate_in):
        (dw2,) = _matmul(u, dy, "tn", (BF16,), tag + "_down_dw")
        token2 = send(tag + "_w2", dw2, 0)
        (dz,) = _matmul(dy, w2, "nt", (BF16,), tag + "_down_dz", lambda acc, a_tile: (acc * (2.0 * a_tile),), (a,))
        (dw1,) = _matmul(h, dz, "tn", (BF16,), tag + "_up_dw")
        token1 = send(tag + "_w1", dw1, 1)
        (dh,) = _matmul(dz, w1, "nt", (F32,), tag + "_up_dh")
        return _norm_mod_bwd(x_in, gain, behind(sc, token2, token1), dh, d, tag + "_up_dnorm", y_in, gate_in)

    d3, dnmlp1, dsc2b, dsh2b, dy3, dg1b = mlp_bwd("mlp1", d4, dy4, u1, a1, w2b, w1b, h4, x3, nmlp1, sc2b, y3, g1b)

    (dw_bout,) = _matmul(o, dy3, "tn", (BF16,), "b_out_dw")
    token = send("b_w_out", dw_bout, 0)
    (do,) = _matmul(dy3, w_bout, "nt", (BF16,), "b_out_du")
    dq, dk, dv = _sb_bwd(q, kv, w_tiles, b_tiles, do, token, "attn_bwd")
    dkv = jnp.concatenate([dk, dv], axis=1).astype(BF16)
    (dw_q,) = _matmul(h3, dq, "tn", (BF16,), "b_q_dw")
    token = send("b_w_q", dw_q, 0)
    (dh3,) = _matmul(dq, w_q, "nt", (F32,), "b_q_dh")
    d2, dnmix1, dsc1b, dsh1b = _norm_mod_bwd(x2, nmix1, behind(sc1b, token), dh3, d3, "b_q_dnorm")
    (dw_kv,) = _matmul(hk, dkv, "tn", (BF16,), "kv_dw")
    token = send("w_kv", dw_kv, 0)
    (dhk,) = _matmul(dkv, w_kvf, "nt", (F32,), "kv_dh")
    d2, dkvn, dkv_sc, dkv_sh, dy2, dg2a = _norm_mod_bwd(x2, kvn, behind(kv_sc, token), dhk, d2, "kv_dnorm", y2, g2a)

    d1, dnmlp0, dsc2a, dsh2a, dy1, dg1a = mlp_bwd("mlp0", d2, dy2, u0, a0, w2a, w1a, h2, x1, nmlp0, sc2a, y1, g1a)

    (dw_aout,) = _matmul(og, dy1, "tn", (BF16,), "a_out_dw")
    token = send("a_w_out", dw_aout, 0)
    (dog,) = _matmul(dy1, w_aout, "nt", (BF16,), "a_out_du")
    dproj, dlb, d_out_gain = _hgrn2_bwd(proj, behind(lb, token), out_gain, oraw, states, dog, "hgrn2_bwd")
    (dw_in,) = _matmul(h1, dproj, "tn", (BF16,), "a_in_dw")
    token = send("a_w_in", _heads_minor(dw_in, "a_w_in_grad_by_part"), 1)
    (dh1,) = _matmul(dproj, w_in, "nt", (F32,), "a_in_dh")
    d0, dnmix0, dsc1a, dsh1a = _norm_mod_bwd(x0, nmix0, behind(sc1a, token), dh1, d1, "a_in_dnorm")
    (d_lb_logits,) = lb_vjp(dlb)

    dmod0 = jnp.concatenate([dsh1a, dsc1a, dg1a, dsh2a, dsc2a, dg2a], axis=1)
    dmod1 = jnp.concatenate([dsh1b, dsc1b, dg1b, dsh2b, dsc2b, dg2b], axis=1)
    dmodk = jnp.concatenate([dkv_sh, dkv_sc], axis=1)
    pieces = [dmod0, dmod1, dmodk, dnmix0, dnmix1, dnmlp0, dnmlp1, dkvn, d_final,
              d_lb_logits.reshape(1, 2 * D), d_out_gain]
    widths = [p.shape[1] for p in pieces]
    offs = [sum(widths[:i]) for i in range(len(widths))]
    part_all = _all_gather(jnp.concatenate(pieces, axis=1), "gather_dsmall")[:, 0, :]
    take = lambda i: part_all[:, offs[i]:offs[i] + widths[i]]
    dmod0_all, dmod1_all, dmodk_all = take(0), take(1), take(2)

    outs = {}

    def update(key, parts, w, m, v, **kw):
        outs[key] = _adamw(parts, w, m, v, "adamw_" + key + ("_%d" % kw["layer"] if "layer" in kw else ""), **kw)

    c_act_cols = _pad_rows(c_act.astype(BF16), 128).T
    my_cols = lambda a, n: _pad_rows(lax.dynamic_slice_in_dim(a, me * n, n, axis=1).astype(BF16), 128)
    g_ada0 = _matmul(c_act_cols, my_cols(dmod0_all, na), "nn", (F32,), "dada0")[0]
    g_ada1 = _matmul(c_act_cols, my_cols(dmod1_all, na), "nn", (F32,), "dada1")[0]
    g_adak = _matmul(c_act_cols, my_cols(dmodk_all, nk), "nn", (F32,), "dada_kv")[0]
    update("ada_w", g_ada1[None], ada_w, m_ada_w, v_ada_w, layer=1)
    update("ada_w", g_ada0[None], ada_w, m_ada_w, v_ada_w, layer=0, filled=outs["ada_w"])
    update("kv_ada_w", g_adak[None], kv_ada_w, m_kv_ada_w, v_kv_ada_w)
    update("ada_b", jnp.stack([dmod0_all, dmod1_all], axis=1), ada_b, m_ada_b, v_ada_b)
    update("kv_ada_b", dmodk_all, kv_ada_b, m_kv_ada_b, v_kv_ada_b)
    update("norm_mix", jnp.stack([take(3), take(4)], axis=1), norm_mix, m_norm_mix, v_norm_mix)
    update("norm_mlp", jnp.stack([take(5), take(6)], axis=1), norm_mlp, m_norm_mlp, v_norm_mlp)
    update("kv_norm", take(7), kv_norm, m_kv_norm, v_kv_norm)
    update("final_norm", take(8), final_norm, m_final_norm, v_final_norm)
    d_lb_all = take(9).reshape(N_DEV, 2, D)
    update("a_lb_logits", lax.dynamic_slice_in_dim(d_lb_all, me * dl, dl, axis=2), a_lb_logits, m_a_lb_logits,
           v_a_lb_logits)
    update("a_out_gain", lax.dynamic_slice_in_dim(take(10), me * dl, dl, axis=1)[:, None, :], a_out_gain,
           m_a_out_gain, v_a_out_gain)

    def landed(key, after):
        handle, route = sent[key]
        return _copies_wait(handle, route, after, "scatter_" + key + "_wait")

    update("mlp_w2", landed("mlp1_w2", d0), mlp_w2, m_mlp_w2, v_mlp_w2, layer=1)
    update("mlp_w1", landed("mlp1_w1", d0), mlp_w1, m_mlp_w1, v_mlp_w1, layer=1)
    update("b_w_out", landed("b_w_out", d0), b_w_out, m_b_w_out, v_b_w_out)
    update("b_w_q", landed("b_w_q", d0), b_w_q, m_b_w_q, v_b_w_q)
    update("w_kv", landed("w_kv", d0), w_kv, m_w_kv, v_w_kv)
    update("mlp_w2", landed("mlp0_w2", d0), mlp_w2, m_mlp_w2, v_mlp_w2, layer=0, filled=outs["mlp_w2"])
    update("mlp_w1", landed("mlp0_w1", d0), mlp_w1, m_mlp_w1, v_mlp_w1, layer=0, filled=outs["mlp_w1"])
    update("a_w_out", landed("a_w_out", d0), a_w_out, m_a_w_out, v_a_w_out)
    update("a_w_in", landed("a_w_in", outs["mlp_w1"][0]), a_w_in, m_a_w_in, v_a_w_in)

    names = ["ada_w", "ada_b", "norm_mix", "norm_mlp", "a_w_in", "a_lb_logits", "a_out_gain", "a_w_out", "kv_ada_w",
             "kv_ada_b", "kv_norm", "w_kv", "b_w_q", "b_w_out", "mlp_w1", "mlp_w2", "final_norm"]
    result = [loss, d0[None]]
    for field in range(4):
        result += [outs[n][field] for n in names]
    return tuple(result)
```

```python
import jax
import jax.numpy as jnp
from jax import lax
from jax.experimental import pallas as pl
from jax.experimental.pallas import tpu as pltpu

F32 = jnp.float32
BF16 = jnp.bfloat16

N_DEV = 8
MESH_AXES = ("x", "y", "c")
HEAD_DIM = 128
A_CHUNK = 16
HGRN2_UNROLL = 8
HGRN2_GROUP = 128
HGRN2_BLOCK = 2048
SB_TILE = 256
NORM_EPS = 1e-6
ADAM_LR = 0.001
ADAM_B1 = 0.9
ADAM_B2 = 0.999
ADAM_EPS = 1e-08
ADAM_WD = 0.01
ADAM_STEP = 10
VMEM_LIMIT = 56 * 1024 * 1024
MATMUL_VMEM = 40 * 1024 * 1024


def _pick(dim, target, align):
    t = (min(dim, target) // align) * align
    while t >= align:
        if dim % t == 0:
            return t
        t -= align
    return dim


def _sigmoid(x):
    return 1.0 / (1.0 + jnp.exp(-x))


def _silu(x):
    return x * _sigmoid(x)


def _dsilu(x):
    s = _sigmoid(x)
    return s * (1.0 + x * (1.0 - s))


def _dot(a, b, dims):
    return lax.dot_general(a, b, (dims, ((), ())), preferred_element_type=F32)


def _dot_nn(a, b):
    return _dot(a, b, ((1,), (0,)))


def _dot_nt(a, b):
    return _dot(a, b, ((1,), (1,)))


def _dot_tn(a, b):
    return _dot(a, b, ((0,), (0,)))


def _mesh_pos():
    return lax.axis_index("x"), lax.axis_index("y"), lax.axis_index("c")


def _flip(v, d):
    return 1 - v if d else v


def _all_gather(x, name, axis=None):
    if axis is None:
        out_shape, place = (N_DEV,) + x.shape, lambda d: (d,)
    else:
        route = _gather_route(x, axis)
        out_shape, place = route.land_shape, route.dst_slice

    def body(x_ref, out_ref, send_sems, recv_sems, local_sem):
        x, y, c = _mesh_pos()
        me, sibling = (x, y, c), (x, y, 1 - c)
        chips = [(1 - x, y), (x, 1 - y), (1 - x, 1 - y)]

        def rows(px, py, pc):
            return out_ref.at[place(4 * px + 2 * py + pc)]

        def copy(k, block, to, src=None):
            return pltpu.make_async_remote_copy(
                src_ref=rows(*block) if src is None else src,
                dst_ref=rows(*block),
                send_sem=send_sems.at[k],
                recv_sem=recv_sems.at[k],
                device_id=to,
                device_id_type=pl.DeviceIdType.MESH,
            )

        mine = pltpu.make_async_copy(x_ref, rows(*me), local_sem)
        mine.start()
        first = [copy(0, me, sibling, src=x_ref)]
        first += [copy(1 + j, me, (*chip, c), src=x_ref) for j, chip in enumerate(chips)]
        for cp in first:
            cp.start()
        passed = [copy(4 + j, (*chip, c), sibling) for j, chip in enumerate(chips)]
        for j, chip in enumerate(chips):
            copy(1 + j, (*chip, c), me).wait_recv()
            passed[j].start()
        copy(0, sibling, me).wait_recv()
        for j, chip in enumerate(chips):
            copy(4 + j, (*chip, 1 - c), me).wait_recv()
        for cp in first + passed:
            cp.wait_send()
        mine.wait()

    return pl.pallas_call(
        body,
        name=name,
        out_shape=jax.ShapeDtypeStruct(out_shape, x.dtype),
        in_specs=[pl.BlockSpec(memory_space=pl.ANY)],
        out_specs=pl.BlockSpec(memory_space=pl.ANY),
        scratch_shapes=[
            pltpu.SemaphoreType.DMA((7,)),
            pltpu.SemaphoreType.DMA((7,)),
            pltpu.SemaphoreType.DMA(()),
        ],
    )(x)


_RELATIONS = [(dx, dy, dc) for dx in (0, 1) for dy in (0, 1) for dc in (0, 1) if (dx, dy, dc) != (0, 0, 0)]
_HBM = pl.BlockSpec(memory_space=pltpu.HBM)
_SEM = pl.BlockSpec(memory_space=pltpu.SEMAPHORE)
_EFFECT = pltpu.SideEffectType.DATAFLOW_SIDE_EFFECTING


def _at(ref, idx):
    return ref.at[idx] if idx else ref


def _block(axis, ndim, n):
    return lambda d: (slice(None),) * axis + (pl.ds(d * n, n),) + (slice(None),) * (ndim - axis - 1)


class _Route:
    def __init__(self, land_shape, src_slice, dst_slice):
        self.land_shape, self.src_slice, self.dst_slice = tuple(land_shape), src_slice, dst_slice


def _gather_route(shard, axis):
    n = shard.shape[axis]
    shape = shard.shape[:axis] + (N_DEV * n,) + shard.shape[axis + 1:]
    return _Route(shape, lambda p: (), _block(axis, shard.ndim, n))


def _scatter_route(g, axis):
    n = g.shape[axis] // N_DEV
    shape = (N_DEV,) + g.shape[:axis] + (n,) + g.shape[axis + 1:]
    return _Route(shape, _block(axis, g.ndim, n), lambda p: (p,))


def _peers():
    x, y, c = _mesh_pos()
    out = []
    for k, (dx, dy, dc) in enumerate(_RELATIONS):
        px, py, pc = _flip(x, dx), _flip(y, dy), _flip(c, dc)
        out.append((k, (px, py, pc), 4 * px + 2 * py + pc))
    return 4 * x + 2 * y + c, out


_LOCAL = len(_RELATIONS)


def _copies_start(srcs, routes, after, name):
    n = len(srcs)
    order = [] if after is None else [after]

    def body(*refs):
        src_refs, land_refs = refs[:n], refs[n:2 * n]
        outs = refs[2 * n + len(order):]
        send, recv, token = outs[:n], outs[n:2 * n], outs[-1]
        me, peers = _peers()
        for i, route in enumerate(routes):
            for k, peer, pid in peers:
                pltpu.make_async_remote_copy(
                    src_ref=_at(src_refs[i], route.src_slice(pid)), dst_ref=_at(land_refs[i], route.dst_slice(me)),
                    send_sem=send[i].at[k], recv_sem=recv[i].at[k],
                    device_id=peer, device_id_type=pl.DeviceIdType.MESH).start()
            pltpu.make_async_copy(_at(src_refs[i], route.src_slice(me)), _at(land_refs[i], route.dst_slice(me)),
                                  send[i].at[_LOCAL]).start()
        token[...] = jnp.zeros_like(token)

    lands = [lax.empty(r.land_shape, s.dtype) for s, r in zip(srcs, routes)]
    hbm = lambda a: pltpu.with_memory_space_constraint(a, pltpu.HBM)
    res = pl.pallas_call(
        body, name=name,
        out_shape=([pltpu.SemaphoreType.DMA((_LOCAL + 1,))] * n + [pltpu.SemaphoreType.DMA((_LOCAL,))] * n
                   + [pltpu.HBM(s.shape, s.dtype) for s in srcs]
                   + [pltpu.HBM(l.shape, l.dtype) for l in lands]
                   + [jax.ShapeDtypeStruct((8, 128), F32)]),
        in_specs=[_HBM] * (2 * n) + [pl.BlockSpec(memory_space=pl.ANY)] * len(order),
        out_specs=[_SEM] * (2 * n) + [_HBM] * (2 * n) + [pl.BlockSpec(memory_space=pltpu.VMEM)],
        input_output_aliases={i: 2 * n + i for i in range(2 * n)},
        compiler_params=pltpu.CompilerParams(has_side_effects=_EFFECT),
    )(*[hbm(s) for s in srcs], *[hbm(l) for l in lands], *order)
    handles = [(res[i], res[n + i], res[2 * n + i], res[3 * n + i]) for i in range(n)]
    return handles, res[-1]


def _copies_wait(handle, route, after, name):
    send_sems, recv_sems, src, land = handle

    def body(src_ref, land_ref, send_ref, recv_ref, after_ref, src_out, got_ref):
        me, peers = _peers()
        for k, peer, pid in peers:
            cp = pltpu.make_async_remote_copy(
                src_ref=_at(src_ref, route.src_slice(pid)), dst_ref=_at(land_ref, route.dst_slice(pid)),
                send_sem=send_ref.at[k], recv_sem=recv_ref.at[k],
                device_id=peer, device_id_type=pl.DeviceIdType.MESH)
            cp.wait_send()
            cp.wait_recv()
        pltpu.make_async_copy(_at(src_ref, route.src_slice(me)), _at(land_ref, route.dst_slice(me)),
                              send_ref.at[_LOCAL]).wait()

    return pl.pallas_call(
        body, name=name,
        out_shape=(pltpu.HBM(src.shape, src.dtype), pltpu.HBM(land.shape, land.dtype)),
        in_specs=[_HBM, _HBM, _SEM, _SEM, pl.BlockSpec(memory_space=pl.ANY)],
        out_specs=(_HBM, _HBM),
        input_output_aliases={0: 0, 1: 1},
        compiler_params=pltpu.CompilerParams(has_side_effects=_EFFECT),
    )(src, land, send_sems, recv_sems, after)[1]


def _matmul_tiles(M, N, K, a_item, b_item, mn_bytes, whole_rows):
    def fit(tk):
        tm = _pick(M, 512, 128)
        tn = N if whole_rows else _pick(N, 2048, 128)
        while True:
            cast = (tm * tk * 2 if a_item != 2 else 0) + (tk * tn * 2 if b_item != 2 else 0)
            need = 2 * (tm * tk * a_item + tk * tn * b_item + tm * tn * mn_bytes) + 2 * tm * tn * 4 + cast
            if whole_rows:
                smaller = _pick(M, tm - 128, 128) if tm > 128 else tm
                if need <= MATMUL_VMEM or smaller >= tm:
                    return tm, tn, tk
                tm = smaller
            else:
                smaller = _pick(N, tn - 128, 128) if tn > 128 else tn
                if need <= MATMUL_VMEM or smaller >= tn:
                    return tm, tn, tk
                tn = smaller

    shallow, deep = fit(_pick(K, 2048, 128)), fit(_pick(K, 4096, 128))
    return deep if deep[:2] == shallow[:2] else shallow


def _matmul(a, b, mode, out_dtypes, name, epilogue=None, tiles=(), rows=(), whole_rows=False):
    if mode == "nn":
        (M, K), (K2, N) = a.shape, b.shape
    elif mode == "nt":
        (M, K), (N, K2) = a.shape, b.shape
    else:
        (K, M), (K2, N) = a.shape, b.shape
    assert K == K2, (a.shape, b.shape, mode)
    mn_bytes = sum(t.dtype.itemsize for t in tiles) + sum(jnp.dtype(d).itemsize for d in out_dtypes)
    tm, tn, tk = _matmul_tiles(M, N, K, a.dtype.itemsize, b.dtype.itemsize, mn_bytes, whole_rows)
    nm, nn, nk = M // tm, N // tn, K // tk
    n_extra = len(tiles) + len(rows)
    n_out = len(out_dtypes)
    if epilogue is None:
        epilogue = lambda acc: (acc,)
    dims = {"nn": ((1,), (0,)), "nt": ((1,), (1,)), "tn": ((0,), (0,))}[mode]

    def finish(acc, extra, outs):
        vals = epilogue(acc, *[r[...] for r in extra])
        for o_ref, v in zip(outs, vals):
            o_ref[...] = v.astype(o_ref.dtype)

    def kern(*refs):
        a_ref, b_ref = refs[0], refs[1]
        extra = refs[2:2 + n_extra]
        outs = refs[2 + n_extra:2 + n_extra + n_out]
        part = _dot(a_ref[...].astype(BF16), b_ref[...].astype(BF16), dims)
        if nk == 1:
            finish(part, extra, outs)
            return
        acc_ref = refs[-1]
        k = pl.program_id(2)

        @pl.when(k == 0)
        def _():
            acc_ref[...] = part

        @pl.when(k > 0)
        def _():
            acc_ref[...] += part

        @pl.when(k == nk - 1)
        def _():
            finish(acc_ref[...], extra, outs)

    a_bytes, b_bytes = a.size * a.dtype.itemsize, b.size * b.dtype.itemsize
    m_outer = a_bytes + nm * b_bytes <= nn * a_bytes + b_bytes
    ij = (lambda g0, g1: (g0, g1)) if m_outer else (lambda g0, g1: (g1, g0))

    def spec(shape, fn):
        return pl.BlockSpec(shape, lambda g0, g1, k: fn(*ij(g0, g1), k))

    a_spec = {"nn": spec((tm, tk), lambda i, j, k: (i, k)),
              "nt": spec((tm, tk), lambda i, j, k: (i, k)),
              "tn": spec((tk, tm), lambda i, j, k: (k, i))}[mode]
    b_spec = {"nn": spec((tk, tn), lambda i, j, k: (k, j)),
              "nt": spec((tn, tk), lambda i, j, k: (j, k)),
              "tn": spec((tk, tn), lambda i, j, k: (k, j))}[mode]
    tile_spec = spec((tm, tn), lambda i, j, k: (i, j))
    row_spec = spec((1, tn), lambda i, j, k: (0, j))
    return pl.pallas_call(
        kern,
        name=name,
        grid=(nm, nn, nk) if m_outer else (nn, nm, nk),
        in_specs=[a_spec, b_spec] + [tile_spec] * len(tiles) + [row_spec] * len(rows),
        out_specs=[tile_spec] * n_out,
        out_shape=[jax.ShapeDtypeStruct((M, N), dt) for dt in out_dtypes],
        scratch_shapes=[pltpu.VMEM((tm, tn), F32)] if nk > 1 else [],
        compiler_params=pltpu.CompilerParams(
            dimension_semantics=("parallel", "parallel", "arbitrary"), vmem_limit_bytes=VMEM_LIMIT),
    )(a, b, *tiles, *rows)


def _relu2_epilogue(acc):
    a = jnp.maximum(acc, 0.0)
    return a * a, a


def _modulated_norm(x, gain, sc, sh):
    inv = lax.rsqrt(jnp.mean(x * x, axis=-1, keepdims=True) + NORM_EPS)
    return (x * inv) * gain * (1.0 + sc) + sh


def _residual_epilogue(acc, x_tile, gate_row, *norm_rows):
    x_new = x_tile + gate_row * acc
    normed = [_modulated_norm(x_new, *norm_rows[i:i + 3]) for i in range(0, len(norm_rows), 3)]
    return (x_new, acc, *normed)


def _norm_mod_fwd(x, gain, sc, sh, name):
    S, D = x.shape
    ts = _pick(S, 512, 16)

    def kern(x_ref, g_ref, sc_ref, sh_ref, h_ref):
        h_ref[...] = _modulated_norm(x_ref[...], g_ref[...], sc_ref[...], sh_ref[...]).astype(h_ref.dtype)

    row = pl.BlockSpec((1, D), lambda i: (0, 0))
    blk = pl.BlockSpec((ts, D), lambda i: (i, 0))
    return pl.pallas_call(
        kern, name=name, grid=(S // ts,),
        in_specs=[blk, row, row, row], out_specs=blk,
        out_shape=jax.ShapeDtypeStruct((S, D), BF16),
        compiler_params=pltpu.CompilerParams(dimension_semantics=("parallel",), vmem_limit_bytes=VMEM_LIMIT),
    )(x, gain, sc, sh)


def _through_gate(d, y_ref, gate_ref, dy_ref, dgate_ref):
    dy_ref[...] = (gate_ref[...] * d).astype(dy_ref.dtype)
    dgate_ref[...] += jnp.sum(d * y_ref[...].astype(F32), axis=0, keepdims=True)


def _norm_mod_bwd(x, gain, sc, dh, d_in, name, y=None, gate=None):
    S, D = x.shape
    ts = _pick(S, 512, 16)
    gated = y is not None

    def kern(x_ref, g_ref, sc_ref, dh_ref, din_ref, *rest):
        dx_ref, dg_ref, dsc_ref, dsh_ref = rest[-6:-2] if gated else rest

        @pl.when(pl.program_id(0) == 0)
        def _():
            dg_ref[...] = jnp.zeros_like(dg_ref)
            dsc_ref[...] = jnp.zeros_like(dsc_ref)
            dsh_ref[...] = jnp.zeros_like(dsh_ref)
            if gated:
                rest[-1][...] = jnp.zeros_like(rest[-1])

        xv = x_ref[...]
        dh = dh_ref[...].astype(F32)
        inv = lax.rsqrt(jnp.mean(xv * xv, axis=-1, keepdims=True) + NORM_EPS)
        xn = xv * inv
        g = g_ref[...]
        dsh_ref[...] += jnp.sum(dh, axis=0, keepdims=True)
        dsc_ref[...] += jnp.sum(dh * (xn * g), axis=0, keepdims=True)
        dn = dh * (1.0 + sc_ref[...])
        dg_ref[...] += jnp.sum(dn * xn, axis=0, keepdims=True)
        dxn = dn * g
        d = din_ref[...] + inv * (dxn - xn * jnp.mean(dxn * xn, axis=-1, keepdims=True))
        dx_ref[...] = d
        if gated:
            _through_gate(d, rest[0], rest[1], rest[-2], rest[-1])

    row = pl.BlockSpec((1, D), lambda i: (0, 0))
    blk = pl.BlockSpec((ts, D), lambda i: (i, 0))
    vec = jax.ShapeDtypeStruct((1, D), F32)
    full = lambda dt: jax.ShapeDtypeStruct((S, D), dt)
    return pl.pallas_call(
        kern, name=name, grid=(S // ts,),
        in_specs=[blk, row, row, blk, blk] + ([blk, row] if gated else []),
        out_specs=[blk, row, row, row] + ([blk, row] if gated else []),
        out_shape=[full(F32), vec, vec, vec] + ([full(BF16), vec] if gated else []),
        compiler_params=pltpu.CompilerParams(dimension_semantics=("arbitrary",), vmem_limit_bytes=VMEM_LIMIT),
    )(x, gain, sc, dh, d_in, *((y, gate) if gated else ()))


def _final_loss(x, gain, target, y, gate, name):
    S, D = x.shape
    ts = _pick(S, 512, 16)

    def kern(x_ref, g_ref, t_ref, y_ref, gate_ref, loss_ref, dx_ref, dg_ref, dy_ref, dgate_ref):
        @pl.when(pl.program_id(0) == 0)
        def _():
            loss_ref[...] = jnp.zeros_like(loss_ref)
            dg_ref[...] = jnp.zeros_like(dg_ref)
            dgate_ref[...] = jnp.zeros_like(dgate_ref)

        xv = x_ref[...]
        g = g_ref[...]
        inv = lax.rsqrt(jnp.mean(xv * xv, axis=-1, keepdims=True) + NORM_EPS)
        xn = xv * inv
        err = xn * g - t_ref[...]
        row_loss = jnp.mean(err * err, axis=-1, keepdims=True)
        loss_ref[...] += 0.5 * jnp.sum(row_loss, axis=0, keepdims=True)
        dout = err * (1.0 / D)
        dg_ref[...] += jnp.sum(dout * xn, axis=0, keepdims=True)
        dxn = dout * g
        d = inv * (dxn - xn * jnp.mean(dxn * xn, axis=-1, keepdims=True))
        dx_ref[...] = d
        _through_gate(d, y_ref, gate_ref, dy_ref, dgate_ref)

    row = pl.BlockSpec((1, D), lambda i: (0, 0))
    blk = pl.BlockSpec((ts, D), lambda i: (i, 0))
    vec = jax.ShapeDtypeStruct((1, D), F32)
    return pl.pallas_call(
        kern, name=name, grid=(S // ts,),
        in_specs=[blk, row, blk, blk, row],
        out_specs=[pl.BlockSpec((1, 128), lambda i: (0, 0)), blk, row, blk, row],
        out_shape=[jax.ShapeDtypeStruct((1, 128), F32), jax.ShapeDtypeStruct((S, D), F32), vec,
                   jax.ShapeDtypeStruct((S, D), BF16), vec],
        compiler_params=pltpu.CompilerParams(dimension_semantics=("arbitrary",), vmem_limit_bytes=VMEM_LIMIT),
    )(x, gain, target, y, gate)


def _regroup_columns(w, outer, inner, name):
    rows = w.shape[0]

    def kern(i_ref, o_ref):
        o_ref[...] = i_ref[...]

    return pl.pallas_call(
        kern, name=name, grid=(outer, inner),
        in_specs=[pl.BlockSpec((rows, HEAD_DIM), lambda a, b: (0, a * inner + b))],
        out_specs=pl.BlockSpec((rows, HEAD_DIM), lambda a, b: (0, b * outer + a)),
        out_shape=jax.ShapeDtypeStruct(w.shape, w.dtype),
        compiler_params=pltpu.CompilerParams(
            dimension_semantics=("parallel", "parallel"), vmem_limit_bytes=VMEM_LIMIT),
    )(w)


def _heads_major(w, name):
    return _regroup_columns(w, 4, w.shape[1] // (4 * HEAD_DIM), name)


def _heads_minor(w, name):
    return _regroup_columns(w, w.shape[1] // (4 * HEAD_DIM), 4, name)


def _part(ref, j, rows=slice(None)):
    return ref[rows, j * HEAD_DIM:(j + 1) * HEAD_DIM]


def _chunk_scan(v, ric, reverse):
    n = v.shape[0]
    d = 1
    while d < A_CHUNK:
        if reverse:
            v = v + jnp.where(ric < A_CHUNK - d, pltpu.roll(v, n - d, 0), 0.0)
        else:
            v = v + jnp.where(ric >= d, pltpu.roll(v, d, 0), 0.0)
        d *= 2
    return v


def _hgrn2_gates(q_raw, f_logit, lbv, ric):
    qs = _silu(q_raw)
    sig = _sigmoid(f_logit)
    f = lbv + (1.0 - lbv) * sig
    log_f = jnp.log(f)
    kk = (1.0 - lbv) * _sigmoid(-f_logit)
    cum = _chunk_scan(log_f, ric, False)
    cl = cum + _chunk_scan(log_f, ric, True) - log_f
    e_cum = jnp.exp(cum)
    e_neg = jnp.exp(-cum)
    e_end = jnp.exp(cl - cum)
    dec = jnp.exp(cl)
    return qs, sig, f, kk, e_cum, e_neg, e_end, dec


def _same_chunk_mask():
    r = lax.broadcasted_iota(jnp.int32, (HGRN2_GROUP, HGRN2_GROUP), 0)
    c = lax.broadcasted_iota(jnp.int32, (HGRN2_GROUP, HGRN2_GROUP), 1)
    return (r // A_CHUNK == c // A_CHUNK) & (r >= c)


def _hgrn2_specs(S, D, TB, reverse):
    H = D // HEAD_DIM
    NB = S // TB
    pos = (lambda nb: NB - 1 - nb) if reverse else (lambda nb: nb)
    proj = pl.BlockSpec((TB, 4 * HEAD_DIM), lambda h, nb: (pos(nb), h))
    head = pl.BlockSpec((TB, HEAD_DIM), lambda h, nb: (pos(nb), h))
    vec = pl.BlockSpec((1, HEAD_DIM), lambda h, nb: (0, h))
    state = pl.BlockSpec((TB // A_CHUNK, None, HEAD_DIM, HEAD_DIM), lambda h, nb: (pos(nb), h, 0, 0))
    return H, NB, proj, head, vec, state


def _hgrn2_fwd(proj, lb, gain, name):
    S, D4 = proj.shape
    D = D4 // 4
    TB = _pick(S, HGRN2_BLOCK, HGRN2_GROUP)
    H, NB, pspec, head, vec, state = _hgrn2_specs(S, D, TB, False)
    NCB = TB // A_CHUNK

    def kern(p_ref, lb_ref, gain_ref, og_ref, oraw_ref, st_ref, a_s, b_s, k_s, v_s, dec_s, o_s, st_s):
        @pl.when(pl.program_id(1) == 0)
        def _():
            st_s[...] = jnp.zeros_like(st_s)

        ric = lax.broadcasted_iota(jnp.int32, (TB, HEAD_DIM), 0) % A_CHUNK
        qs, _, _, kk, e_cum, e_neg, e_end, dec = _hgrn2_gates(_part(p_ref, 0), _part(p_ref, 1), lb_ref[...], ric)
        a_s[...] = (qs * e_cum).astype(BF16)
        b_s[...] = (kk * e_neg).astype(BF16)
        k_s[...] = (kk * e_end).astype(BF16)
        v_s[...] = _part(p_ref, 2).astype(BF16)
        dec_s[...] = dec

        same_chunk = _same_chunk_mask()
        for gi in range(TB // HGRN2_GROUP):
            rows = pl.ds(gi * HGRN2_GROUP, HGRN2_GROUP)
            p = jnp.where(same_chunk, _dot_nt(a_s[rows, :], b_s[rows, :]), 0.0).astype(BF16)
            o_s[rows, :] = _dot_nn(p, v_s[rows, :])

        def chunks(it, st):
            for u in range(HGRN2_UNROLL):
                ci = it * HGRN2_UNROLL + u
                r = pl.multiple_of(ci * A_CHUNK, A_CHUNK)
                rows = pl.ds(r, A_CHUNK)
                st_bf = st.astype(BF16)
                st_ref[ci] = st_bf
                o_s[rows, :] += _dot_nt(a_s[rows, :], st_bf)
                st = dec_s[pl.ds(r, 1), :] * st + _dot_tn(v_s[rows, :], k_s[rows, :])
            return st

        st_s[...] = lax.fori_loop(0, NCB // HGRN2_UNROLL, chunks, st_s[...])
        o = o_s[...]
        oraw_ref[...] = o
        on = o * lax.rsqrt(jnp.mean(o * o, axis=-1, keepdims=True) + NORM_EPS)
        og_ref[...] = ((on * gain_ref[...]) * _silu(_part(p_ref, 3))).astype(og_ref.dtype)

    tb_bf = pltpu.VMEM((TB, HEAD_DIM), BF16)
    tb_f = pltpu.VMEM((TB, HEAD_DIM), F32)
    return pl.pallas_call(
        kern, name=name, grid=(H, NB),
        in_specs=[pspec, vec, vec],
        out_specs=[head, head, state],
        out_shape=[jax.ShapeDtypeStruct((S, D), BF16), jax.ShapeDtypeStruct((S, D), F32),
                   jax.ShapeDtypeStruct((S // A_CHUNK, H, HEAD_DIM, HEAD_DIM), BF16)],
        scratch_shapes=[tb_bf, tb_bf, tb_bf, tb_bf, tb_f, tb_f, pltpu.VMEM((HEAD_DIM, HEAD_DIM), F32)],
        compiler_params=pltpu.CompilerParams(
            dimension_semantics=("parallel", "arbitrary"), vmem_limit_bytes=VMEM_LIMIT),
    )(proj, lb, gain)


def _hgrn2_bwd(proj, lb, gain, oraw, states, dog, name):
    S, D4 = proj.shape
    D = D4 // 4
    TB = _pick(S, HGRN2_BLOCK, HGRN2_GROUP)
    H, NB, pspec, head, vec, state = _hgrn2_specs(S, D, TB, True)
    NCB = TB // A_CHUNK

    def kern(p_ref, lb_ref, gain_ref, oraw_ref, st_ref, dog_ref, dp_ref, dlb_ref, dgain_ref,
             a_s, b_s, k_s, v_s, do_s, dec_s, da_s, db_s, dk_s, dv_s, ddec_s, dst_s):
        @pl.when(pl.program_id(1) == 0)
        def _():
            dst_s[...] = jnp.zeros_like(dst_s)
            dlb_ref[...] = jnp.zeros_like(dlb_ref)
            dgain_ref[...] = jnp.zeros_like(dgain_ref)

        ric = lax.broadcasted_iota(jnp.int32, (TB, HEAD_DIM), 0) % A_CHUNK
        lbv = lb_ref[...]
        q_raw = _part(p_ref, 0)
        qs, sig, f, kk, e_cum, e_neg, e_end, dec = _hgrn2_gates(q_raw, _part(p_ref, 1), lbv, ric)
        a32, b32, k32 = qs * e_cum, kk * e_neg, kk * e_end
        a_s[...] = a32.astype(BF16)
        b_s[...] = b32.astype(BF16)
        k_s[...] = k32.astype(BF16)
        v_s[...] = _part(p_ref, 2).astype(BF16)
        dec_s[...] = dec

        o = oraw_ref[...]
        gain_v = gain_ref[...]
        g_raw = _part(p_ref, 3)
        rinv = lax.rsqrt(jnp.mean(o * o, axis=-1, keepdims=True) + NORM_EPS)
        on = o * rinv
        dog_v = dog_ref[...].astype(F32)
        dp_ref[:, 3 * HEAD_DIM:4 * HEAD_DIM] = (dog_v * (on * gain_v) * _dsilu(g_raw)).astype(dp_ref.dtype)
        dog2 = dog_v * _silu(g_raw)
        dgain_ref[...] += jnp.sum(dog2 * on, axis=0, keepdims=True)
        don = dog2 * gain_v
        do_s[...] = (rinv * (don - on * jnp.mean(don * on, axis=-1, keepdims=True))).astype(BF16)

        same_chunk = _same_chunk_mask()
        for gi in range(TB // HGRN2_GROUP):
            rows = pl.ds(gi * HGRN2_GROUP, HGRN2_GROUP)
            a, b, do = a_s[rows, :], b_s[rows, :], do_s[rows, :]
            p = jnp.where(same_chunk, _dot_nt(a, b), 0.0).astype(BF16)
            dp = jnp.where(same_chunk, _dot_nt(do, v_s[rows, :]), 0.0).astype(BF16)
            dv_s[rows, :] = _dot_tn(p, do)
            da_s[rows, :] = _dot_nn(dp, b)
            db_s[rows, :] = _dot_tn(dp, a)

        def chunks(it, dst):
            for u in range(HGRN2_UNROLL):
                ci = NCB - 1 - (it * HGRN2_UNROLL + u)
                r = pl.multiple_of(ci * A_CHUNK, A_CHUNK)
                rows = pl.ds(r, A_CHUNK)
                do = do_s[rows, :]
                st_prev = st_ref[ci]
                dst_bf = dst.astype(BF16)
                dv_s[rows, :] += _dot_nt(k_s[rows, :], dst_bf)
                da_s[rows, :] += _dot_nn(do, st_prev)
                dk_s[rows, :] = _dot_nn(v_s[rows, :], dst_bf)
                ddec = jnp.sum(dst * st_prev.astype(F32), axis=0, keepdims=True)
                ddec_s[rows, :] = jnp.broadcast_to(ddec, (A_CHUNK, HEAD_DIM))
                dst = dec_s[pl.ds(r, 1), :] * dst + _dot_tn(do, a_s[rows, :])
            return dst

        dst_s[...] = lax.fori_loop(0, NCB // HGRN2_UNROLL, chunks, dst_s[...])

        da, db, dk = da_s[...], db_s[...], dk_s[...]
        dqs = da * e_cum
        dkk = db * e_neg + dk * e_end
        w = dk * k32
        dlog_f = (_chunk_scan(da * a32 - db * b32, ric, True) + (_chunk_scan(w, ric, False) - w)
                  + ddec_s[...] * dec)
        dfg = dlog_f / f - dkk
        dlb_ref[...] += jnp.sum(dfg * (1.0 - sig), axis=0, keepdims=True)
        dp_ref[:, 0:HEAD_DIM] = (dqs * _dsilu(q_raw)).astype(dp_ref.dtype)
        dp_ref[:, HEAD_DIM:2 * HEAD_DIM] = (dfg * (1.0 - lbv) * sig * (1.0 - sig)).astype(dp_ref.dtype)
        dp_ref[:, 2 * HEAD_DIM:3 * HEAD_DIM] = dv_s[...].astype(dp_ref.dtype)

    tb_bf = pltpu.VMEM((TB, HEAD_DIM), BF16)
    tb_f = pltpu.VMEM((TB, HEAD_DIM), F32)
    vec_shape = jax.ShapeDtypeStruct((1, D), F32)
    return pl.pallas_call(
        kern, name=name, grid=(H, NB),
        in_specs=[pspec, vec, vec, head, state, head],
        out_specs=[pspec, vec, vec],
        out_shape=[jax.ShapeDtypeStruct((S, D4), BF16), vec_shape, vec_shape],
        scratch_shapes=[tb_bf, tb_bf, tb_bf, tb_bf, tb_bf, tb_f, tb_f, tb_f, tb_f, tb_f, tb_f,
                        pltpu.VMEM((HEAD_DIM, HEAD_DIM), F32)],
        compiler_params=pltpu.CompilerParams(
            dimension_semantics=("parallel", "arbitrary"), vmem_limit_bytes=VMEM_LIMIT),
    )(proj, lb, gain, oraw, states, dog)


def _with_ones(tri):
    return jnp.concatenate([tri, jnp.ones_like(tri)], axis=1)


def _stack(ref, G):
    return jnp.concatenate([ref[:, g * HEAD_DIM:(g + 1) * HEAD_DIM] for g in range(G)], axis=0)


def _unstack(v, G):
    return jnp.concatenate([v[g * SB_TILE:(g + 1) * SB_TILE, :] for g in range(G)], axis=1)


def _sb_specs(S, D, KVH):
    G = D // HEAD_DIM // KVH
    qblk = pl.BlockSpec((SB_TILE, G * HEAD_DIM), lambda h, qi: (qi, h))
    kblk = pl.BlockSpec((S, HEAD_DIM), lambda h, qi: (0, h))
    vblk = pl.BlockSpec((S, HEAD_DIM), lambda h, qi: (0, KVH + h))
    return G, qblk, kblk, vblk


def _tile_index(qi, j):
    return qi * (qi + 1) // 2 + j


def _sb_fwd(q, kv, name):
    S, D = q.shape
    KVH = kv.shape[1] // (2 * HEAD_DIM)
    G, qblk, kblk, vblk = _sb_specs(S, D, KVH)
    R = G * SB_TILE
    NQ = S // SB_TILE
    scale = HEAD_DIM ** -0.5

    def kern(q_ref, k_ref, v_ref, o_ref, w_hbm, b_hbm, acc_s, run_s, w_stage, b_stage, sems):
        head, qi = pl.program_id(0), pl.program_id(1)
        qs = _stack(q_ref, G)
        row = lax.broadcasted_iota(jnp.int32, (R, SB_TILE), 0) % SB_TILE
        colm = lax.broadcasted_iota(jnp.int32, (R, SB_TILE), 1)
        mask = colm < row
        ti = lax.broadcasted_iota(jnp.int32, (SB_TILE, SB_TILE), 0)
        tj = lax.broadcasted_iota(jnp.int32, (SB_TILE, SB_TILE), 1)
        after = _with_ones((ti > tj).astype(BF16))
        acc_s[...] = jnp.zeros_like(acc_s)
        run_s[...] = jnp.zeros_like(run_s)

        def save(slot, j):
            idx = _tile_index(qi, j)
            return (pltpu.make_async_copy(w_stage.at[slot], w_hbm.at[head, idx], sems.at[0, slot]),
                    pltpu.make_async_copy(b_stage.at[slot], b_hbm.at[head, idx], sems.at[1, slot]))

        def tile(j, slot, masked):
            ks = pl.ds(pl.multiple_of(j * SB_TILE, SB_TILE), SB_TILE)
            kj, vj = k_ref[ks, :], v_ref[ks, :]
            z = _dot_nt(qs, kj) * scale
            t = jnp.exp(-jnp.abs(z))
            one_t = 1.0 + t
            log_beta = jnp.minimum(z, 0.0) - jnp.log(one_t)
            log_rest = log_beta - z
            if masked:
                log_rest = jnp.where(mask, log_rest, 0.0)
            sums = _dot_nn(log_rest.astype(BF16), after)
            between = sums[:, :SB_TILE] + run_s[...]
            w = jnp.exp(log_beta + between)
            if masked:
                w = jnp.where(mask, w, 0.0)
            w_bf = w.astype(BF16)
            acc_s[...] += _dot_nn(w_bf, vj)
            run_s[...] += sums[:, SB_TILE:]
            w_stage[slot] = w_bf
            b_stage[slot] = (jnp.where(z >= 0.0, 1.0, t) * pl.reciprocal(one_t, approx=True)).astype(BF16)
            for cp in save(slot, j):
                cp.start()

        tile(qi, 0, True)

        def body(it, carry):
            slot = (it + 1) % 2

            @pl.when(it >= 1)
            def _():
                for cp in save(slot, 0):
                    cp.wait()

            tile(qi - 1 - it, slot, False)
            return carry

        lax.fori_loop(0, qi, body, 0)
        o_ref[...] = _unstack(acc_s[...], G).astype(o_ref.dtype)
        for cp in save(qi % 2, 0):
            cp.wait()

        @pl.when(qi >= 1)
        def _():
            for cp in save((qi + 1) % 2, 0):
                cp.wait()

    tiles = jax.ShapeDtypeStruct((KVH, NQ * (NQ + 1) // 2, R, SB_TILE), BF16)
    stage = pltpu.VMEM((2, R, SB_TILE), BF16)
    return pl.pallas_call(
        kern, name=name, grid=(KVH, NQ),
        in_specs=[qblk, kblk, vblk],
        out_specs=[qblk, pl.BlockSpec(memory_space=pl.ANY), pl.BlockSpec(memory_space=pl.ANY)],
        out_shape=[jax.ShapeDtypeStruct((S, D), BF16), tiles, tiles],
        scratch_shapes=[pltpu.VMEM((R, HEAD_DIM), F32), pltpu.VMEM((R, SB_TILE), F32), stage, stage,
                        pltpu.SemaphoreType.DMA((2, 2))],
        compiler_params=pltpu.CompilerParams(
            dimension_semantics=("parallel", "arbitrary"), vmem_limit_bytes=VMEM_LIMIT),
    )(q, kv, kv)


def _sb_bwd(q, kv, w_tiles, b_tiles, do, after, name):
    S, D = q.shape
    KVH = kv.shape[1] // (2 * HEAD_DIM)
    G, qblk, kblk, vblk = _sb_specs(S, D, KVH)
    R = G * SB_TILE
    scale = HEAD_DIM ** -0.5

    def kern(q_ref, k_ref, v_ref, w_hbm, b_hbm, do_ref, after_ref, dq_ref, dk_ref, dv_ref,
             dq_s, esum_s, w_stage, b_stage, sems):
        head, qi = pl.program_id(0), pl.program_id(1)

        @pl.when(qi == 0)
        def _():
            dk_ref[...] = jnp.zeros_like(dk_ref)
            dv_ref[...] = jnp.zeros_like(dv_ref)

        qs = _stack(q_ref, G)
        dos = _stack(do_ref, G)
        row = lax.broadcasted_iota(jnp.int32, (R, SB_TILE), 0) % SB_TILE
        colm = lax.broadcasted_iota(jnp.int32, (R, SB_TILE), 1)
        mask = colm < row
        ti = lax.broadcasted_iota(jnp.int32, (SB_TILE, SB_TILE), 0)
        tj = lax.broadcasted_iota(jnp.int32, (SB_TILE, SB_TILE), 1)
        before = (ti < tj).astype(BF16)
        dq_s[...] = jnp.zeros_like(dq_s)
        esum_s[...] = jnp.zeros_like(esum_s)

        def fetch(slot, j):
            idx = _tile_index(qi, j)
            return (pltpu.make_async_copy(w_hbm.at[head, idx], w_stage.at[slot], sems.at[0, slot]),
                    pltpu.make_async_copy(b_hbm.at[head, idx], b_stage.at[slot], sems.at[1, slot]))

        def tile(j, slot, masked):
            for cp in fetch(slot, j):
                cp.wait()
            ks = pl.ds(pl.multiple_of(j * SB_TILE, SB_TILE), SB_TILE)
            kj, vj = k_ref[ks, :], v_ref[ks, :]
            w_bf = w_stage[slot]
            beta = b_stage[slot].astype(F32)
            e = _dot_nt(dos, vj) * w_bf.astype(F32)
            e_before = esum_s[...] + _dot_nn(e.astype(BF16), before)
            dz = e - beta * (e + e_before)
            if masked:
                dz = jnp.where(mask, dz, 0.0)
            dzs = (dz * scale).astype(BF16)
            dq_s[...] += _dot_nn(dzs, kj)
            dk_ref[ks, :] += _dot_tn(dzs, qs)
            dv_ref[ks, :] += _dot_tn(w_bf, dos)
            esum_s[...] += jnp.sum(e, axis=1, keepdims=True)

        for cp in fetch(0, 0):
            cp.start()

        def body(j, carry):
            for cp in fetch((j + 1) % 2, j + 1):
                cp.start()
            tile(j, j % 2, False)
            return carry

        lax.fori_loop(0, qi, body, 0)
        tile(qi, qi % 2, True)
        dq_ref[...] = _unstack(dq_s[...], G).astype(dq_ref.dtype)

    kvout = pl.BlockSpec((S, HEAD_DIM), lambda h, qi: (0, h))
    stage = pltpu.VMEM((2, R, SB_TILE), BF16)
    hbm = pl.BlockSpec(memory_space=pl.ANY)
    return pl.pallas_call(
        kern, name=name, grid=(KVH, S // SB_TILE),
        in_specs=[qblk, kblk, vblk, hbm, hbm, qblk, pl.BlockSpec(after.shape, lambda h, qi: (0, 0))],
        out_specs=[qblk, kvout, kvout],
        out_shape=[jax.ShapeDtypeStruct((S, D), BF16), jax.ShapeDtypeStruct((S, KVH * HEAD_DIM), F32),
                   jax.ShapeDtypeStruct((S, KVH * HEAD_DIM), F32)],
        scratch_shapes=[pltpu.VMEM((R, HEAD_DIM), F32), pltpu.VMEM((R, SB_TILE), F32), stage, stage,
                        pltpu.SemaphoreType.DMA((2, 2))],
        compiler_params=pltpu.CompilerParams(
            dimension_semantics=("parallel", "arbitrary"), vmem_limit_bytes=VMEM_LIMIT),
    )(q, kv, kv, w_tiles, b_tiles, do, after)


def _adamw(parts, w, m, v, name, layer=None, filled=None):
    shape = w.shape
    L = 1 if layer is None else shape[0]
    l = 0 if layer is None else layer
    C = shape[-1]
    R = w.size // (C * L)
    P = parts.shape[0]
    parts3 = parts.reshape(P, R, C)
    w3, m3, v3 = w.reshape(L, R, C), m.reshape(L, R, C), v.reshape(L, R, C)
    tr = _pick(R, max(8, (1 << 18) // C), 16)

    def kern(p_ref, w_ref, m_ref, v_ref, *rest):
        g_ref, d_ref, nm_ref, nv_ref = rest[-4:]
        g = p_ref[0].astype(F32)
        for i in range(1, P):
            g = g + p_ref[i].astype(F32)
        nm = ADAM_B1 * m_ref[...] + (1.0 - ADAM_B1) * g
        nv = ADAM_B2 * v_ref[...] + (1.0 - ADAM_B2) * (g * g)
        m_hat = nm / (1.0 - ADAM_B1 ** ADAM_STEP)
        v_hat = nv / (1.0 - ADAM_B2 ** ADAM_STEP)
        g_ref[...] = g
        d_ref[...] = -ADAM_LR * (m_hat / (jnp.sqrt(v_hat) + ADAM_EPS) + ADAM_WD * w_ref[...])
        nm_ref[...] = nm
        nv_ref[...] = nv

    blk = pl.BlockSpec((None, tr, C), lambda i: (l, i, 0))
    out = jax.ShapeDtypeStruct((L, R, C), F32)
    extra = [] if filled is None else [f.reshape(L, R, C) for f in filled]
    res = pl.pallas_call(
        kern, name=name, grid=(R // tr,),
        in_specs=[pl.BlockSpec((P, tr, C), lambda i: (0, i, 0)), blk, blk, blk]
        + [pl.BlockSpec(memory_space=pl.ANY)] * len(extra),
        out_specs=[blk, blk, blk, blk], out_shape=[out, out, out, out],
        input_output_aliases={4 + j: j for j in range(len(extra))},
        compiler_params=pltpu.CompilerParams(dimension_semantics=("parallel",), vmem_limit_bytes=VMEM_LIMIT),
    )(parts3, w3, m3, v3, *extra)
    return tuple(r.reshape(shape) for r in res)


def _lower_bound(logits):
    return jnp.cumsum(jax.nn.softmax(logits.astype(F32), axis=0), axis=0)[0:1]


def _pad_rows(a, rows):
    return jnp.zeros((rows, a.shape[1]), a.dtype).at[:a.shape[0]].set(a)


def kernel(x, c, ada_w, ada_b, norm_mix, norm_mlp, a_w_in, a_lb_logits, a_out_gain, a_w_out, kv_ada_w, kv_ada_b, kv_norm, w_kv, b_w_q, b_w_out, mlp_w1, mlp_w2, final_norm, loss_target, m_ada_w, m_ada_b, m_norm_mix, m_norm_mlp, m_a_w_in, m_a_lb_logits, m_a_out_gain, m_a_w_out, m_kv_ada_w, m_kv_ada_b, m_kv_norm, m_w_kv, m_b_w_q, m_b_w_out, m_mlp_w1, m_mlp_w2, m_final_norm, v_ada_w, v_ada_b, v_norm_mix, v_norm_mlp, v_a_w_in, v_a_lb_logits, v_a_out_gain, v_a_w_out, v_kv_ada_w, v_kv_ada_b, v_kv_norm, v_w_kv, v_b_w_q, v_b_w_out, v_mlp_w1, v_mlp_w2, v_final_norm):
    S, D = x.shape[1], x.shape[2]
    assert x.shape[0] == 1 and ada_w.shape[0] == 2 and a_w_in.shape[0] == 1 and b_w_q.shape[0] == 1
    me = 4 * lax.axis_index("x") + 2 * lax.axis_index("y") + lax.axis_index("c")
    dl = D // N_DEV
    na = ada_w.shape[2]
    nk = kv_ada_w.shape[1]

    small = jnp.concatenate([c.reshape(1, D), a_lb_logits.reshape(1, 2 * dl), a_out_gain.reshape(1, dl)], axis=1)
    small_all = _all_gather(small, "gather_small")[:, 0, :]
    c_all = small_all[:, :D]
    lb_logits = small_all[:, D:D + 2 * dl].reshape(N_DEV, 2, dl).transpose(1, 0, 2).reshape(2, D)
    out_gain = small_all[:, D + 2 * dl:].reshape(1, D)

    c_act = jax.nn.silu(c_all)
    c_act_rows = _pad_rows(c_act.astype(BF16), 128)
    mod_cols = jnp.concatenate([
        _matmul(c_act_rows, ada_w[0].astype(BF16), "nn", (F32,), "ada0")[0][:N_DEV],
        _matmul(c_act_rows, ada_w[1].astype(BF16), "nn", (F32,), "ada1")[0][:N_DEV],
        _matmul(c_act_rows, kv_ada_w.astype(BF16), "nn", (F32,), "ada_kv")[0][:N_DEV]], axis=1)
    mod_all = _all_gather(mod_cols, "gather_mod")
    mod_mine = lax.dynamic_index_in_dim(mod_all, me, axis=1, keepdims=False)
    mod0 = mod_mine[:, :na].reshape(1, 6 * D) + ada_b[0:1]
    mod1 = mod_mine[:, na:2 * na].reshape(1, 6 * D) + ada_b[1:2]
    modk = mod_mine[:, 2 * na:].reshape(1, 2 * D) + kv_ada_b.reshape(1, 2 * D)
    sh1a, sc1a, g1a, sh2a, sc2a, g2a = jnp.split(mod0, 6, axis=1)
    sh1b, sc1b, g1b, sh2b, sc2b, g2b = jnp.split(mod1, 6, axis=1)
    kv_sh, kv_sc = jnp.split(modk, 2, axis=1)
    nmix0, nmix1, nmlp0, nmlp1 = norm_mix[0:1], norm_mix[1:2], norm_mlp[0:1], norm_mlp[1:2]
    kvn = kv_norm.reshape(1, D)
    lb, lb_vjp = jax.vjp(_lower_bound, lb_logits)

    w_in = _all_gather(a_w_in[0].astype(BF16), "gather_a_w_in", axis=1)
    w_names = ["a_w_out", "mlp0_w1", "mlp0_w2", "w_kv", "b_w_q", "b_w_out", "mlp1_w1", "mlp1_w2"]
    w_shards = [a_w_out[0], mlp_w1[0], mlp_w2[0], w_kv, b_w_q[0], b_w_out[0], mlp_w1[1], mlp_w2[1]]
    w_axes = [0, 1, 0, 0, 0, 0, 1, 0]
    w_shards = [s.astype(BF16) for s in w_shards]
    w_routes = [_gather_route(s, ax) for s, ax in zip(w_shards, w_axes)]
    w_handles, w_token = _copies_start(w_shards, w_routes, w_in, "gather_weights_start")

    def weight(key, after):
        i = w_names.index(key)
        return _copies_wait(w_handles[i], w_routes[i], after, "gather_" + key + "_wait")

    x0 = x[0]
    h1 = _norm_mod_fwd(x0, nmix0, sc1a + w_token[0:1, 0:1], sh1a, "a_in_norm")
    w_in = _heads_major(w_in, "a_w_in_by_head")
    (proj,) = _matmul(h1, w_in, "nn", (F32,), "a_in_mm")
    og, oraw, states = _hgrn2_fwd(proj, lb, out_gain, "hgrn2_fwd")
    w_aout = weight("a_w_out", og)
    x1, y1, h2 = _matmul(og, w_aout, "nn", (F32, BF16, BF16), "a_out_mm", _residual_epilogue, (x0,),
                         (g1a, nmlp0, sc2a, sh2a), whole_rows=True)
    w1a = weight("mlp0_w1", h2)
    u0, a0 = _matmul(h2, w1a, "nn", (BF16, BF16), "mlp0_up_mm", _relu2_epilogue)
    w2a = weight("mlp0_w2", u0)
    x2, y2 = _matmul(u0, w2a, "nn", (F32, BF16), "mlp0_down_mm", _residual_epilogue, (x1,), (g2a,))

    hk = _norm_mod_fwd(x2, kvn, kv_sc, kv_sh, "kv_norm")
    w_kvf = weight("w_kv", hk)
    (kv,) = _matmul(hk, w_kvf, "nn", (BF16,), "kv_mm")
    h3 = _norm_mod_fwd(x2, nmix1, sc1b, sh1b, "b_q_norm")
    w_q = weight("b_w_q", h3)
    (q,) = _matmul(h3, w_q, "nn", (BF16,), "b_q_mm")
    o, w_tiles, b_tiles = _sb_fwd(q, kv, "attn_fwd")
    w_bout = weight("b_w_out", o)
    x3, y3, h4 = _matmul(o, w_bout, "nn", (F32, BF16, BF16), "b_out_mm", _residual_epilogue, (x2,),
                         (g1b, nmlp1, sc2b, sh2b), whole_rows=True)
    w1b = weight("mlp1_w1", h4)
    u1, a1 = _matmul(h4, w1b, "nn", (BF16, BF16), "mlp1_up_mm", _relu2_epilogue)
    w2b = weight("mlp1_w2", u1)
    x4, y4 = _matmul(u1, w2b, "nn", (F32, BF16), "mlp1_down_mm", _residual_epilogue, (x3,), (g2b,))

    loss_vec, d4, d_final, dy4, dg2b = _final_loss(x4, final_norm.reshape(1, D), loss_target[0], y4, g2b, "final_loss")
    loss = lax.psum(loss_vec[0, 0], MESH_AXES)

    sent = {}

    def send(key, g, axis):
        route = _scatter_route(g, axis)
        (handle,), token = _copies_start([g], [route], None, "scatter_" + key + "_start")
        sent[key] = (handle, route)
        return token

    def behind(vec, *tokens):
        for token in tokens:
            vec = vec + token[0:1, 0:1]
        return vec

    def mlp_bwd(tag, d, dy, u, a, w2, w1, h, x_in, gain, sc, y_in, gate_in):
        (dw2,) = _matmul(u, dy, "tn", (BF16,), tag + "_down_dw")
        token2 = send(tag + "_w2", dw2, 0)
        (dz,) = _matmul(dy, w2, "nt", (BF16,), tag + "_down_dz", lambda acc, a_tile: (acc * (2.0 * a_tile),), (a,))
        (dw1,) = _matmul(h, dz, "tn", (BF16,), tag + "_up_dw")
        token1 = send(tag + "_w1", dw1, 1)
        (dh,) = _matmul(dz, w1, "nt", (F32,), tag + "_up_dh")
        return _norm_mod_bwd(x_in, gain, behind(sc, token2, token1), dh, d, tag + "_up_dnorm", y_in, gate_in)

    d3, dnmlp1, dsc2b, dsh2b, dy3, dg1b = mlp_bwd("mlp1", d4, dy4, u1, a1, w2b, w1b, h4, x3, nmlp1, sc2b, y3, g1b)

    (dw_bout,) = _matmul(o, dy3, "tn", (BF16,), "b_out_dw")
    token = send("b_w_out", dw_bout, 0)
    (do,) = _matmul(dy3, w_bout, "nt", (BF16,), "b_out_du")
    dq, dk, dv = _sb_bwd(q, kv, w_tiles, b_tiles, do, token, "attn_bwd")
    dkv = jnp.concatenate([dk, dv], axis=1).astype(BF16)
    (dw_q,) = _matmul(h3, dq, "tn", (BF16,), "b_q_dw")
    token = send("b_w_q", dw_q, 0)
    (dh3,) = _matmul(dq, w_q, "nt", (F32,), "b_q_dh")
    d2, dnmix1, dsc1b, dsh1b = _norm_mod_bwd(x2, nmix1, behind(sc1b, token), dh3, d3, "b_q_dnorm")
    (dw_kv,) = _matmul(hk, dkv, "tn", (BF16,), "kv_dw")
    token = send("w_kv", dw_kv, 0)
    (dhk,) = _matmul(dkv, w_kvf, "nt", (F32,), "kv_dh")
    d2, dkvn, dkv_sc, dkv_sh, dy2, dg2a = _norm_mod_bwd(x2, kvn, behind(kv_sc, token), dhk, d2, "kv_dnorm", y2, g2a)

    d1, dnmlp0, dsc2a, dsh2a, dy1, dg1a = mlp_bwd("mlp0", d2, dy2, u0, a0, w2a, w1a, h2, x1, nmlp0, sc2a, y1, g1a)

    (dw_aout,) = _matmul(og, dy1, "tn", (BF16,), "a_out_dw")
    token = send("a_w_out", dw_aout, 0)
    (dog,) = _matmul(dy1, w_aout, "nt", (BF16,), "a_out_du")
    dproj, dlb, d_out_gain = _hgrn2_bwd(proj, behind(lb, token), out_gain, oraw, states, dog, "hgrn2_bwd")
    (dw_in,) = _matmul(h1, dproj, "tn", (BF16,), "a_in_dw")
    token = send("a_w_in", _heads_minor(dw_in, "a_w_in_grad_by_part"), 1)
    (dh1,) = _matmul(dproj, w_in, "nt", (F32,), "a_in_dh")
    d0, dnmix0, dsc1a, dsh1a = _norm_mod_bwd(x0, nmix0, behind(sc1a, token), dh1, d1, "a_in_dnorm")
    (d_lb_logits,) = lb_vjp(dlb)

    dmod0 = jnp.concatenate([dsh1a, dsc1a, dg1a, dsh2a, dsc2a, dg2a], axis=1)
    dmod1 = jnp.concatenate([dsh1b, dsc1b, dg1b, dsh2b, dsc2b, dg2b], axis=1)
    dmodk = jnp.concatenate([dkv_sh, dkv_sc], axis=1)
    pieces = [dmod0, dmod1, dmodk, dnmix0, dnmix1, dnmlp0, dnmlp1, dkvn, d_final,
              d_lb_logits.reshape(1, 2 * D), d_out_gain]
    widths = [p.shape[1] for p in pieces]
    offs = [sum(widths[:i]) for i in range(len(widths))]
    part_all = _all_gather(jnp.concatenate(pieces, axis=1), "gather_dsmall")[:, 0, :]
    take = lambda i: part_all[:, offs[i]:offs[i] + widths[i]]
    dmod0_all, dmod1_all, dmodk_all = take(0), take(1), take(2)

    outs = {}

    def update(key, parts, w, m, v, **kw):
        outs[key] = _adamw(parts, w, m, v, "adamw_" + key + ("_%d" % kw["layer"] if "layer" in kw else ""), **kw)

    c_act_cols = _pad_rows(c_act.astype(BF16), 128).T
    my_cols = lambda a, n: _pad_rows(lax.dynamic_slice_in_dim(a, me * n, n, axis=1).astype(BF16), 128)
    g_ada0 = _matmul(c_act_cols, my_cols(dmod0_all, na), "nn", (F32,), "dada0")[0]
    g_ada1 = _matmul(c_act_cols, my_cols(dmod1_all, na), "nn", (F32,), "dada1")[0]
    g_adak = _matmul(c_act_cols, my_cols(dmodk_all, nk), "nn", (F32,), "dada_kv")[0]
    update("ada_w", g_ada1[None], ada_w, m_ada_w, v_ada_w, layer=1)
    update("ada_w", g_ada0[None], ada_w, m_ada_w, v_ada_w, layer=0, filled=outs["ada_w"])
    update("kv_ada_w", g_adak[None], kv_ada_w, m_kv_ada_w, v_kv_ada_w)
    update("ada_b", jnp.stack([dmod0_all, dmod1_all], axis=1), ada_b, m_ada_b, v_ada_b)
    update("kv_ada_b", dmodk_all, kv_ada_b, m_kv_ada_b, v_kv_ada_b)
    update("norm_mix", jnp.stack([take(3), take(4)], axis=1), norm_mix, m_norm_mix, v_norm_mix)
    update("norm_mlp", jnp.stack([take(5), take(6)], axis=1), norm_mlp, m_norm_mlp, v_norm_mlp)
    update("kv_norm", take(7), kv_norm, m_kv_norm, v_kv_norm)
    update("final_norm", take(8), final_norm, m_final_norm, v_final_norm)
    d_lb_all = take(9).reshape(N_DEV, 2, D)
    update("a_lb_logits", lax.dynamic_slice_in_dim(d_lb_all, me * dl, dl, axis=2), a_lb_logits, m_a_lb_logits,
           v_a_lb_logits)
    update("a_out_gain", lax.dynamic_slice_in_dim(take(10), me * dl, dl, axis=1)[:, None, :], a_out_gain,
           m_a_out_gain, v_a_out_gain)

    def landed(key, after):
        handle, route = sent[key]
        return _copies_wait(handle, route, after, "scatter_" + key + "_wait")

    update("mlp_w2", landed("mlp1_w2", d0), mlp_w2, m_mlp_w2, v_mlp_w2, layer=1)
    update("mlp_w1", landed("mlp1_w1", d0), mlp_w1, m_mlp_w1, v_mlp_w1, layer=1)
    update("b_w_out", landed("b_w_out", d0), b_w_out, m_b_w_out, v_b_w_out)
    update("b_w_q", landed("b_w_q", d0), b_w_q, m_b_w_q, v_b_w_q)
    update("w_kv", landed("w_kv", d0), w_kv, m_w_kv, v_w_kv)
    update("mlp_w2", landed("mlp0_w2", d0), mlp_w2, m_mlp_w2, v_mlp_w2, layer=0, filled=outs["mlp_w2"])
    update("mlp_w1", landed("mlp0_w1", d0), mlp_w1, m_mlp_w1, v_mlp_w1, layer=0, filled=outs["mlp_w1"])
    update("a_w_out", landed("a_w_out", d0), a_w_out, m_a_w_out, v_a_w_out)
    update("a_w_in", landed("a_w_in", outs["mlp_w1"][0]), a_w_in, m_a_w_in, v_a_w_in)

    names = ["ada_w", "ada_b", "norm_mix", "norm_mlp", "a_w_in", "a_lb_logits", "a_out_gain", "a_w_out", "kv_ada_w",
             "kv_ada_b", "kv_norm", "w_kv", "b_w_q", "b_w_out", "mlp_w1", "mlp_w2", "final_norm"]
    result = [loss, d0[None]]
    for field in range(4):
        result += [outs[n][field] for n in names]
    return tuple(result)
```
